```python
import numpy as np
import jax
import jax.numpy as jnp
from jax import lax

D_MODEL = 4096
BATCH = 1
SEQ = 8192
DEPTH = 2
DEC_BATCH = 8
DEC_SEQ = 16
PAST_LEN = 1024

CHUNK = 64
HEAD_DIM = 128
H_RET = D_MODEL // (2 * HEAD_DIM)
RET_W = H_RET * HEAD_DIM
RET_THETA = 10000.0
H_DSA = D_MODEL // (2 * HEAD_DIM)
KVH_DSA = 4
DSA_W = H_DSA * HEAD_DIM
KV_W = KVH_DSA * HEAD_DIM
H_IDX = 32
D_IDX = 64
TOPK_MAX = 256
QBLOCK = 128
ROPE_THETA = 500000.0
ROPE_DIM = HEAD_DIM // 4
IDX_ROPE_DIM = D_IDX // 4
H_CHK = D_MODEL // HEAD_DIM
CHK_W = H_CHK * HEAD_DIM
LEFT_CHUNKS = 8
REL_CLIP = 128
D_FF = 14336
N_EXPERTS = 8
TOP_K = 2
MOE_BLOCK = 512
D_PLE = 256
LN_EPS = 1e-5
ALPHA = (2.0 * DEPTH) ** 0.25
BETA = (8.0 * DEPTH) ** -0.25
N_EVEN = (DEPTH + 1) // 2
N_ODD = DEPTH // 2
EVEN_SPLITS = (RET_W, RET_W, RET_W, RET_W, DSA_W, KV_W, KV_W, H_IDX * D_IDX, D_IDX, H_IDX)
IN_EVEN = sum(EVEN_SPLITS)

kernel_name = 'hybrid_streaming_encoder_step'


def layer_norm(x, g, b):
    xf = x.astype(jnp.float32)
    mu = jnp.mean(xf, -1, keepdims=True)
    var = jnp.mean(jnp.square(xf - mu), -1, keepdims=True)
    return ((xf - mu) * lax.rsqrt(var + LN_EPS) * g + b).astype(x.dtype)


def rotary(x, pos, rot_dim, theta):
    half = rot_dim // 2
    inv = theta ** (-jnp.arange(half, dtype=jnp.float32) / half)
    ang = pos.astype(jnp.float32)[:, None] * inv[None, :]
    cos = jnp.cos(ang)[:, None, :]
    sin = jnp.sin(ang)[:, None, :]
    xr = x[..., :rot_dim].astype(jnp.float32)
    x1, x2 = xr[..., :half], xr[..., half:]
    rot = jnp.concatenate([x1 * cos - x2 * sin, x2 * cos + x1 * sin], -1).astype(x.dtype)
    return jnp.concatenate([rot, x[..., rot_dim:]], -1)


def retention(q, k, v, state0):
    B, T, H, Dk = q.shape
    Dv = v.shape[-1]
    c = min(CHUNK, T)
    n = T // c
    lg = jnp.log1p(-(2.0 ** (-5.0 - jnp.arange(H, dtype=jnp.float32))))
    idx = jnp.arange(c, dtype=jnp.float32)
    diff = idx[:, None] - idx[None, :]
    intra = jnp.where(diff >= 0, jnp.exp(lg[:, None, None] * jnp.maximum(diff, 0.0)), 0.0)
    q_dec = jnp.exp(lg[None, :] * (idx[:, None] + 1.0))
    k_dec = jnp.exp(lg[None, :] * (c - 1.0 - idx[:, None]))
    c_dec = jnp.exp(lg * c)

    def to_chunks(a):
        return a.astype(jnp.float32).reshape(B, n, c, H, a.shape[-1]).swapaxes(0, 1)

    qc, kc, vc = to_chunks(q), to_chunks(k) * Dk ** -0.5, to_chunks(v)

    def step(S, inp):
        qi, ki, vi = inp
        s = jnp.einsum('bnhd,bmhd->bhnm', qi, ki) * intra[None]
        o = (jnp.einsum('bhnm,bmhe->bnhe', s, vi)
             + jnp.einsum('bnhd,bhde->bnhe', qi, S) * q_dec[None, :, :, None])
        S = S * c_dec[None, :, None, None] + jnp.einsum('bmhd,bmhe->bhde', ki * k_dec[None, :, :, None], vi)
        return S, o

    S, o = lax.scan(step, state0.astype(jnp.float32), (qc, kc, vc))
    return o.swapaxes(0, 1).reshape(B, T, H, Dv), S


def dsa_attention(q, k_all, v_all, iq, ik_all, iw, q_pos, k_pos, n_sel):
    B, T, H, Dh = q.shape
    G = H // KVH_DSA
    qb = min(QBLOCK, T)
    nb = T // qb
    ik_f = ik_all.astype(jnp.float32)
    k_chunk = k_pos // CHUNK
    bidx = jnp.arange(B)[:, None, None]

    def blocks(a):
        return a.reshape(B, nb, qb, *a.shape[2:]).swapaxes(0, 1)

    def one_block(args):
        qx, iqx, iwx, qp = args
        s_idx = jax.nn.relu(jnp.einsum('bthd,bsd->bths', iqx.astype(jnp.float32), ik_f))
        score = jnp.einsum('bths,bth->bts', s_idx, iwx)
        ok = k_chunk[None, :] <= (qp // CHUNK)[:, None]
        score = jnp.where(ok[None], score, -jnp.inf)
        top_s, top_i = lax.top_k(score, n_sel)
        valid = top_s > -jnp.inf
        k_sel = k_all[bidx, top_i]
        v_sel = v_all[bidx, top_i]
        logits = jnp.einsum('btkgd,btskd->btkgs', qx.reshape(B, qb, KVH_DSA, G, Dh), k_sel)
        logits = logits.astype(jnp.float32) * Dh ** -0.5
        logits = jnp.where(valid[:, :, None, None, :], logits, -jnp.inf)
        pr = jax.nn.softmax(logits, axis=-1).astype(v_sel.dtype)
        return jnp.einsum('btkgs,btskd->btkgd', pr, v_sel).reshape(B, qb, H, Dh)

    o = lax.map(one_block, (blocks(q), blocks(iq), blocks(iw), q_pos.reshape(nb, qb)))
    return o.swapaxes(0, 1).reshape(B, T, H, Dh)


def chunk_band_attention(q, k_new, v_new, k_pre, v_pre, pos0, rel_bias):
    B, T, H, Dh = q.shape
    P = k_pre.shape[1]
    c = min(CHUNK, T)
    n = T // c
    band = P + c
    kf = jnp.concatenate([k_pre, k_new], 1)
    vf = jnp.concatenate([v_pre, v_new], 1)

    def one_chunk(i):
        qc = lax.dynamic_slice_in_dim(q, i * c, c, axis=1)
        kb = lax.dynamic_slice_in_dim(kf, i * c, band, axis=1)
        vb = lax.dynamic_slice_in_dim(vf, i * c, band, axis=1)
        qp = pos0 + i * c + jnp.arange(c)
        kp = pos0 - P + i * c + jnp.arange(band)
        rel = jnp.clip(qp[:, None] - kp[None, :], -REL_CLIP, REL_CLIP) + REL_CLIP
        bias = rel_bias[:, rel].astype(jnp.float32)
        qch, kch = qp // CHUNK, kp // CHUNK
        ok = ((kp >= 0)[None, :] & (kch[None, :] <= qch[:, None])
              & (kch[None, :] >= qch[:, None] - LEFT_CHUNKS))
        s = jnp.einsum('bqhd,bkhd->bhqk', qc, kb).astype(jnp.float32) * Dh ** -0.5 + bias[None]
        s = jnp.where(ok[None, None], s, -jnp.inf)
        pr = jax.nn.softmax(s, axis=-1).astype(vb.dtype)
        return jnp.einsum('bhqk,bkhd->bqhd', pr, vb)

    o = lax.map(one_chunk, jnp.arange(n))
    return o.swapaxes(0, 1).reshape(B, T, H, Dh)


def even_mixer(x, pos0, ret_state, k_cache, v_cache, kidx_cache, w_in, ret_gn_g, kidx_ln_g, kidx_ln_b, w_out):
    B, T, _ = x.shape
    pos = pos0 + jnp.arange(T)
    offs = np.cumsum(EVEN_SPLITS)[:-1].tolist()
    rq, rk, rv, rg, dq, dk, dv, iq, ik, iw = jnp.split(x @ w_in, offs, axis=-1)

    def heads(a, h):
        return a.reshape(B, T, h, -1)

    rq = rotary(heads(rq, H_RET), pos, HEAD_DIM, RET_THETA)
    rk = rotary(heads(rk, H_RET), pos, HEAD_DIM, RET_THETA)
    ro, new_state = retention(rq, rk, heads(rv, H_RET), ret_state)
    mu = jnp.mean(ro, -1, keepdims=True)
    var = jnp.mean(jnp.square(ro - mu), -1, keepdims=True)
    ro = ((ro - mu) * lax.rsqrt(var + LN_EPS)).reshape(B, T, RET_W) * ret_gn_g
    ro = ro.astype(x.dtype) * jax.nn.silu(rg)
    dq = rotary(heads(dq, H_DSA), pos, ROPE_DIM, ROPE_THETA)
    dk = rotary(heads(dk, KVH_DSA), pos, ROPE_DIM, ROPE_THETA)
    dv = heads(dv, KVH_DSA)
    iq = rotary(heads(iq, H_IDX), pos, IDX_ROPE_DIM, ROPE_THETA)
    ik = rotary(layer_norm(ik, kidx_ln_g, kidx_ln_b)[:, :, None, :], pos, IDX_ROPE_DIM, ROPE_THETA)[:, :, 0]
    iw = iw.astype(jnp.float32) * (H_IDX ** -0.5 * D_IDX ** -0.5)
    k_all = jnp.concatenate([k_cache, dk], 1)
    v_all = jnp.concatenate([v_cache, dv], 1)
    ik_all = jnp.concatenate([kidx_cache, ik], 1)
    L = k_all.shape[1]
    n_sel = min(TOPK_MAX, L // 4)
    do = dsa_attention(dq, k_all, v_all, iq, ik_all, iw, pos, jnp.arange(L), n_sel)
    out = jnp.concatenate([ro, do.reshape(B, T, DSA_W)], -1) @ w_out
    return out, new_state.astype(ret_state.dtype), dk, dv, ik


def odd_mixer(x, pos0, k_pre, v_pre, w_in, rel_bias, w_out):
    B, T, _ = x.shape
    q, k, v = jnp.split(x @ w_in, 3, axis=-1)
    q = q.reshape(B, T, H_CHK, HEAD_DIM)
    k = k.reshape(B, T, H_CHK, HEAD_DIM)
    v = v.reshape(B, T, H_CHK, HEAD_DIM)
    o = chunk_band_attention(q, k, v, k_pre, v_pre, pos0, rel_bias)
    return o.reshape(B, T, CHK_W) @ w_out, k, v


def swiglu(x, wg, wu, wd):
    return (jax.nn.silu(x @ wg) * (x @ wu)) @ wd


def moe_swiglu(x, w_router, wg, wu, wd):
    B, T, D = x.shape
    xt = x.reshape(-1, D)
    N = xt.shape[0]
    logits = (xt @ w_router).astype(jnp.float32)
    top_l, top_e = lax.top_k(logits, TOP_K)
    gates = jax.nn.softmax(top_l, axis=-1)
    A = N * TOP_K
    blk = max(1, min(MOE_BLOCK, A // N_EXPERTS))
    nb = -(-A // blk) + N_EXPERTS
    flat_e = top_e.reshape(-1)
    order = jnp.argsort(flat_e)
    e_sorted = flat_e[order]
    counts = jnp.zeros((N_EXPERTS,), jnp.int32).at[flat_e].add(1)
    padded = (counts + blk - 1) // blk * blk
    pad_end = jnp.cumsum(padded)
    pad_start = pad_end - padded
    start = jnp.cumsum(counts) - counts
    slot = pad_start[e_sorted] + jnp.arange(A, dtype=jnp.int32) - start[e_sorted]
    slot_tok = jnp.full((nb * blk,), N, jnp.int32).at[slot].set((order // TOP_K).astype(jnp.int32))
    slot_gate = jnp.zeros((nb * blk,), jnp.float32).at[slot].set(gates.reshape(-1)[order])
    blk_e = jnp.minimum(jnp.searchsorted(pad_end, jnp.arange(nb, dtype=jnp.int32) * blk, side='right'),
                        N_EXPERTS - 1)
    xp = jnp.concatenate([xt, jnp.zeros((1, D), xt.dtype)], 0)

    def run(args):
        tok, e = args
        xb = xp[tok]
        return (jax.nn.silu(xb @ wg[e]) * (xb @ wu[e])) @ wd[e]

    yb = lax.map(run, (slot_tok.reshape(nb, blk), blk_e))
    y = jnp.zeros((N + 1, D), jnp.float32).at[slot_tok].add(
        yb.reshape(-1, D).astype(jnp.float32) * slot_gate[:, None])
    return y[:N].astype(x.dtype).reshape(B, T, D)


def trunk(x, p, pos0, ret_state, dsa_k, dsa_v, dsa_kidx, chk_k, chk_v, weights):
    (ln_mix_g, ln_mix_b, ln_ffn_g, ln_ffn_b, ple_proj, ple_gate,
     w_in_even, ret_gn_g, kidx_ln_g, kidx_ln_b, w_out_even, ffn_w_gate, ffn_w_up, ffn_w_down,
     w_in_odd, rel_bias, w_out_odd, router_w, exp_w_gate, exp_w_up, exp_w_down) = weights
    new_ret, new_k, new_v, new_ki, new_ck, new_cv = [], [], [], [], [], []
    for i in range(DEPTH):
        j = i // 2
        if i % 2 == 0:
            mix, s, k, v, ki = even_mixer(x, pos0, ret_state[j], dsa_k[j], dsa_v[j], dsa_kidx[j],
                                          w_in_even[j], ret_gn_g[j], kidx_ln_g[j], kidx_ln_b[j], w_out_even[j])
            new_ret.append(s)
            new_k.append(k)
            new_v.append(v)
            new_ki.append(ki)
            x = layer_norm(ALPHA * x + mix, ln_mix_g[i], ln_mix_b[i])
            ffn = swiglu(x, ffn_w_gate[j], ffn_w_up[j], ffn_w_down[j])
        else:
            mix, k, v = odd_mixer(x, pos0, chk_k[j], chk_v[j], w_in_odd[j], rel_bias[j], w_out_odd[j])
            new_ck.append(k)
            new_cv.append(v)
            x = layer_norm(ALPHA * x + mix, ln_mix_g[i], ln_mix_b[i])
            ffn = moe_swiglu(x, router_w[j], exp_w_gate[j], exp_w_up[j], exp_w_down[j])
        x = layer_norm(ALPHA * x + ffn, ln_ffn_g[i], ln_ffn_b[i])
        x = x + jax.nn.sigmoid(x @ ple_gate[i]) * (p[i] @ ple_proj[i])
    return x, (jnp.stack(new_ret), jnp.stack(new_k), jnp.stack(new_v), jnp.stack(new_ki),
               jnp.stack(new_ck), jnp.stack(new_cv))


def setup_inputs(seed: int = 0) -> dict:
    key = jax.random.key(seed)
    keys = iter(jax.random.split(key, 48))

    def nrm(shape, scale):
        return jax.random.normal(next(keys), shape, jnp.float32) * scale

    cbuf = min(LEFT_CHUNKS * CHUNK, PAST_LEN)
    d = D_MODEL
    return {
        'x_prompt': nrm((BATCH, SEQ, d), 1.0),
        'x_sample': nrm((DEC_BATCH, DEC_SEQ, d), 1.0),
        'p_prompt': nrm((DEPTH, BATCH, SEQ, D_PLE), 1.0),
        'p_sample': nrm((DEPTH, DEC_BATCH, DEC_SEQ, D_PLE), 1.0),
        'state_ret': nrm((N_EVEN, DEC_BATCH, H_RET, HEAD_DIM, HEAD_DIM), 0.1),
        'cache_dsa_k': nrm((N_EVEN, DEC_BATCH, PAST_LEN, KVH_DSA, HEAD_DIM), 1.0),
        'cache_dsa_v': nrm((N_EVEN, DEC_BATCH, PAST_LEN, KVH_DSA, HEAD_DIM), 1.0),
        'cache_dsa_kidx': nrm((N_EVEN, DEC_BATCH, PAST_LEN, D_IDX), 1.0),
        'cache_chk_k': nrm((N_ODD, DEC_BATCH, cbuf, H_CHK, HEAD_DIM), 1.0),
        'cache_chk_v': nrm((N_ODD, DEC_BATCH, cbuf, H_CHK, HEAD_DIM), 1.0),
        'ln_mix_g': 1.0 + nrm((DEPTH, d), 0.1),
        'ln_mix_b': nrm((DEPTH, d), 0.02),
        'ln_ffn_g': 1.0 + nrm((DEPTH, d), 0.1),
        'ln_ffn_b': nrm((DEPTH, d), 0.02),
        'ple_proj': nrm((DEPTH, D_PLE, d), D_PLE ** -0.5),
        'ple_gate': nrm((DEPTH, d, d), d ** -0.5),
        'w_in_even': nrm((N_EVEN, d, IN_EVEN), d ** -0.5),
        'ret_gn_g': 1.0 + nrm((N_EVEN, RET_W), 0.1),
        'kidx_ln_g': 1.0 + nrm((N_EVEN, D_IDX), 0.1),
        'kidx_ln_b': nrm((N_EVEN, D_IDX), 0.02),
        'w_out_even': nrm((N_EVEN, RET_W + DSA_W, d), BETA * (RET_W + DSA_W) ** -0.5),
        'ffn_w_gate': nrm((N_EVEN, d, D_FF), d ** -0.5),
        'ffn_w_up': nrm((N_EVEN, d, D_FF), d ** -0.5),
        'ffn_w_down': nrm((N_EVEN, D_FF, d), BETA * D_FF ** -0.5),
        'w_in_odd': nrm((N_ODD, d, 3 * CHK_W), d ** -0.5),
        'rel_bias': nrm((N_ODD, H_CHK, 2 * REL_CLIP + 1), 0.1),
        'w_out_odd': nrm((N_ODD, CHK_W, d), BETA * CHK_W ** -0.5),
        'router_w': nrm((N_ODD, d, N_EXPERTS), d ** -0.5),
        'exp_w_gate': nrm((N_ODD, N_EXPERTS, d, D_FF), d ** -0.5),
        'exp_w_up': nrm((N_ODD, N_EXPERTS, d, D_FF), d ** -0.5),
        'exp_w_down': nrm((N_ODD, N_EXPERTS, D_FF, d), BETA * D_FF ** -0.5),
    }


def reference(x_prompt, x_sample, p_prompt, p_sample, state_ret, cache_dsa_k, cache_dsa_v, cache_dsa_kidx,
              cache_chk_k, cache_chk_v, ln_mix_g, ln_mix_b, ln_ffn_g, ln_ffn_b, ple_proj, ple_gate,
              w_in_even, ret_gn_g, kidx_ln_g, kidx_ln_b, w_out_even, ffn_w_gate, ffn_w_up, ffn_w_down,
              w_in_odd, rel_bias, w_out_odd, router_w, exp_w_gate, exp_w_up, exp_w_down):
    weights = (ln_mix_g, ln_mix_b, ln_ffn_g, ln_ffn_b, ple_proj, ple_gate,
               w_in_even, ret_gn_g, kidx_ln_g, kidx_ln_b, w_out_even, ffn_w_gate, ffn_w_up, ffn_w_down,
               w_in_odd, rel_bias, w_out_odd, router_w, exp_w_gate, exp_w_up, exp_w_down)
    bp, seq = x_prompt.shape[0], x_prompt.shape[1]
    dt = x_prompt.dtype
    zero_state = jnp.zeros((N_EVEN, bp, H_RET, HEAD_DIM, HEAD_DIM), state_ret.dtype)
    no_rows = jnp.zeros((N_EVEN, bp, 0, KVH_DSA, HEAD_DIM), dt)
    no_idx_rows = jnp.zeros((N_EVEN, bp, 0, D_IDX), dt)
    band_pad = jnp.zeros((N_ODD, bp, LEFT_CHUNKS * CHUNK, H_CHK, HEAD_DIM), dt)
    y_prompt, st_p = trunk(x_prompt, p_prompt, 0, zero_state, no_rows, no_rows, no_idx_rows,
                           band_pad, band_pad, weights)
    past = cache_dsa_k.shape[2]
    y_sample, st_s = trunk(x_sample, p_sample, past, state_ret, cache_dsa_k, cache_dsa_v, cache_dsa_kidx,
                           cache_chk_k, cache_chk_v, weights)
    ret_state_prompt, dsa_k_prompt, dsa_v_prompt, dsa_kidx_prompt, chk_k_all, chk_v_all = st_p
    ret_state_sample, dsa_k_sample, dsa_v_sample, dsa_kidx_sample, chk_k_sample, chk_v_sample = st_s
    keep = min(LEFT_CHUNKS * CHUNK, seq)
    chk_k_prompt = chk_k_all[:, :, seq - keep:]
    chk_v_prompt = chk_v_all[:, :, seq - keep:]
    return (y_prompt, y_sample, ret_state_prompt, ret_state_sample, dsa_k_prompt, dsa_v_prompt, dsa_kidx_prompt,
            dsa_k_sample, dsa_v_sample, dsa_kidx_sample, chk_k_prompt, chk_v_prompt, chk_k_sample, chk_v_sample)
```

```python
import functools

import numpy as np
import jax
import jax.numpy as jnp
from jax import lax
from jax.experimental import pallas as pl
from jax.experimental.pallas import tpu as pltpu

F32 = jnp.float32
BF16 = jnp.bfloat16
I32 = jnp.int32

CHUNK = 64
HEAD_DIM = 128
H_RET = 16
H_DSA = 16
KVH_DSA = 4
DSA_GROUP = H_DSA // KVH_DSA
H_IDX = 32
D_IDX = 64
TOPK_MAX = 256
H_CHK = 32
LEFT_CHUNKS = 8
REL_CLIP = 128
N_EXPERTS = 8
TOP_K = 2
RET_THETA = 10000.0
ROPE_THETA = 500000.0
ROPE_DIM = HEAD_DIM // 4
IDX_ROPE_DIM = D_IDX // 4
LN_EPS = 1e-5
DEPTH = 2
ALPHA = (2.0 * DEPTH) ** 0.25

LANES = 128
VMEM_LIMIT = 56 * 1024 * 1024
NEG_BIG = -1e30
INT_MIN = -(2 ** 31)
INT_MAX = 2 ** 31 - 1


def _params(*sem):
    return pltpu.CompilerParams(dimension_semantics=sem, vmem_limit_bytes=VMEM_LIMIT)


def _cast_stripes(first, w_refs, wb_refs):
    @pl.when(first)
    def _():
        for w_ref, wb_ref in zip(w_refs, wb_refs):
            wb_ref[...] = w_ref[...].astype(BF16)


def _mm_body(x_ref, w_ref, o_ref, wb_ref):
    _cast_stripes(pl.program_id(1) == 0, [w_ref], [wb_ref])
    o_ref[...] = jnp.dot(x_ref[...], wb_ref[...], preferred_element_type=F32).astype(o_ref.dtype)


def _mm_acc_body(x_ref, w_ref, a_ref, o_ref, wb_ref):
    _cast_stripes(pl.program_id(1) == 0, [w_ref], [wb_ref])
    o_ref[...] = a_ref[...] + jnp.dot(x_ref[...], wb_ref[...], preferred_element_type=F32)


def _mm_swiglu_body(x_ref, wg_ref, wu_ref, o_ref, wgb_ref, wub_ref):
    _cast_stripes(pl.program_id(1) == 0, [wg_ref, wu_ref], [wgb_ref, wub_ref])
    x = x_ref[...]
    g = jnp.dot(x, wgb_ref[...], preferred_element_type=F32)
    u = jnp.dot(x, wub_ref[...], preferred_element_type=F32)
    o_ref[...] = (jax.nn.silu(g) * u).astype(o_ref.dtype)


def _mm_ple_body(x_ref, w_ref, p_ref, pw_ref, r_ref, of_ref, ob_ref, wb_ref):
    _cast_stripes(pl.program_id(1) == 0, [w_ref], [wb_ref])
    gate = jnp.dot(x_ref[...], wb_ref[...], preferred_element_type=F32)
    proj = jnp.dot(p_ref[...], pw_ref[...].astype(BF16), preferred_element_type=F32)
    y = r_ref[...] + jax.nn.sigmoid(gate) * proj
    of_ref[...] = y
    ob_ref[...] = y.astype(BF16)


def _w_spec(w, lead, tk, tn, kc):
    if w.ndim == 2:
        return pl.BlockSpec((tk, tn), lambda j, i: (kc, j))
    return pl.BlockSpec((None, tk, tn), lambda j, i: (lead, kc, j))


def _matmul(x, w, *, tm, tn, out_dtype=F32, lead=0, kc=0, tk=None, acc=None):
    M = x.shape[0]
    N = w.shape[-1]
    tk = x.shape[1] if tk is None else tk
    in_specs = [pl.BlockSpec((tm, tk), lambda j, i: (i, kc)), _w_spec(w, lead, tk, tn, kc)]
    args = [x, w]
    body = _mm_body
    if acc is not None:
        in_specs.append(pl.BlockSpec((tm, tn), lambda j, i: (i, j)))
        args.append(acc)
        body = _mm_acc_body
    return pl.pallas_call(
        body,
        grid=(pl.cdiv(N, tn), M // tm),
        in_specs=in_specs,
        out_specs=pl.BlockSpec((tm, tn), lambda j, i: (i, j)),
        out_shape=jax.ShapeDtypeStruct((M, N), out_dtype),
        scratch_shapes=[pltpu.VMEM((tk, tn), BF16)],
        compiler_params=_params("arbitrary", "arbitrary"),
        name="mm_acc" if acc is not None else "mm",
    )(*args)


def _matmul_kchunks(x, w, *, tm, tn, n_chunks, lead=0):
    tk = x.shape[1] // n_chunks
    y = None
    for kc in range(n_chunks):
        y = _matmul(x, w, tm=tm, tn=tn, lead=lead, kc=kc, tk=tk, acc=y)
    return y


def _matmul_swiglu(x, wg, wu, *, tm, tn, lead=0):
    M, K = x.shape
    N = wg.shape[-1]
    return pl.pallas_call(
        _mm_swiglu_body,
        grid=(N // tn, M // tm),
        in_specs=[pl.BlockSpec((tm, K), lambda j, i: (i, 0)),
                  _w_spec(wg, lead, K, tn, 0), _w_spec(wu, lead, K, tn, 0)],
        out_specs=pl.BlockSpec((tm, tn), lambda j, i: (i, j)),
        out_shape=jax.ShapeDtypeStruct((M, N), BF16),
        scratch_shapes=[pltpu.VMEM((K, tn), BF16), pltpu.VMEM((K, tn), BF16)],
        compiler_params=_params("arbitrary", "arbitrary"),
        name="mm_swiglu",
    )(x, wg, wu)


def _matmul_ple(xb, w, pb, pw, resid, *, tm, tn, lead):
    M, K = xb.shape
    N = w.shape[-1]
    KP = pb.shape[1]
    tile = pl.BlockSpec((tm, tn), lambda j, i: (i, j))
    return pl.pallas_call(
        _mm_ple_body,
        grid=(N // tn, M // tm),
        in_specs=[pl.BlockSpec((tm, K), lambda j, i: (i, 0)), _w_spec(w, lead, K, tn, 0),
                  pl.BlockSpec((tm, KP), lambda j, i: (i, 0)), _w_spec(pw, lead, KP, tn, 0), tile],
        out_specs=[tile, tile],
        out_shape=[jax.ShapeDtypeStruct((M, N), F32), jax.ShapeDtypeStruct((M, N), BF16)],
        scratch_shapes=[pltpu.VMEM((K, tn), BF16)],
        compiler_params=_params("arbitrary", "arbitrary"),
        name="mm_ple",
    )(xb, w, pb, pw, resid)


def _new_stripe(be_ref):
    i = pl.program_id(1)
    return jnp.logical_or(i == 0, be_ref[i] != be_ref[jnp.maximum(i - 1, 0)])


def _gmm_swiglu_body(be_ref, nu_ref, x_ref, wg_ref, wu_ref, o_ref, wgb_ref, wub_ref):
    used = pl.program_id(1) < nu_ref[0]
    _cast_stripes(jnp.logical_and(used, _new_stripe(be_ref)), [wg_ref, wu_ref], [wgb_ref, wub_ref])

    @pl.when(used)
    def _():
        x = x_ref[...]
        g = jnp.dot(x, wgb_ref[...], preferred_element_type=F32)
        u = jnp.dot(x, wub_ref[...], preferred_element_type=F32)
        o_ref[...] = (jax.nn.silu(g) * u).astype(o_ref.dtype)

    @pl.when(jnp.logical_not(used))
    def _():
        o_ref[...] = jnp.zeros_like(o_ref)


def _gmm_acc_body(be_ref, nu_ref, x_ref, w_ref, *rest, has_acc):
    a_ref = rest[0] if has_acc else None
    o_ref, wb_ref = rest[-2], rest[-1]
    used = pl.program_id(1) < nu_ref[0]
    _cast_stripes(jnp.logical_and(used, _new_stripe(be_ref)), [w_ref], [wb_ref])

    @pl.when(used)
    def _():
        y = jnp.dot(x_ref[...], wb_ref[...], preferred_element_type=F32)
        o_ref[...] = y + a_ref[...] if has_acc else y

    @pl.when(jnp.logical_not(used))
    def _():
        o_ref[...] = jnp.zeros_like(o_ref)


def _grouped_swiglu(blk_e, n_used, xs, wg, wu, *, tm, tn):
    A, K = xs.shape
    N = wg.shape[-1]
    w_spec = pl.BlockSpec((None, K, tn), lambda j, i, be, nu: (be[i], 0, j))
    return pl.pallas_call(
        _gmm_swiglu_body,
        grid_spec=pltpu.PrefetchScalarGridSpec(
            num_scalar_prefetch=2,
            grid=(N // tn, A // tm),
            in_specs=[pl.BlockSpec((tm, K), lambda j, i, be, nu: (i, 0)), w_spec, w_spec],
            out_specs=pl.BlockSpec((tm, tn), lambda j, i, be, nu: (i, j)),
            scratch_shapes=[pltpu.VMEM((K, tn), BF16), pltpu.VMEM((K, tn), BF16)]),
        out_shape=jax.ShapeDtypeStruct((A, N), BF16),
        compiler_params=_params("arbitrary", "arbitrary"),
        name="gmm_swiglu",
    )(blk_e, n_used, xs, wg, wu)


def _grouped_down(blk_e, n_used, acts, wd, *, tm, tn, n_chunks):
    A, K = acts.shape
    N = wd.shape[-1]
    tk = K // n_chunks
    tile = pl.BlockSpec((tm, tn), lambda j, i, be, nu: (i, j))
    y = None
    for kc in range(n_chunks):
        in_specs = [pl.BlockSpec((tm, tk), lambda j, i, be, nu, kc=kc: (i, kc)),
                    pl.BlockSpec((None, tk, tn), lambda j, i, be, nu, kc=kc: (be[i], kc, j))]
        args = [blk_e, n_used, acts, wd]
        if y is not None:
            in_specs.append(tile)
            args.append(y)
        y = pl.pallas_call(
            functools.partial(_gmm_acc_body, has_acc=y is not None),
            grid_spec=pltpu.PrefetchScalarGridSpec(
                num_scalar_prefetch=2,
                grid=(N // tn, A // tm),
                in_specs=in_specs,
                out_specs=tile,
                scratch_shapes=[pltpu.VMEM((tk, tn), BF16)]),
            out_shape=jax.ShapeDtypeStruct((A, N), F32),
            compiler_params=_params("arbitrary", "arbitrary"),
            name="gmm_down",
        )(*args)
    return y


def _layer_norm_rows(z, g, b):
    mu = jnp.mean(z, -1, keepdims=True)
    d = z - mu
    var = jnp.mean(d * d, -1, keepdims=True)
    return d * lax.rsqrt(var + LN_EPS) * g + b


def _ln_body(a_ref, b_ref, g_ref, beta_ref, of_ref, ob_ref):
    y = _layer_norm_rows(ALPHA * a_ref[...] + b_ref[...], g_ref[...], beta_ref[...])
    of_ref[...] = y
    ob_ref[...] = y.astype(BF16)


def _residual_layer_norm(a, b, g, beta, *, tr):
    M, D = a.shape
    row = pl.BlockSpec((tr, D), lambda i: (i, 0))
    vec = pl.BlockSpec((1, D), lambda i: (0, 0))
    return pl.pallas_call(
        _ln_body,
        grid=(M // tr,),
        in_specs=[row, row, vec, vec],
        out_specs=[row, row],
        out_shape=[jax.ShapeDtypeStruct((M, D), F32), jax.ShapeDtypeStruct((M, D), BF16)],
        compiler_params=_params("parallel"),
        name="res_ln",
    )(a, b, g.reshape(1, D), beta.reshape(1, D))


def _rotate(x, c, s_up, s_down, half):
    w = x.shape[1]
    reps = w // LANES
    if reps > 1:
        c, s_up, s_down = (jnp.concatenate([t] * reps, axis=1) for t in (c, s_up, s_down))
    return x * c + pltpu.roll(x, w - half, 1) * s_up + pltpu.roll(x, half, 1) * s_down


def _rot_body(x_ref, c_ref, su_ref, sd_ref, *o_refs, half):
    y = _rotate(x_ref[...], c_ref[...], su_ref[...], sd_ref[...], half)
    for o_ref in o_refs:
        o_ref[...] = y.astype(o_ref.dtype)


def _rotary_cols(h, col_block, width, tables, half, out_dtypes, *, tr):
    M = h.shape[0]
    tab = pl.BlockSpec((tr, LANES), lambda i: (i, 0))
    return pl.pallas_call(
        functools.partial(_rot_body, half=half),
        grid=(M // tr,),
        in_specs=[pl.BlockSpec((tr, width), lambda i: (i, col_block)), tab, tab, tab],
        out_specs=[pl.BlockSpec((tr, width), lambda i: (i, 0)) for _ in out_dtypes],
        out_shape=[jax.ShapeDtypeStruct((M, width), dt) for dt in out_dtypes],
        compiler_params=_params("parallel"),
        name="rotary",
    )(h, *tables)


def _ik_body(x_ref, g_ref, b_ref, c_ref, su_ref, sd_ref, ik_ref, ik2_ref, iw_ref):
    x = x_ref[...]
    is_key = lax.broadcasted_iota(I32, x.shape, 1) < D_IDX
    mu = jnp.sum(jnp.where(is_key, x, 0.0), -1, keepdims=True) / D_IDX
    d = jnp.where(is_key, x - mu, 0.0)
    var = jnp.sum(d * d, -1, keepdims=True) / D_IDX
    y = jnp.where(is_key, d * lax.rsqrt(var + LN_EPS) * g_ref[...] + b_ref[...], 0.0)
    y = _rotate(y, c_ref[...], su_ref[...], sd_ref[...], IDX_ROPE_DIM // 2)
    y = jnp.where(is_key, y, 0.0)
    ik_ref[...] = y[:, :D_IDX]
    ik2_ref[...] = (y + pltpu.roll(y, D_IDX, 1)).astype(BF16)
    iw_ref[...] = pltpu.roll(x, LANES - D_IDX, 1)[:, :H_IDX]


def _indexer_keys(h, col_block, g, b, tables, *, tr):
    M = h.shape[0]
    pad = LANES - D_IDX
    tab = pl.BlockSpec((tr, LANES), lambda i: (i, 0))
    vec = pl.BlockSpec((1, LANES), lambda i: (0, 0))
    return pl.pallas_call(
        _ik_body,
        grid=(M // tr,),
        in_specs=[pl.BlockSpec((tr, LANES), lambda i: (i, col_block)), vec, vec, tab, tab, tab],
        out_specs=[pl.BlockSpec((tr, D_IDX), lambda i: (i, 0)), pl.BlockSpec((tr, LANES), lambda i: (i, 0)),
                   pl.BlockSpec((tr, H_IDX), lambda i: (i, 0))],
        out_shape=[jax.ShapeDtypeStruct((M, D_IDX), F32), jax.ShapeDtypeStruct((M, LANES), BF16),
                   jax.ShapeDtypeStruct((M, H_IDX), F32)],
        compiler_params=_params("parallel"),
        name="indexer_keys",
    )(h, jnp.pad(g, (0, pad)).reshape(1, LANES), jnp.pad(b, (0, pad)).reshape(1, LANES), *tables)


def _router_body(l_ref, e_ref, g_ref):
    lg = l_ref[...]
    idx = lax.broadcasted_iota(I32, lg.shape, 1)
    m1 = jnp.max(lg, -1, keepdims=True)
    e1 = jnp.min(jnp.where(lg == m1, idx, N_EXPERTS), -1, keepdims=True)
    rest = jnp.where(idx == e1, -jnp.inf, lg)
    m2 = jnp.max(rest, -1, keepdims=True)
    e2 = jnp.min(jnp.where(rest == m2, idx, N_EXPERTS), -1, keepdims=True)
    ex2 = jnp.exp(m2 - m1)
    den = 1.0 + ex2
    e_ref[...] = jnp.concatenate([e1, e2], axis=1)
    g_ref[...] = jnp.concatenate([1.0 / den, ex2 / den], axis=1)


def _router_top2(logits, *, tr):
    M, E = logits.shape
    return pl.pallas_call(
        _router_body,
        grid=(M // tr,),
        in_specs=[pl.BlockSpec((tr, E), lambda i: (i, 0))],
        out_specs=[pl.BlockSpec((tr, TOP_K), lambda i: (i, 0)), pl.BlockSpec((tr, TOP_K), lambda i: (i, 0))],
        out_shape=[jax.ShapeDtypeStruct((M, TOP_K), I32), jax.ShapeDtypeStruct((M, TOP_K), F32)],
        compiler_params=_params("parallel"),
        name="router_top2",
    )(logits)


def _row_copy(src_hbm, row, dst, r, sem):
    return pltpu.make_async_copy(src_hbm.at[pl.ds(row, 1), :], dst.at[pl.ds(r, 1), :], sem)


def _gather_body(tok_ref, x_hbm, o_ref, buf, sem, *, tb):
    base = pl.program_id(0) * tb

    def start(r, _):
        _row_copy(x_hbm, tok_ref[base + r], buf, r, sem).start()
        return 0

    def wait(r, _):
        _row_copy(x_hbm, 0, buf, r, sem).wait()
        return 0

    lax.fori_loop(0, tb, start, 0)
    lax.fori_loop(0, tb, wait, 0)
    o_ref[...] = buf[...].astype(BF16)


def _gather_rows(slot_tok, x, *, tb):
    A = slot_tok.shape[0]
    D = x.shape[1]
    return pl.pallas_call(
        functools.partial(_gather_body, tb=tb),
        grid_spec=pltpu.PrefetchScalarGridSpec(
            num_scalar_prefetch=1,
            grid=(A // tb,),
            in_specs=[pl.BlockSpec(memory_space=pl.ANY)],
            out_specs=pl.BlockSpec((tb, D), lambda i, tok: (i, 0)),
            scratch_shapes=[pltpu.VMEM((tb, D), F32), pltpu.SemaphoreType.DMA(())]),
        out_shape=jax.ShapeDtypeStruct((A, D), BF16),
        compiler_params=_params("arbitrary"),
        name="moe_gather",
    )(slot_tok, x)


def _combine_body(slot_ref, y_hbm, gate_ref, x_ref, g_ref, beta_ref, of_ref, ob_ref, buf, sem, *, tb):
    base = pl.program_id(0) * tb

    def start(r, _):
        for k in range(TOP_K):
            _row_copy(y_hbm, slot_ref[(base + r) * TOP_K + k], buf.at[k], r, sem).start()
        return 0

    def wait(r, _):
        for k in range(TOP_K):
            _row_copy(y_hbm, 0, buf.at[k], r, sem).wait()
        return 0

    lax.fori_loop(0, tb, start, 0)
    lax.fori_loop(0, tb, wait, 0)
    gates = gate_ref[...]
    y = buf[0] * gates[:, 0:1] + buf[1] * gates[:, 1:2]
    z = _layer_norm_rows(ALPHA * x_ref[...] + y, g_ref[...], beta_ref[...])
    of_ref[...] = z
    ob_ref[...] = z.astype(BF16)


def _moe_combine_ln(tok_slot, yb, gates, x, g, beta, *, tb):
    M, D = x.shape
    row = pl.BlockSpec((tb, D), lambda i, s: (i, 0))
    vec = pl.BlockSpec((1, D), lambda i, s: (0, 0))
    return pl.pallas_call(
        functools.partial(_combine_body, tb=tb),
        grid_spec=pltpu.PrefetchScalarGridSpec(
            num_scalar_prefetch=1,
            grid=(M // tb,),
            in_specs=[pl.BlockSpec(memory_space=pl.ANY), pl.BlockSpec((tb, TOP_K), lambda i, s: (i, 0)),
                      row, vec, vec],
            out_specs=[row, row],
            scratch_shapes=[pltpu.VMEM((TOP_K, tb, D), F32), pltpu.SemaphoreType.DMA(())]),
        out_shape=[jax.ShapeDtypeStruct((M, D), F32), jax.ShapeDtypeStruct((M, D), BF16)],
        compiler_params=_params("arbitrary"),
        name="moe_combine",
    )(tok_slot.reshape(-1), yb, gates, x, g.reshape(1, D), beta.reshape(1, D))


def _ret_body(q_ref, k_ref, v_ref, gate_ref, c_ref, su_ref, sd_ref, lg_ref, gn_ref, s0_ref, _, o_ref, st_ref,
              s_scr, *, C):
    @pl.when(pl.program_id(1) == 0)
    def _():
        s_scr[...] = s0_ref[...]

    c, su, sd = c_ref[...], su_ref[...], sd_ref[...]
    q = _rotate(q_ref[...], c, su, sd, HEAD_DIM // 2)
    k = _rotate(k_ref[...], c, su, sd, HEAD_DIM // 2) * HEAD_DIM ** -0.5
    vb = v_ref[...].astype(BF16)
    lg_row = lg_ref[0:1, :C]
    lg_lane = lg_ref[0:1, :LANES]
    n_col = lax.broadcasted_iota(I32, (C, C), 0)
    m_row = lax.broadcasted_iota(I32, (C, C), 1)
    diff = (n_col - m_row).astype(F32)
    intra = jnp.where(diff >= 0, jnp.exp(lg_row * jnp.maximum(diff, 0.0)), 0.0)
    n_idx = lax.broadcasted_iota(I32, (C, LANES), 0).astype(F32)
    q_dec = jnp.exp(lg_lane * (n_idx + 1.0))
    k_dec = jnp.exp(lg_lane * (C - 1.0 - n_idx))
    c_dec = jnp.exp(lg_lane * C)

    qb = q.astype(BF16)
    s = lax.dot_general(qb, k.astype(BF16), (((1,), (1,)), ((), ())), preferred_element_type=F32) * intra
    state = s_scr[...]
    o = (jnp.dot(s.astype(BF16), vb, preferred_element_type=F32)
         + jnp.dot(qb, state.astype(BF16), preferred_element_type=F32) * q_dec)
    kd = (k * k_dec).astype(BF16)
    state = state * c_dec + lax.dot_general(kd, vb, (((0,), (0,)), ((), ())), preferred_element_type=F32)
    s_scr[...] = state
    st_ref[...] = state

    mu = jnp.mean(o, -1, keepdims=True)
    d = o - mu
    var = jnp.mean(d * d, -1, keepdims=True)
    on = d * lax.rsqrt(var + LN_EPS) * gn_ref[...]
    o_ref[...] = (on * jax.nn.silu(gate_ref[...])).astype(o_ref.dtype)


def _retention(h, tables, lg_tab, gn_g, state0, out_buf, *, C, n_seq, n_chunks, row0):
    rb0 = row0 // C

    def rows(sh, c):
        return rb0 + (sh // H_RET) * n_chunks + c

    def hcol(off):
        return pl.BlockSpec((C, HEAD_DIM), lambda sh, c, off=off: (rows(sh, c), off + sh % H_RET))

    tab = pl.BlockSpec((C, LANES), lambda sh, c: (rows(sh, c), 0))
    in_specs = [hcol(0), hcol(H_RET), hcol(2 * H_RET), hcol(3 * H_RET), tab, tab, tab,
                pl.BlockSpec((None, 8, lg_tab.shape[2]), lambda sh, c: (sh % H_RET, 0, 0)),
                pl.BlockSpec((1, HEAD_DIM), lambda sh, c: (0, sh % H_RET)),
                pl.BlockSpec((None, HEAD_DIM, HEAD_DIM), lambda sh, c: (sh, 0, 0)),
                pl.BlockSpec(memory_space=pl.ANY)]
    state_shape = (n_seq * H_RET, HEAD_DIM, HEAD_DIM)
    return pl.pallas_call(
        functools.partial(_ret_body, C=C),
        grid=(n_seq * H_RET, n_chunks),
        in_specs=in_specs,
        out_specs=[pl.BlockSpec((C, HEAD_DIM), lambda sh, c: (rows(sh, c), sh % H_RET)),
                   pl.BlockSpec((None, HEAD_DIM, HEAD_DIM), lambda sh, c: (sh, 0, 0))],
        out_shape=[jax.ShapeDtypeStruct(out_buf.shape, out_buf.dtype), jax.ShapeDtypeStruct(state_shape, F32)],
        scratch_shapes=[pltpu.VMEM((HEAD_DIM, HEAD_DIM), F32)],
        input_output_aliases={len(in_specs) - 1: 0},
        compiler_params=_params("arbitrary", "arbitrary"),
        name="retention",
    )(h, h, h, h, *tables, lg_tab, gn_g.reshape(1, -1), state0.reshape(state_shape), out_buf)


def _dsa_body(q_ref, iq_ref, iwt_ref, k_ref, v_ref, ik2_ref, *rest, TQ, TK, L, n_sel, q_pos0, q_stride, causal):
    o_ref, key_scr, iqm_scr, q4_scr, m_scr, l_scr, acc_scr = rest[-7:]
    G = DSA_GROUP
    i = pl.program_id(0)
    q_pos = q_pos0 + i * q_stride + lax.broadcasted_iota(I32, (1, TQ), 1)
    q_lim = ((q_pos >> 6) + 1) << 6
    if causal:
        n_tiles = jnp.minimum(((i + 1) * q_stride + TK - 1) // TK, L // TK)
    else:
        n_tiles = L // TK

    lane = lax.broadcasted_iota(I32, (TQ, LANES), 1)
    for h in range(H_IDX):
        pair = iq_ref[:, (h // 2) * LANES:(h // 2 + 1) * LANES]
        keep = (lane < D_IDX) if h % 2 == 0 else (lane >= D_IDX)
        iqm_scr[h] = jnp.where(keep, pair, jnp.zeros_like(pair))
    iwt = iwt_ref[...] * (H_IDX ** -0.5 * D_IDX ** -0.5)

    def key_pos(kt):
        return kt * TK + lax.broadcasted_iota(I32, (TK, TQ), 0)

    def score_tile(kt, _):
        ik2 = ik2_ref[pl.ds(kt * TK, TK), :]
        acc = jnp.zeros((TK, TQ), F32)
        for h in range(H_IDX):
            a = lax.dot_general(ik2, iqm_scr[h], (((1,), (1,)), ((), ())), preferred_element_type=F32)
            acc = acc + jnp.maximum(a, 0.0) * iwt[h:h + 1, :]
        acc = jnp.where(key_pos(kt) < q_lim, acc, -jnp.inf)
        bits = pltpu.bitcast(acc, I32)
        key_scr[pl.ds(kt * TK, TK), :] = bits ^ ((bits >> 31) & INT_MAX)
        return 0

    lax.fori_loop(0, n_tiles, score_tile, 0)

    def bisect(_, carry):
        lo, hi = carry
        mid = lo + lax.shift_right_logical(hi - lo, 1)

        def count_tile(kt, c8):
            ge = (key_scr[pl.ds(kt * TK, TK), :] >= mid).astype(I32)
            return c8 + jnp.sum(ge.reshape(TK // 8, 8, TQ), axis=0)

        c8 = lax.fori_loop(0, n_tiles, count_tile, jnp.zeros((8, TQ), I32))
        enough = jnp.sum(c8, axis=0, keepdims=True) >= n_sel
        return jnp.where(enough, mid, lo), jnp.where(enough, hi, mid)

    thr, _ = lax.fori_loop(0, 32, bisect, (jnp.full((1, TQ), INT_MIN, I32), jnp.full((1, TQ), INT_MAX, I32)))

    for kv in range(KVH_DSA):
        q4_scr[kv] = jnp.concatenate(
            [q_ref[:, (kv * G + g) * HEAD_DIM:(kv * G + g + 1) * HEAD_DIM] for g in range(G)], axis=0)
    m_scr[...] = jnp.full_like(m_scr, NEG_BIG)
    l_scr[...] = jnp.zeros_like(l_scr)
    acc_scr[...] = jnp.zeros_like(acc_scr)

    def attend_tile(kt, _):
        sel = jnp.logical_and(key_scr[pl.ds(kt * TK, TK), :] >= thr, key_pos(kt) < q_lim)
        sel4 = jnp.concatenate([sel] * G, axis=1)
        for kv in range(KVH_DSA):
            cols = slice(kv * HEAD_DIM, (kv + 1) * HEAD_DIM)
            kt_ = k_ref[pl.ds(kt * TK, TK), cols]
            vt_ = v_ref[pl.ds(kt * TK, TK), cols]
            lg = lax.dot_general(kt_, q4_scr[kv], (((1,), (1,)), ((), ())), preferred_element_type=F32)
            lg = jnp.where(sel4, lg * HEAD_DIM ** -0.5, NEG_BIG)
            m_old = m_scr[kv]
            m_new = jnp.maximum(m_old, jnp.max(lg, axis=0, keepdims=True))
            alpha = jnp.exp(m_old - m_new)
            p = jnp.where(sel4, jnp.exp(lg - m_new), 0.0)
            l_scr[kv] = alpha * l_scr[kv] + jnp.sum(p, axis=0, keepdims=True)
            pv = lax.dot_general(vt_, p.astype(BF16), (((0,), (0,)), ((), ())), preferred_element_type=F32)
            acc_scr[kv] = acc_scr[kv] * alpha + pv
            m_scr[kv] = m_new
        return 0

    lax.fori_loop(0, n_tiles, attend_tile, 0)

    for kv in range(KVH_DSA):
        o_t = acc_scr[kv] / l_scr[kv]
        for g in range(G):
            hd = kv * G + g
            o_ref[:, hd * HEAD_DIM:(hd + 1) * HEAD_DIM] = o_t[:, g * TQ:(g + 1) * TQ].T.astype(o_ref.dtype)


def _dsa(q, iq, iwt, k, v, ik2, out_buf, *, TQ, TK, n_blocks, n_sel, q_pos0, q_stride, causal,
         q_row0, out_row0, out_col0, per_block_keys):
    L = k.shape[-2]
    W = H_DSA * HEAD_DIM
    qb0, ob0, oc0 = q_row0 // TQ, out_row0 // TQ, out_col0 // W
    if per_block_keys:
        kspec = lambda a: pl.BlockSpec((None, L, a.shape[-1]), lambda i: (i, 0, 0))
    else:
        kspec = lambda a: pl.BlockSpec((L, a.shape[-1]), lambda i: (0, 0))
    G = DSA_GROUP
    return pl.pallas_call(
        functools.partial(_dsa_body, TQ=TQ, TK=TK, L=L, n_sel=n_sel, q_pos0=q_pos0, q_stride=q_stride,
                          causal=causal),
        grid=(n_blocks,),
        in_specs=[pl.BlockSpec((TQ, W), lambda i: (qb0 + i, 0)),
                  pl.BlockSpec((TQ, H_IDX * D_IDX), lambda i: (qb0 + i, 0)),
                  pl.BlockSpec((H_IDX, TQ), lambda i: (0, qb0 + i)),
                  kspec(k), kspec(v), kspec(ik2),
                  pl.BlockSpec(memory_space=pl.ANY)],
        out_specs=pl.BlockSpec((TQ, W), lambda i: (ob0 + i, oc0)),
        out_shape=jax.ShapeDtypeStruct(out_buf.shape, out_buf.dtype),
        scratch_shapes=[pltpu.VMEM((L, TQ), I32), pltpu.VMEM((H_IDX, TQ, LANES), BF16),
                        pltpu.VMEM((KVH_DSA, G * TQ, HEAD_DIM), BF16),
                        pltpu.VMEM((KVH_DSA, 1, G * TQ), F32), pltpu.VMEM((KVH_DSA, 1, G * TQ), F32),
                        pltpu.VMEM((KVH_DSA, HEAD_DIM, G * TQ), F32)],
        input_output_aliases={6: 0},
        compiler_params=_params("arbitrary"),
        name="dsa",
    )(q, iq, iwt, k, v, ik2, out_buf)


def _bias_body(rb_ref, idx_ref, o_ref):
    n = rb_ref.shape[1]
    onehot = (lax.broadcasted_iota(I32, (n, idx_ref.shape[1]), 0) == idx_ref[...]).astype(F32)
    o_ref[...] = jnp.dot(rb_ref[...], onehot, preferred_element_type=F32, precision=lax.Precision.HIGHEST)


def _expand_bias(rel_bias, n_q, n_k, *, tn):
    H, R = rel_bias.shape
    RP = 3 * LANES
    t = np.arange(n_q)[:, None]
    s = np.arange(n_k)[None, :]
    idx = (np.clip(LEFT_CHUNKS * CHUNK + t - s, -REL_CLIP, REL_CLIP) + REL_CLIP).reshape(1, -1).astype(np.int32)
    out = pl.pallas_call(
        _bias_body,
        grid=(idx.shape[1] // tn,),
        in_specs=[pl.BlockSpec((H, RP), lambda j: (0, 0)), pl.BlockSpec((1, tn), lambda j: (0, j))],
        out_specs=pl.BlockSpec((H, tn), lambda j: (0, j)),
        out_shape=jax.ShapeDtypeStruct((H, idx.shape[1]), F32),
        compiler_params=_params("parallel"),
        name="rel_bias_expand",
    )(jnp.pad(rel_bias, ((0, 0), (0, RP - R))), jnp.asarray(idx))
    return out.reshape(H, n_q, n_k)


def _band_heads(q, k_parts, v_parts, bias_ref, ok, o_ref, HB):
    for hh in range(HB):
        cols = slice(hh * HEAD_DIM, (hh + 1) * HEAD_DIM)
        qh = q[:, cols].astype(BF16)
        kw = jnp.concatenate([p[:, cols] for p in k_parts], axis=0).astype(BF16)
        vw = jnp.concatenate([p[:, cols] for p in v_parts], axis=0).astype(BF16)
        s = lax.dot_general(qh, kw, (((1,), (1,)), ((), ())), preferred_element_type=F32)
        s = s * HEAD_DIM ** -0.5 + bias_ref[hh]
        if ok is not None:
            s = jnp.where(ok, s, NEG_BIG)
        p = jnp.exp(s - jnp.max(s, -1, keepdims=True))
        pr = (p / jnp.sum(p, -1, keepdims=True)).astype(BF16)
        o_ref[:, cols] = jnp.dot(pr, vw, preferred_element_type=F32).astype(o_ref.dtype)


def _band_prompt_body(q_ref, *rest, HB, TQ, NP):
    k_refs, v_refs = rest[:NP], rest[NP:2 * NP]
    bias_ref, o_ref = rest[2 * NP], rest[2 * NP + 2]
    j = pl.program_id(1)
    nk = NP * TQ
    q_pos = j * TQ + lax.broadcasted_iota(I32, (TQ, nk), 0)
    k_pos = (j - (NP - 1)) * TQ + lax.broadcasted_iota(I32, (TQ, nk), 1)
    qc, kc = q_pos >> 6, k_pos >> 6
    ok = jnp.logical_and(jnp.logical_and(k_pos >= 0, kc <= qc), kc >= qc - LEFT_CHUNKS)
    _band_heads(q_ref[...], [r[...] for r in k_refs], [r[...] for r in v_refs], bias_ref, ok, o_ref, HB)


def _band_prompt(qkv, bias, out_buf, *, n_blocks, HB, TQ, NP):
    W = HB * HEAD_DIM
    n_hg = H_CHK // HB

    def kv_spec(r, third):
        return pl.BlockSpec((TQ, W), lambda hg, j: (jnp.maximum(j - (NP - 1) + r, 0), third * n_hg + hg))

    return pl.pallas_call(
        functools.partial(_band_prompt_body, HB=HB, TQ=TQ, NP=NP),
        grid=(n_hg, n_blocks),
        in_specs=([pl.BlockSpec((TQ, W), lambda hg, j: (j, hg))]
                  + [kv_spec(r, 1) for r in range(NP)] + [kv_spec(r, 2) for r in range(NP)]
                  + [pl.BlockSpec((HB, TQ, NP * TQ), lambda hg, j: (hg, 0, 0)), pl.BlockSpec(memory_space=pl.ANY)]),
        out_specs=pl.BlockSpec((TQ, W), lambda hg, j: (j, hg)),
        out_shape=jax.ShapeDtypeStruct(out_buf.shape, out_buf.dtype),
        input_output_aliases={2 * NP + 2: 0},
        compiler_params=_params("parallel", "parallel"),
        name="band_prompt",
    )(qkv, *([qkv] * (2 * NP)), bias, out_buf)


def _band_sample_body(q_ref, kc_ref, kn_ref, vc_ref, vn_ref, bias_ref, _, o_ref, *, HB):
    _band_heads(q_ref[...], [kc_ref[...], kn_ref[...]], [vc_ref[...], vn_ref[...]], bias_ref, None, o_ref, HB)


def _band_sample(qkv, cache_k, cache_v, bias, out_buf, *, n_seq, T, row0, HB):
    P = cache_k.shape[1]
    W = HB * HEAD_DIM
    n_hg = H_CHK // HB
    rb0 = row0 // T
    new = lambda third: pl.BlockSpec((T, W), lambda b, hg: (rb0 + b, third * n_hg + hg))
    cache = pl.BlockSpec((None, P, W), lambda b, hg: (b, 0, hg))
    return pl.pallas_call(
        functools.partial(_band_sample_body, HB=HB),
        grid=(n_seq, n_hg),
        in_specs=[new(0), cache, new(1), cache, new(2),
                  pl.BlockSpec((HB, T, P + T), lambda b, hg: (hg, 0, 0)), pl.BlockSpec(memory_space=pl.ANY)],
        out_specs=pl.BlockSpec((T, W), lambda b, hg: (rb0 + b, hg)),
        out_shape=jax.ShapeDtypeStruct(out_buf.shape, out_buf.dtype),
        input_output_aliases={6: 0},
        compiler_params=_params("parallel", "parallel"),
        name="band_sample",
    )(qkv, cache_k, qkv, cache_v, qkv, bias, out_buf)


def _rot_tables(pos, head_dim, rot_dim, theta):
    half = rot_dim // 2
    inv = theta ** (-jnp.arange(half, dtype=F32) / half)
    ang = pos.astype(F32)[:, None] * inv[None, :]
    cos, sin = jnp.cos(ang), jnp.sin(ang)
    m = pos.shape[0]
    zh = jnp.zeros((m, half), F32)
    rest0 = jnp.zeros((m, head_dim - rot_dim), F32)
    c = jnp.concatenate([cos, cos, jnp.ones((m, head_dim - rot_dim), F32)], 1)
    s_up = jnp.concatenate([-sin, zh, rest0], 1)
    s_down = jnp.concatenate([zh, sin, rest0], 1)
    reps = LANES // head_dim
    return tuple(jnp.tile(t, (1, reps)) for t in (c, s_up, s_down))


def _log_gamma_table(width):
    lg = jnp.log1p(-(2.0 ** (-5.0 - jnp.arange(H_RET, dtype=F32))))
    return jnp.broadcast_to(lg[:, None, None], (H_RET, 8, width))


def _routing(top_e, tm):
    n = top_e.shape[0]
    a = n * TOP_K
    nb = -(-a // tm) + N_EXPERTS
    flat_e = top_e.reshape(-1)
    onehot = (flat_e[:, None] == jnp.arange(N_EXPERTS, dtype=I32)[None, :]).astype(I32)
    rank = jnp.take_along_axis(jnp.cumsum(onehot, 0), flat_e[:, None], 1)[:, 0] - 1
    counts = jnp.sum(onehot, 0)
    padded = (counts + tm - 1) // tm * tm
    pad_end = jnp.cumsum(padded)
    slot = (pad_end - padded)[flat_e] + rank
    slot_tok = jnp.zeros((nb * tm,), I32).at[slot].set(jnp.arange(a, dtype=I32) // TOP_K)
    blk_e = jnp.minimum(jnp.searchsorted(pad_end, jnp.arange(nb, dtype=I32) * tm, side='right'),
                        N_EXPERTS - 1).astype(I32)
    n_used = (pad_end[-1:] // tm).astype(I32)
    return slot.reshape(n, TOP_K).astype(I32), slot_tok, blk_e, n_used


TM = 1040
TN = 512
TN_PAIR = 256
TR = 208
TM_MOE = 512
DOWN_CHUNKS = 4
RET_CHUNK = 256
DSA_TQ = 128
DSA_TK = 512
BAND_HB = 4
BAND_TQ = 128
BAND_PARTS = (LEFT_CHUNKS * CHUNK) // BAND_TQ + 1


def kernel(x_prompt, x_sample, p_prompt, p_sample, state_ret, cache_dsa_k, cache_dsa_v, cache_dsa_kidx,
           cache_chk_k, cache_chk_v, ln_mix_g, ln_mix_b, ln_ffn_g, ln_ffn_b, ple_proj, ple_gate,
           w_in_even, ret_gn_g, kidx_ln_g, kidx_ln_b, w_out_even, ffn_w_gate, ffn_w_up, ffn_w_down,
           w_in_odd, rel_bias, w_out_odd, router_w, exp_w_gate, exp_w_up, exp_w_down):
    seq, d = x_prompt.shape[1], x_prompt.shape[2]
    nb_s, t_s = x_sample.shape[0], x_sample.shape[1]
    n_s = nb_s * t_s
    m = seq + n_s
    past = cache_dsa_k.shape[2]
    ret_w = H_RET * HEAD_DIM
    dsa_w = H_DSA * HEAD_DIM
    kv_w = KVH_DSA * HEAD_DIM

    x = jnp.concatenate([x_prompt[0], x_sample.reshape(n_s, d)], 0)
    p = jnp.concatenate([p_prompt[:, 0], p_sample.reshape(DEPTH, n_s, -1)], 1).astype(BF16)
    pos = jnp.concatenate([jnp.arange(seq, dtype=I32), past + jnp.tile(jnp.arange(t_s, dtype=I32), nb_s)])
    tab_ret = _rot_tables(pos, HEAD_DIM, HEAD_DIM, RET_THETA)
    tab_dsa = _rot_tables(pos, HEAD_DIM, ROPE_DIM, ROPE_THETA)
    tab_idx = _rot_tables(pos, D_IDX, IDX_ROPE_DIM, ROPE_THETA)

    h = _matmul(x.astype(BF16), w_in_even, tm=TM, tn=TN)
    c_dq = 4 * ret_w
    c_dk = c_dq + dsa_w
    c_dv = c_dk + kv_w
    c_iq = c_dv + kv_w
    c_ik = c_iq + H_IDX * D_IDX

    mix_in = jnp.zeros((m, ret_w + dsa_w), BF16)
    state0_p = jnp.zeros((1, H_RET, HEAD_DIM, HEAD_DIM), F32)
    mix_in, st_p = _retention(h, tab_ret, _log_gamma_table(RET_CHUNK), ret_gn_g[0], state0_p, mix_in,
                              C=RET_CHUNK, n_seq=1, n_chunks=seq // RET_CHUNK, row0=0)
    mix_in, st_s = _retention(h, tab_ret, _log_gamma_table(LANES), ret_gn_g[0], state_ret[0], mix_in,
                              C=t_s, n_seq=nb_s, n_chunks=1, row0=seq)

    (dq,) = _rotary_cols(h, c_dq // dsa_w, dsa_w, tab_dsa, ROPE_DIM // 2, [BF16], tr=TR)
    dk, dk_b = _rotary_cols(h, c_dk // kv_w, kv_w, tab_dsa, ROPE_DIM // 2, [F32, BF16], tr=TR)
    iq_lo, = _rotary_cols(h, c_iq // 1024, 1024, tab_idx, IDX_ROPE_DIM // 2, [BF16], tr=TR)
    iq_hi, = _rotary_cols(h, c_iq // 1024 + 1, 1024, tab_idx, IDX_ROPE_DIM // 2, [BF16], tr=TR)
    iq = jnp.concatenate([iq_lo, iq_hi], 1)
    ik, ik2, iw = _indexer_keys(h, c_ik // LANES, kidx_ln_g[0], kidx_ln_b[0], tab_idx, tr=TR)
    dv = h[:, c_dv:c_dv + kv_w]
    dv_b = dv.astype(BF16)
    iwt = iw.T

    n_sel_p = min(TOPK_MAX, seq // 4)
    mix_in = _dsa(dq, iq, iwt, dk_b[:seq], dv_b[:seq], ik2[:seq], mix_in, TQ=DSA_TQ, TK=DSA_TK,
                  n_blocks=seq // DSA_TQ, n_sel=n_sel_p, q_pos0=0, q_stride=DSA_TQ, causal=True,
                  q_row0=0, out_row0=0, out_col0=ret_w, per_block_keys=False)

    def pad_queries(a):
        a = a[seq:].reshape(nb_s, t_s, -1)
        return jnp.pad(a, ((0, 0), (0, DSA_TQ - t_s), (0, 0))).reshape(nb_s * DSA_TQ, -1)

    def with_cache(cache, new):
        return jnp.concatenate([cache.reshape(nb_s, past, -1).astype(BF16), new[seq:].reshape(nb_s, t_s, -1)], 1)

    kidx_c = cache_dsa_kidx[0].astype(BF16)
    l_s = past + t_s
    do_s = _dsa(pad_queries(dq), pad_queries(iq), pad_queries(iw).T,
                with_cache(cache_dsa_k[0], dk_b), with_cache(cache_dsa_v[0], dv_b),
                with_cache(jnp.concatenate([kidx_c, kidx_c], -1), ik2),
                jnp.zeros((nb_s * DSA_TQ, dsa_w), BF16), TQ=DSA_TQ, TK=l_s, n_blocks=nb_s,
                n_sel=min(TOPK_MAX, l_s // 4), q_pos0=past, q_stride=0, causal=False,
                q_row0=0, out_row0=0, out_col0=0, per_block_keys=True)
    do_s = do_s.reshape(nb_s, DSA_TQ, dsa_w)[:, :t_s].reshape(n_s, dsa_w)
    mix_in = lax.dynamic_update_slice(mix_in, do_s, (seq, ret_w))

    mix = _matmul(mix_in, w_out_even, tm=TM, tn=TN)
    x1, x1b = _residual_layer_norm(x, mix, ln_mix_g[0], ln_mix_b[0], tr=TR)
    act = _matmul_swiglu(x1b, ffn_w_gate, ffn_w_up, tm=TM, tn=TN_PAIR)
    ffn = _matmul_kchunks(act, ffn_w_down, tm=TM, tn=TN, n_chunks=DOWN_CHUNKS)
    x2, x2b = _residual_layer_norm(x1, ffn, ln_ffn_g[0], ln_ffn_b[0], tr=TR)
    x3, x3b = _matmul_ple(x2b, ple_gate, p[0], ple_proj, x2, tm=TM, tn=TN, lead=0)

    qkv = _matmul(x3b, w_in_odd, tm=TM, tn=TN)
    chk_w = H_CHK * HEAD_DIM
    p_band = cache_chk_k.shape[2]
    bias_p = _expand_bias(rel_bias[0], BAND_TQ, BAND_PARTS * BAND_TQ, tn=2048)
    bias_s = _expand_bias(rel_bias[0], t_s, p_band + t_s, tn=(t_s * (p_band + t_s)) // 2)
    att = _band_prompt(qkv, bias_p, jnp.zeros((m, chk_w), BF16), n_blocks=seq // BAND_TQ, HB=BAND_HB, TQ=BAND_TQ,
                       NP=BAND_PARTS)
    att = _band_sample(qkv, cache_chk_k[0].reshape(nb_s, p_band, chk_w), cache_chk_v[0].reshape(nb_s, p_band, chk_w),
                       bias_s, att, n_seq=nb_s, T=t_s, row0=seq, HB=BAND_HB)
    mix = _matmul(att, w_out_odd, tm=TM, tn=TN)
    x4, x4b = _residual_layer_norm(x3, mix, ln_mix_g[1], ln_mix_b[1], tr=TR)

    logits = _matmul(x4b, router_w, tm=TM, tn=N_EXPERTS)
    top_e, gates = _router_top2(logits, tr=TR)
    tok_slot, slot_tok, blk_e, n_used = _routing(top_e, TM_MOE)
    xs = _gather_rows(slot_tok, x4, tb=TM_MOE // 2)
    acts = _grouped_swiglu(blk_e, n_used, xs, exp_w_gate[0], exp_w_up[0], tm=TM_MOE, tn=TN_PAIR)
    yb = _grouped_down(blk_e, n_used, acts, exp_w_down[0], tm=TM_MOE, tn=TN, n_chunks=DOWN_CHUNKS)
    x5, x5b = _moe_combine_ln(tok_slot, yb, gates, x4, ln_ffn_g[1], ln_ffn_b[1], tb=TR)
    y, _ = _matmul_ple(x5b, ple_gate, p[1], ple_proj, x5, tm=TM, tn=TN, lead=1)

    keep = min(LEFT_CHUNKS * CHUNK, seq)
    ck = qkv[:, chk_w:2 * chk_w]
    cv = qkv[:, 2 * chk_w:]

    def heads(a, nh):
        return a.reshape(a.shape[0], nh, HEAD_DIM)

    return (y[:seq][None], y[seq:].reshape(nb_s, t_s, d),
            st_p.reshape(1, 1, H_RET, HEAD_DIM, HEAD_DIM), st_s.reshape(1, nb_s, H_RET, HEAD_DIM, HEAD_DIM),
            heads(dk[:seq], KVH_DSA)[None, None], heads(dv[:seq], KVH_DSA)[None, None], ik[:seq][None, None],
            heads(dk[seq:], KVH_DSA).reshape(1, nb_s, t_s, KVH_DSA, HEAD_DIM),
            heads(dv[seq:], KVH_DSA).reshape(1, nb_s, t_s, KVH_DSA, HEAD_DIM),
            ik[seq:].reshape(1, nb_s, t_s, D_IDX),
            heads(ck[seq - keep:seq], H_CHK)[None, None], heads(cv[seq - keep:seq], H_CHK)[None, None],
            heads(ck[seq:], H_CHK).reshape(1, nb_s, t_s, H_CHK, HEAD_DIM),
            heads(cv[seq:], H_CHK).reshape(1, nb_s, t_s, H_CHK, HEAD_DIM))
```

```python
import functools
import math

import numpy as np
import jax
import jax.numpy as jnp
from jax import lax
from jax.experimental import pallas as pl
from jax.experimental.pallas import tpu as pltpu

F32 = jnp.float32
BF16 = jnp.bfloat16
I32 = jnp.int32

CHUNK = 64
HEAD_DIM = 128
H_RET = 16
H_DSA = 16
KVH_DSA = 4
DSA_GROUP = H_DSA // KVH_DSA
H_IDX = 32
D_IDX = 64
TOPK_MAX = 256
H_CHK = 32
LEFT_CHUNKS = 8
REL_CLIP = 128
N_EXPERTS = 8
TOP_K = 2
RET_THETA = 10000.0
ROPE_THETA = 500000.0
ROPE_DIM = HEAD_DIM // 4
IDX_ROPE_DIM = D_IDX // 4
LN_EPS = 1e-5
DEPTH = 2
ALPHA = (2.0 * DEPTH) ** 0.25

LANES = 128
VMEM_LIMIT = 56 * 1024 * 1024
NEG_BIG = -1e30
EXP2_SCALE = HEAD_DIM ** -0.5 * math.log2(math.e)
INT_MIN = -(2 ** 31)
INT_MAX = 2 ** 31 - 1


def _params(*sem):
    return pltpu.CompilerParams(dimension_semantics=sem, vmem_limit_bytes=VMEM_LIMIT)


class _Groups:
    def __init__(self, e, st, nb, rg, sb, dense):
        self.e, self.st, self.nb, self.rg, self.sb, self.dense = e, st, nb, rg, sb, dense
        self.n = e.shape[0]

    def rows(self, width, col, buffers=None, own=False):
        kw = {} if buffers is None else dict(pipeline_mode=pl.Buffered(buffers))
        if own:
            return pl.BlockSpec((self.rg, width), lambda s, *a: (s, col(a[-1][s] > 0, *a[:-3])), **kw)
        if self.dense:
            return pl.BlockSpec((self.rg, width), lambda s, *a: (a[-2][s], col(a[-1][s] > 0, *a[:-3])), **kw)
        unit = self.sb
        return pl.BlockSpec((pl.Element(self.rg), pl.Element(width)),
                            lambda s, *a: (a[-2][s] * unit, col(a[-1][s] > 0, *a[:-3]) * width), **kw)

    def weights(self, k_rows, width, kcol):
        return pl.BlockSpec((None, k_rows, width), lambda s, *a: (a[-3][s], *kcol(a[-1][s] > 0, *a[:-3])))


def _dense_groups(m, rg, lead=0):
    n = m // rg
    return _Groups(jnp.full((n,), lead, I32), jnp.arange(n, dtype=I32), jnp.ones((n,), I32), rg, rg, True)


def _live_rows(nb_ref, n_sub, sb, compute):
    nblk = nb_ref[pl.program_id(0)]
    if n_sub == 1:
        @pl.when(nblk > 0)
        def _():
            compute(pl.ds(0, sb))

        return nblk

    def run_pair(r, c):
        compute(pl.ds(pl.multiple_of(r * (2 * sb), 2 * sb), 2 * sb))
        return c

    lax.fori_loop(0, lax.shift_right_logical(nblk, 1), run_pair, 0)

    @pl.when(nblk & 1 == 1)
    def _():
        compute(pl.ds(pl.multiple_of((nblk - 1) * sb, sb), sb))

    return nblk


def _row_loops(nb_ref, n_sub, sb, compute, o_refs):
    nblk = _live_rows(nb_ref, n_sub, sb, compute)

    def clear(r, c):
        for o_ref in o_refs:
            o_ref[pl.ds(pl.multiple_of(r * sb, sb), sb), :] = jnp.zeros((sb, o_ref.shape[1]), o_ref.dtype)
        return c

    lax.fori_loop(nblk, n_sub, clear, 0)


def _rg_mm_body(e_ref, st_ref, nb_ref, x_ref, w_ref, o_ref, wb_ref, *, sb):
    wb_ref[...] = w_ref[...].astype(BF16)

    def compute(rows):
        o_ref[rows, :] = jnp.dot(x_ref[rows, :], wb_ref[...], preferred_element_type=F32).astype(o_ref.dtype)

    _row_loops(nb_ref, x_ref.shape[0] // sb, sb, compute, [o_ref])


def _rg_swiglu_body(e_ref, st_ref, nb_ref, x_ref, wg_ref, wu_ref, o_ref, wgb_ref, wub_ref, *, sb):
    wgb_ref[...] = wg_ref[...].astype(BF16)
    wub_ref[...] = wu_ref[...].astype(BF16)

    def compute(rows):
        x = x_ref[rows, :]
        g = jnp.dot(x, wgb_ref[...], preferred_element_type=F32)
        u = jnp.dot(x, wub_ref[...], preferred_element_type=F32)
        o_ref[rows, :] = (jax.nn.silu(g) * u).astype(o_ref.dtype)

    _row_loops(nb_ref, x_ref.shape[0] // sb, sb, compute, [o_ref])


def _rg_ple_body(e_ref, st_ref, nb_ref, x_ref, w_ref, p_ref, pw_ref, r_ref, of_ref, ob_ref, wb_ref, *, sb):
    wb_ref[...] = w_ref[...].astype(BF16)
    pwb = pw_ref[...].astype(BF16)

    def compute(rows):
        gate = jnp.dot(x_ref[rows, :], wb_ref[...], preferred_element_type=F32)
        proj = jnp.dot(p_ref[rows, :], pwb, preferred_element_type=F32)
        y = r_ref[rows, :] + jax.nn.sigmoid(gate) * proj
        of_ref[rows, :] = y
        ob_ref[rows, :] = y.astype(BF16)

    _row_loops(nb_ref, x_ref.shape[0] // sb, sb, compute, [of_ref, ob_ref])


def _rg_down_body(e_ref, st_ref, nb_ref, a_ref, w_ref, o_ref, wb_ref, *, sb):
    wb_ref[...] = w_ref[...].astype(BF16)

    @pl.when(pl.program_id(2) == 0)
    def _():
        o_ref[...] = jnp.zeros_like(o_ref)

    def compute(rows):
        o_ref[rows, :] += jnp.dot(a_ref[rows, :], wb_ref[...], preferred_element_type=F32)

    _live_rows(nb_ref, a_ref.shape[0] // sb, sb, compute)


def _last_if_dead(n_blocks):
    return lambda live, j: jnp.where(live, j, n_blocks - 1)


def _rg_matmul(g, x, w, *, tn, out_dtype=F32):
    K = x.shape[1]
    N = w.shape[-1]
    J = pl.cdiv(N, tn)
    col = _last_if_dead(J)
    return pl.pallas_call(
        functools.partial(_rg_mm_body, sb=g.sb),
        grid_spec=pltpu.PrefetchScalarGridSpec(
            num_scalar_prefetch=3, grid=(g.n, J),
            in_specs=[g.rows(K, lambda live, j: 0, buffers=1), g.weights(K, tn, lambda live, j: (0, col(live, j)))],
            out_specs=g.rows(tn, lambda live, j: j, own=True),
            scratch_shapes=[pltpu.VMEM((K, tn), BF16)]),
        out_shape=jax.ShapeDtypeStruct((g.n * g.rg, N), out_dtype),
        compiler_params=_params("arbitrary", "arbitrary"),
        name="rg_mm",
    )(g.e, g.st, g.nb, x, w)


def _rg_swiglu(g, x, wg, wu, *, tn):
    K = x.shape[1]
    N = wg.shape[-1]
    col = _last_if_dead(N // tn)
    wspec = g.weights(K, tn, lambda live, j: (0, col(live, j)))
    return pl.pallas_call(
        functools.partial(_rg_swiglu_body, sb=g.sb),
        grid_spec=pltpu.PrefetchScalarGridSpec(
            num_scalar_prefetch=3, grid=(g.n, N // tn),
            in_specs=[g.rows(K, lambda live, j: 0, buffers=1), wspec, wspec],
            out_specs=g.rows(tn, lambda live, j: j, own=True),
            scratch_shapes=[pltpu.VMEM((K, tn), BF16), pltpu.VMEM((K, tn), BF16)]),
        out_shape=jax.ShapeDtypeStruct((g.n * g.rg, N), BF16),
        compiler_params=_params("arbitrary", "arbitrary"),
        name="rg_swiglu",
    )(g.e, g.st, g.nb, x, wg, wu)


def _rg_ple(g, x, w, p, pw, resid, *, tn):
    K, KP = x.shape[1], p.shape[1]
    N = w.shape[-1]
    col = _last_if_dead(N // tn)
    wcol = lambda live, j: (0, col(live, j))
    tile = g.rows(tn, lambda live, j: j, own=True)
    return pl.pallas_call(
        functools.partial(_rg_ple_body, sb=g.sb),
        grid_spec=pltpu.PrefetchScalarGridSpec(
            num_scalar_prefetch=3, grid=(g.n, N // tn),
            in_specs=[g.rows(K, lambda live, j: 0, buffers=1), g.weights(K, tn, wcol),
                      g.rows(KP, lambda live, j: 0, buffers=1), g.weights(KP, tn, wcol), tile],
            out_specs=[tile, tile],
            scratch_shapes=[pltpu.VMEM((K, tn), BF16)]),
        out_shape=[jax.ShapeDtypeStruct((g.n * g.rg, N), F32), jax.ShapeDtypeStruct((g.n * g.rg, N), BF16)],
        compiler_params=_params("arbitrary", "arbitrary"),
        name="rg_ple",
    )(g.e, g.st, g.nb, x, w, p, pw, resid)


def _rg_down(g, a, w, *, tn, tk):
    K = a.shape[1]
    N = w.shape[-1]
    NT, KC = N // tn, K // tk
    ncol = lambda live, n, kc: jnp.where(live, n, NT - 1)
    kcol = lambda live, n, kc: jnp.where(live, kc, KC - 1)
    return pl.pallas_call(
        functools.partial(_rg_down_body, sb=g.sb),
        grid_spec=pltpu.PrefetchScalarGridSpec(
            num_scalar_prefetch=3, grid=(g.n, NT, KC),
            in_specs=[g.rows(tk, kcol, own=True),
                      g.weights(tk, tn, lambda live, n, kc: (kcol(live, n, kc), ncol(live, n, kc)))],
            out_specs=g.rows(tn, lambda live, n, kc: n, buffers=1, own=True),
            scratch_shapes=[pltpu.VMEM((tk, tn), BF16)]),
        out_shape=jax.ShapeDtypeStruct((g.n * g.rg, N), F32),
        compiler_params=_params("arbitrary", "arbitrary", "arbitrary"),
        name="rg_down",
    )(g.e, g.st, g.nb, a, w)


def _layer_norm_rows(z, g, b):
    mu = jnp.mean(z, -1, keepdims=True)
    d = z - mu
    var = jnp.mean(d * d, -1, keepdims=True)
    return d * lax.rsqrt(var + LN_EPS) * g + b


def _ln_body(a_ref, b_ref, g_ref, beta_ref, of_ref, ob_ref):
    y = _layer_norm_rows(ALPHA * a_ref[...] + b_ref[...], g_ref[...], beta_ref[...])
    of_ref[...] = y
    ob_ref[...] = y.astype(BF16)


def _residual_layer_norm(a, b, g, beta, *, tr):
    M, D = a.shape
    row = pl.BlockSpec((tr, D), lambda i: (i, 0))
    vec = pl.BlockSpec((1, D), lambda i: (0, 0))
    return pl.pallas_call(
        _ln_body,
        grid=(M // tr,),
        in_specs=[row, row, vec, vec],
        out_specs=[row, row],
        out_shape=[jax.ShapeDtypeStruct((M, D), F32), jax.ShapeDtypeStruct((M, D), BF16)],
        compiler_params=_params("parallel"),
        name="res_ln",
    )(a, b, g.reshape(1, D), beta.reshape(1, D))


def _rotate(x, c, s_up, s_down, half):
    w = x.shape[1]
    reps = w // LANES
    if reps > 1:
        c, s_up, s_down = (jnp.concatenate([t] * reps, axis=1) for t in (c, s_up, s_down))
    return x * c + pltpu.roll(x, w - half, 1) * s_up + pltpu.roll(x, half, 1) * s_down


def _rot_body(x_ref, c_ref, su_ref, sd_ref, *o_refs, half):
    y = _rotate(x_ref[...], c_ref[...], su_ref[...], sd_ref[...], half)
    for o_ref in o_refs:
        o_ref[...] = y.astype(o_ref.dtype)


def _rotary_cols(h, col_block, width, tables, half, out_dtypes, *, tr):
    M = h.shape[0]
    tab = pl.BlockSpec((tr, LANES), lambda i: (i, 0))
    return pl.pallas_call(
        functools.partial(_rot_body, half=half),
        grid=(M // tr,),
        in_specs=[pl.BlockSpec((tr, width), lambda i: (i, col_block)), tab, tab, tab],
        out_specs=[pl.BlockSpec((tr, width), lambda i: (i, 0)) for _ in out_dtypes],
        out_shape=[jax.ShapeDtypeStruct((M, width), dt) for dt in out_dtypes],
        compiler_params=_params("parallel"),
        name="rotary",
    )(h, *tables)


def _ik_body(x_ref, g_ref, b_ref, c_ref, su_ref, sd_ref, ik_ref, ik2_ref, iw_ref):
    x = x_ref[...]
    is_key = lax.broadcasted_iota(I32, x.shape, 1) < D_IDX
    mu = jnp.sum(jnp.where(is_key, x, 0.0), -1, keepdims=True) / D_IDX
    d = jnp.where(is_key, x - mu, 0.0)
    var = jnp.sum(d * d, -1, keepdims=True) / D_IDX
    y = jnp.where(is_key, d * lax.rsqrt(var + LN_EPS) * g_ref[...] + b_ref[...], 0.0)
    y = _rotate(y, c_ref[...], su_ref[...], sd_ref[...], IDX_ROPE_DIM // 2)
    y = jnp.where(is_key, y, 0.0)
    ik_ref[...] = y[:, :D_IDX]
    ik2_ref[...] = (y + pltpu.roll(y, D_IDX, 1)).astype(BF16)
    iw_ref[...] = pltpu.roll(x, LANES - D_IDX, 1)[:, :H_IDX]


def _indexer_keys(h, col_block, g, b, tables, *, tr):
    M = h.shape[0]
    pad = LANES - D_IDX
    tab = pl.BlockSpec((tr, LANES), lambda i: (i, 0))
    vec = pl.BlockSpec((1, LANES), lambda i: (0, 0))
    return pl.pallas_call(
        _ik_body,
        grid=(M // tr,),
        in_specs=[pl.BlockSpec((tr, LANES), lambda i: (i, col_block)), vec, vec, tab, tab, tab],
        out_specs=[pl.BlockSpec((tr, D_IDX), lambda i: (i, 0)), pl.BlockSpec((tr, LANES), lambda i: (i, 0)),
                   pl.BlockSpec((tr, H_IDX), lambda i: (i, 0))],
        out_shape=[jax.ShapeDtypeStruct((M, D_IDX), F32), jax.ShapeDtypeStruct((M, LANES), BF16),
                   jax.ShapeDtypeStruct((M, H_IDX), F32)],
        compiler_params=_params("parallel"),
        name="indexer_keys",
    )(h, jnp.pad(g, (0, pad)).reshape(1, LANES), jnp.pad(b, (0, pad)).reshape(1, LANES), *tables)


def _router_body(l_ref, e_ref, g_ref):
    lg = l_ref[...]
    idx = lax.broadcasted_iota(I32, lg.shape, 1)
    m1 = jnp.max(lg, -1, keepdims=True)
    e1 = jnp.min(jnp.where(lg == m1, idx, N_EXPERTS), -1, keepdims=True)
    rest = jnp.where(idx == e1, -jnp.inf, lg)
    m2 = jnp.max(rest, -1, keepdims=True)
    e2 = jnp.min(jnp.where(rest == m2, idx, N_EXPERTS), -1, keepdims=True)
    ex2 = jnp.exp(m2 - m1)
    den = 1.0 + ex2
    e_ref[...] = jnp.concatenate([e1, e2], axis=1)
    g_ref[...] = jnp.concatenate([1.0 / den, ex2 / den], axis=1)


def _router_top2(logits, *, tr):
    M, E = logits.shape
    return pl.pallas_call(
        _router_body,
        grid=(M // tr,),
        in_specs=[pl.BlockSpec((tr, E), lambda i: (i, 0))],
        out_specs=[pl.BlockSpec((tr, TOP_K), lambda i: (i, 0)), pl.BlockSpec((tr, TOP_K), lambda i: (i, 0))],
        out_shape=[jax.ShapeDtypeStruct((M, TOP_K), I32), jax.ShapeDtypeStruct((M, TOP_K), F32)],
        compiler_params=_params("parallel"),
        name="router_top2",
    )(logits)


def _row_copy(src_hbm, row, dst, r, sem):
    return pltpu.make_async_copy(src_hbm.at[pl.ds(row, 1), :], dst.at[pl.ds(r, 1), :], sem)


def _gather_body(tok_ref, x_hbm, o_ref, buf, sem, *, tb):
    base = pl.program_id(0) * tb

    def start(r, _):
        _row_copy(x_hbm, tok_ref[base + r], buf, r, sem).start()
        return 0

    def wait(r, _):
        _row_copy(x_hbm, 0, buf, r, sem).wait()
        return 0

    lax.fori_loop(0, tb, start, 0)
    lax.fori_loop(0, tb, wait, 0)
    o_ref[...] = buf[...].astype(BF16)


def _gather_rows(slot_tok, x, *, tb):
    A = slot_tok.shape[0]
    D = x.shape[1]
    return pl.pallas_call(
        functools.partial(_gather_body, tb=tb),
        grid_spec=pltpu.PrefetchScalarGridSpec(
            num_scalar_prefetch=1,
            grid=(A // tb,),
            in_specs=[pl.BlockSpec(memory_space=pl.ANY)],
            out_specs=pl.BlockSpec((tb, D), lambda i, tok: (i, 0)),
            scratch_shapes=[pltpu.VMEM((tb, D), F32), pltpu.SemaphoreType.DMA(())]),
        out_shape=jax.ShapeDtypeStruct((A, D), BF16),
        compiler_params=_params("arbitrary"),
        name="moe_gather",
    )(slot_tok, x)


def _combine_body(slot_ref, y_hbm, gate_ref, x_ref, g_ref, beta_ref, of_ref, ob_ref, buf, sem, *, tb):
    base = pl.program_id(0) * tb

    def start(r, _):
        for k in range(TOP_K):
            _row_copy(y_hbm, slot_ref[(base + r) * TOP_K + k], buf.at[k], r, sem).start()
        return 0

    def wait(r, _):
        for k in range(TOP_K):
            _row_copy(y_hbm, 0, buf.at[k], r, sem).wait()
        return 0

    lax.fori_loop(0, tb, start, 0)
    lax.fori_loop(0, tb, wait, 0)
    gates = gate_ref[...]
    y = buf[0] * gates[:, 0:1] + buf[1] * gates[:, 1:2]
    z = _layer_norm_rows(ALPHA * x_ref[...] + y, g_ref[...], beta_ref[...])
    of_ref[...] = z
    ob_ref[...] = z.astype(BF16)


def _moe_combine_ln(tok_slot, yb, gates, x, g, beta, *, tb):
    M, D = x.shape
    row = pl.BlockSpec((tb, D), lambda i, s: (i, 0))
    vec = pl.BlockSpec((1, D), lambda i, s: (0, 0))
    return pl.pallas_call(
        functools.partial(_combine_body, tb=tb),
        grid_spec=pltpu.PrefetchScalarGridSpec(
            num_scalar_prefetch=1,
            grid=(M // tb,),
            in_specs=[pl.BlockSpec(memory_space=pl.ANY), pl.BlockSpec((tb, TOP_K), lambda i, s: (i, 0)),
                      row, vec, vec],
            out_specs=[row, row],
            scratch_shapes=[pltpu.VMEM((TOP_K, tb, D), F32), pltpu.SemaphoreType.DMA(())]),
        out_shape=[jax.ShapeDtypeStruct((M, D), F32), jax.ShapeDtypeStruct((M, D), BF16)],
        compiler_params=_params("arbitrary"),
        name="moe_combine",
    )(tok_slot.reshape(-1), yb, gates, x, g.reshape(1, D), beta.reshape(1, D))


def _ret_body(q_ref, k_ref, v_ref, gate_ref, c_ref, su_ref, sd_ref, lg_ref, gn_ref, s0_ref, _, o_ref, st_ref,
              s_scr, *, C):
    @pl.when(pl.program_id(1) == 0)
    def _():
        s_scr[...] = s0_ref[...]

    c, su, sd = c_ref[...], su_ref[...], sd_ref[...]
    q = _rotate(q_ref[...], c, su, sd, HEAD_DIM // 2)
    k = _rotate(k_ref[...], c, su, sd, HEAD_DIM // 2) * HEAD_DIM ** -0.5
    vb = v_ref[...].astype(BF16)
    lg_row = lg_ref[0:1, :C]
    lg_lane = lg_ref[0:1, :LANES]
    n_col = lax.broadcasted_iota(I32, (C, C), 0)
    m_row = lax.broadcasted_iota(I32, (C, C), 1)
    diff = (n_col - m_row).astype(F32)
    intra = jnp.where(diff >= 0, jnp.exp(lg_row * jnp.maximum(diff, 0.0)), 0.0)
    n_idx = lax.broadcasted_iota(I32, (C, LANES), 0).astype(F32)
    q_dec = jnp.exp(lg_lane * (n_idx + 1.0))
    k_dec = jnp.exp(lg_lane * (C - 1.0 - n_idx))
    c_dec = jnp.exp(lg_lane * C)

    qb = q.astype(BF16)
    s = lax.dot_general(qb, k.astype(BF16), (((1,), (1,)), ((), ())), preferred_element_type=F32) * intra
    state = s_scr[...]
    o = (jnp.dot(s.astype(BF16), vb, preferred_element_type=F32)
         + jnp.dot(qb, state.astype(BF16), preferred_element_type=F32) * q_dec)
    kd = (k * k_dec).astype(BF16)
    state = state * c_dec + lax.dot_general(kd, vb, (((0,), (0,)), ((), ())), preferred_element_type=F32)
    s_scr[...] = state
    st_ref[...] = state

    mu = jnp.mean(o, -1, keepdims=True)
    d = o - mu
    var = jnp.mean(d * d, -1, keepdims=True)
    on = d * lax.rsqrt(var + LN_EPS) * gn_ref[...]
    o_ref[...] = (on * jax.nn.silu(gate_ref[...])).astype(o_ref.dtype)


def _retention(h, tables, lg_tab, gn_g, state0, out_buf, *, C, n_seq, n_chunks, row0):
    rb0 = row0 // C

    def rows(sh, c):
        return rb0 + (sh // H_RET) * n_chunks + c

    def hcol(off):
        return pl.BlockSpec((C, HEAD_DIM), lambda sh, c, off=off: (rows(sh, c), off + sh % H_RET))

    tab = pl.BlockSpec((C, LANES), lambda sh, c: (rows(sh, c), 0))
    in_specs = [hcol(0), hcol(H_RET), hcol(2 * H_RET), hcol(3 * H_RET), tab, tab, tab,
                pl.BlockSpec((None, 8, lg_tab.shape[2]), lambda sh, c: (sh % H_RET, 0, 0)),
                pl.BlockSpec((1, HEAD_DIM), lambda sh, c: (0, sh % H_RET)),
                pl.BlockSpec((None, HEAD_DIM, HEAD_DIM), lambda sh, c: (sh, 0, 0)),
                pl.BlockSpec(memory_space=pl.ANY)]
    state_shape = (n_seq * H_RET, HEAD_DIM, HEAD_DIM)
    return pl.pallas_call(
        functools.partial(_ret_body, C=C),
        grid=(n_seq * H_RET, n_chunks),
        in_specs=in_specs,
        out_specs=[pl.BlockSpec((C, HEAD_DIM), lambda sh, c: (rows(sh, c), sh % H_RET)),
                   pl.BlockSpec((None, HEAD_DIM, HEAD_DIM), lambda sh, c: (sh, 0, 0))],
        out_shape=[jax.ShapeDtypeStruct(out_buf.shape, out_buf.dtype), jax.ShapeDtypeStruct(state_shape, F32)],
        scratch_shapes=[pltpu.VMEM((HEAD_DIM, HEAD_DIM), F32)],
        input_output_aliases={len(in_specs) - 1: 0},
        compiler_params=_params("arbitrary", "arbitrary"),
        name="retention",
    )(h, h, h, h, *tables, lg_tab, gn_g.reshape(1, -1), state0.reshape(state_shape), out_buf)


def _dsa_body(q_ref, iq_ref, iwt_ref, k_ref, v_ref, ik2_ref, *rest, TQ, TK, L, n_sel, q_pos0, q_stride, causal):
    o_ref, key_scr, iqm_scr, q4_scr, m_scr, l_scr, acc_scr = rest[-7:]
    G = DSA_GROUP
    i = pl.program_id(0)
    q_pos = q_pos0 + i * q_stride + lax.broadcasted_iota(I32, (1, TQ), 1)
    q_lim = ((q_pos >> 6) + 1) << 6
    if causal:
        n_tiles = jnp.minimum(((i + 1) * q_stride + TK - 1) // TK, L // TK)
    else:
        n_tiles = L // TK

    lane = lax.broadcasted_iota(I32, (TQ, LANES), 1)
    for h in range(H_IDX):
        pair = iq_ref[:, (h // 2) * LANES:(h // 2 + 1) * LANES]
        keep = (lane < D_IDX) if h % 2 == 0 else (lane >= D_IDX)
        iqm_scr[h] = jnp.where(keep, pair, jnp.zeros_like(pair))
    iwt = iwt_ref[...] * (H_IDX ** -0.5 * D_IDX ** -0.5)

    def key_pos(kt):
        return kt * TK + lax.broadcasted_iota(I32, (TK, TQ), 0)

    def score_tile(kt, _):
        ik2 = ik2_ref[pl.ds(kt * TK, TK), :]
        acc = jnp.zeros((TK, TQ), F32)
        for h in range(H_IDX):
            a = lax.dot_general(ik2, iqm_scr[h], (((1,), (1,)), ((), ())), preferred_element_type=F32)
            acc = acc + jnp.maximum(a, 0.0) * iwt[h:h + 1, :]
        acc = jnp.where(key_pos(kt) < q_lim, acc, -jnp.inf)
        bits = pltpu.bitcast(acc, I32)
        key_scr[pl.ds(kt * TK, TK), :] = bits ^ ((bits >> 31) & INT_MAX)
        return 0

    lax.fori_loop(0, n_tiles, score_tile, 0)

    def bisect(_, carry):
        lo, hi = carry
        mid = lo + lax.shift_right_logical(hi - lo, 1)

        def count_tile(kt, c8):
            ge = (key_scr[pl.ds(kt * TK, TK), :] >= mid).astype(I32)
            return c8 + jnp.sum(ge.reshape(TK // 8, 8, TQ), axis=0)

        c8 = lax.fori_loop(0, n_tiles, count_tile, jnp.zeros((8, TQ), I32))
        enough = jnp.sum(c8, axis=0, keepdims=True) >= n_sel
        return jnp.where(enough, mid, lo), jnp.where(enough, hi, mid)

    thr, _ = lax.fori_loop(0, 32, bisect, (jnp.full((1, TQ), INT_MIN, I32), jnp.full((1, TQ), INT_MAX, I32)))

    for kv in range(KVH_DSA):
        q4_scr[kv] = jnp.concatenate(
            [q_ref[:, (kv * G + g) * HEAD_DIM:(kv * G + g + 1) * HEAD_DIM] for g in range(G)], axis=0)
    m_scr[...] = jnp.full_like(m_scr, NEG_BIG)
    l_scr[...] = jnp.zeros_like(l_scr)
    acc_scr[...] = jnp.zeros_like(acc_scr)

    def attend_tile(kt, _):
        sel = jnp.logical_and(key_scr[pl.ds(kt * TK, TK), :] >= thr, key_pos(kt) < q_lim)
        sel4 = jnp.concatenate([sel] * G, axis=1)
        for kv in range(KVH_DSA):
            cols = slice(kv * HEAD_DIM, (kv + 1) * HEAD_DIM)
            kt_ = k_ref[pl.ds(kt * TK, TK), cols]
            vt_ = v_ref[pl.ds(kt * TK, TK), cols]
            lg = lax.dot_general(kt_, q4_scr[kv], (((1,), (1,)), ((), ())), preferred_element_type=F32)
            lg = jnp.where(sel4, lg, NEG_BIG)
            m_old = m_scr[kv]
            m_new = jnp.maximum(m_old, jnp.max(lg, axis=0, keepdims=True))
            alpha = jnp.exp2((m_old - m_new) * EXP2_SCALE)
            p = jnp.exp2((lg - m_new) * EXP2_SCALE)
            l_scr[kv] = alpha * l_scr[kv] + jnp.sum(p, axis=0, keepdims=True)
            pv = lax.dot_general(vt_, p.astype(BF16), (((0,), (0,)), ((), ())), preferred_element_type=F32)
            acc_scr[kv] = acc_scr[kv] * alpha + pv
            m_scr[kv] = m_new
        return 0

    lax.fori_loop(0, n_tiles, attend_tile, 0)

    for kv in range(KVH_DSA):
        o_t = acc_scr[kv] / l_scr[kv]
        for g in range(G):
            hd = kv * G + g
            o_ref[:, hd * HEAD_DIM:(hd + 1) * HEAD_DIM] = o_t[:, g * TQ:(g + 1) * TQ].T.astype(o_ref.dtype)


def _dsa(q, iq, iwt, k, v, ik2, out_buf, *, TQ, TK, n_blocks, n_sel, q_pos0, q_stride, causal,
         q_row0, out_row0, out_col0, per_block_keys):
    L = k.shape[-2]
    W = H_DSA * HEAD_DIM
    qb0, ob0, oc0 = q_row0 // TQ, out_row0 // TQ, out_col0 // W
    if per_block_keys:
        kspec = lambda a: pl.BlockSpec((None, L, a.shape[-1]), lambda i: (i, 0, 0))
    else:
        kspec = lambda a: pl.BlockSpec((L, a.shape[-1]), lambda i: (0, 0), pipeline_mode=pl.Buffered(1))
    G = DSA_GROUP
    return pl.pallas_call(
        functools.partial(_dsa_body, TQ=TQ, TK=TK, L=L, n_sel=n_sel, q_pos0=q_pos0, q_stride=q_stride,
                          causal=causal),
        grid=(n_blocks,),
        in_specs=[pl.BlockSpec((TQ, W), lambda i: (qb0 + i, 0)),
                  pl.BlockSpec((TQ, H_IDX * D_IDX), lambda i: (qb0 + i, 0)),
                  pl.BlockSpec((H_IDX, TQ), lambda i: (0, qb0 + i)),
                  kspec(k), kspec(v), kspec(ik2),
                  pl.BlockSpec(memory_space=pl.ANY)],
        out_specs=pl.BlockSpec((TQ, W), lambda i: (ob0 + i, oc0)),
        out_shape=jax.ShapeDtypeStruct(out_buf.shape, out_buf.dtype),
        scratch_shapes=[pltpu.VMEM((L, TQ), I32), pltpu.VMEM((H_IDX, TQ, LANES), BF16),
                        pltpu.VMEM((KVH_DSA, G * TQ, HEAD_DIM), BF16),
                        pltpu.VMEM((KVH_DSA, 1, G * TQ), F32), pltpu.VMEM((KVH_DSA, 1, G * TQ), F32),
                        pltpu.VMEM((KVH_DSA, HEAD_DIM, G * TQ), F32)],
        input_output_aliases={6: 0},
        compiler_params=_params("arbitrary"),
        name="dsa",
    )(q, iq, iwt, k, v, ik2, out_buf)


def _bias_body(rb_ref, idx_ref, o_ref):
    n = rb_ref.shape[1]
    onehot = (lax.broadcasted_iota(I32, (n, idx_ref.shape[1]), 0) == idx_ref[...]).astype(F32)
    o_ref[...] = jnp.dot(rb_ref[...], onehot, preferred_element_type=F32, precision=lax.Precision.HIGHEST)


def _expand_bias(rel_bias, n_q, n_k, *, tn):
    H, R = rel_bias.shape
    RP = 3 * LANES
    t = np.arange(n_q)[:, None]
    s = np.arange(n_k)[None, :]
    idx = (np.clip(LEFT_CHUNKS * CHUNK + t - s, -REL_CLIP, REL_CLIP) + REL_CLIP).reshape(1, -1).astype(np.int32)
    out = pl.pallas_call(
        _bias_body,
        grid=(idx.shape[1] // tn,),
        in_specs=[pl.BlockSpec((H, RP), lambda j: (0, 0)), pl.BlockSpec((1, tn), lambda j: (0, j))],
        out_specs=pl.BlockSpec((H, tn), lambda j: (0, j)),
        out_shape=jax.ShapeDtypeStruct((H, idx.shape[1]), F32),
        compiler_params=_params("parallel"),
        name="rel_bias_expand",
    )(jnp.pad(rel_bias, ((0, 0), (0, RP - R))), jnp.asarray(idx))
    return out.reshape(H, n_q, n_k)


def _band_heads(q, k_parts, v_parts, bias_ref, ok, o_ref, HB):
    for hh in range(HB):
        cols = slice(hh * HEAD_DIM, (hh + 1) * HEAD_DIM)
        qh = q[:, cols].astype(BF16)
        kw = jnp.concatenate([p[:, cols] for p in k_parts], axis=0).astype(BF16)
        vw = jnp.concatenate([p[:, cols] for p in v_parts], axis=0).astype(BF16)
        s = lax.dot_general(qh, kw, (((1,), (1,)), ((), ())), preferred_element_type=F32)
        s = s * HEAD_DIM ** -0.5 + bias_ref[hh]
        if ok is not None:
            s = jnp.where(ok, s, NEG_BIG)
        p = jnp.exp(s - jnp.max(s, -1, keepdims=True))
        pr = (p / jnp.sum(p, -1, keepdims=True)).astype(BF16)
        o_ref[:, cols] = jnp.dot(pr, vw, preferred_element_type=F32).astype(o_ref.dtype)


def _band_prompt_body(q_ref, *rest, HB, TQ, NP):
    k_refs, v_refs = rest[:NP], rest[NP:2 * NP]
    bias_ref, o_ref = rest[2 * NP], rest[2 * NP + 2]
    j = pl.program_id(1)
    nk = NP * TQ
    q_pos = j * TQ + lax.broadcasted_iota(I32, (TQ, nk), 0)
    k_pos = (j - (NP - 1)) * TQ + lax.broadcasted_iota(I32, (TQ, nk), 1)
    qc, kc = q_pos >> 6, k_pos >> 6
    ok = jnp.logical_and(jnp.logical_and(k_pos >= 0, kc <= qc), kc >= qc - LEFT_CHUNKS)
    _band_heads(q_ref[...], [r[...] for r in k_refs], [r[...] for r in v_refs], bias_ref, ok, o_ref, HB)


def _band_prompt(qkv, bias, out_buf, *, n_blocks, HB, TQ, NP):
    W = HB * HEAD_DIM
    n_hg = H_CHK // HB

    def kv_spec(r, third):
        return pl.BlockSpec((TQ, W), lambda hg, j: (jnp.maximum(j - (NP - 1) + r, 0), third * n_hg + hg))

    return pl.pallas_call(
        functools.partial(_band_prompt_body, HB=HB, TQ=TQ, NP=NP),
        grid=(n_hg, n_blocks),
        in_specs=([pl.BlockSpec((TQ, W), lambda hg, j: (j, hg))]
                  + [kv_spec(r, 1) for r in range(NP)] + [kv_spec(r, 2) for r in range(NP)]
                  + [pl.BlockSpec((HB, TQ, NP * TQ), lambda hg, j: (hg, 0, 0)), pl.BlockSpec(memory_space=pl.ANY)]),
        out_specs=pl.BlockSpec((TQ, W), lambda hg, j: (j, hg)),
        out_shape=jax.ShapeDtypeStruct(out_buf.shape, out_buf.dtype),
        input_output_aliases={2 * NP + 2: 0},
        compiler_params=_params("parallel", "parallel"),
        name="band_prompt",
    )(qkv, *([qkv] * (2 * NP)), bias, out_buf)


def _band_sample_body(q_ref, kc_ref, kn_ref, vc_ref, vn_ref, bias_ref, _, o_ref, *, HB):
    _band_heads(q_ref[...], [kc_ref[...], kn_ref[...]], [vc_ref[...], vn_ref[...]], bias_ref, None, o_ref, HB)


def _band_sample(qkv, cache_k, cache_v, bias, out_buf, *, n_seq, T, row0, HB):
    P = cache_k.shape[1]
    W = HB * HEAD_DIM
    n_hg = H_CHK // HB
    rb0 = row0 // T
    new = lambda third: pl.BlockSpec((T, W), lambda b, hg: (rb0 + b, third * n_hg + hg))
    cache = pl.BlockSpec((None, P, W), lambda b, hg: (b, 0, hg))
    return pl.pallas_call(
        functools.partial(_band_sample_body, HB=HB),
        grid=(n_seq, n_hg),
        in_specs=[new(0), cache, new(1), cache, new(2),
                  pl.BlockSpec((HB, T, P + T), lambda b, hg: (hg, 0, 0)), pl.BlockSpec(memory_space=pl.ANY)],
        out_specs=pl.BlockSpec((T, W), lambda b, hg: (rb0 + b, hg)),
        out_shape=jax.ShapeDtypeStruct(out_buf.shape, out_buf.dtype),
        input_output_aliases={6: 0},
        compiler_params=_params("parallel", "parallel"),
        name="band_sample",
    )(qkv, cache_k, qkv, cache_v, qkv, bias, out_buf)


def _rot_tables(pos, head_dim, rot_dim, theta):
    half = rot_dim // 2
    inv = theta ** (-jnp.arange(half, dtype=F32) / half)
    ang = pos.astype(F32)[:, None] * inv[None, :]
    cos, sin = jnp.cos(ang), jnp.sin(ang)
    m = pos.shape[0]
    zh = jnp.zeros((m, half), F32)
    rest0 = jnp.zeros((m, head_dim - rot_dim), F32)
    c = jnp.concatenate([cos, cos, jnp.ones((m, head_dim - rot_dim), F32)], 1)
    s_up = jnp.concatenate([-sin, zh, rest0], 1)
    s_down = jnp.concatenate([zh, sin, rest0], 1)
    reps = LANES // head_dim
    return tuple(jnp.tile(t, (1, reps)) for t in (c, s_up, s_down))


def _log_gamma_table(width):
    lg = jnp.log1p(-(2.0 ** (-5.0 - jnp.arange(H_RET, dtype=F32))))
    return jnp.broadcast_to(lg[:, None, None], (H_RET, 8, width))


def _routing(top_e, sb, rg):
    n = top_e.shape[0]
    a = n * TOP_K
    n_rows = (-(-a // sb) + N_EXPERTS) * sb + rg
    n_groups = N_EXPERTS + a // rg
    flat_e = top_e.reshape(-1)
    onehot = (flat_e[:, None] == jnp.arange(N_EXPERTS, dtype=I32)[None, :]).astype(I32)
    rank = jnp.take_along_axis(jnp.cumsum(onehot, 0), flat_e[:, None], 1)[:, 0] - 1
    counts = jnp.sum(onehot, 0)
    padded = (counts + sb - 1) // sb * sb
    pad_end = jnp.cumsum(padded)
    pad_start = pad_end - padded
    slot = pad_start[flat_e] + rank
    slot_tok = jnp.zeros((n_rows,), I32).at[slot].set(jnp.arange(a, dtype=I32) // TOP_K)

    g_count = (padded + rg - 1) // rg
    g_end = jnp.cumsum(g_count)
    g_start = g_end - g_count
    s = jnp.arange(n_groups, dtype=I32)
    live = s < g_end[-1]
    expert = jnp.minimum(jnp.searchsorted(g_end, jnp.where(live, s, g_end[-1] - 1), side='right'), N_EXPERTS - 1)
    k = s - g_start[expert]
    nb = jnp.where(live, jnp.clip(padded[expert] - k * rg, 0, rg) // sb, 0)
    st = jnp.where(live, pad_start[expert] + k * rg, pad_end[-1]) // sb
    groups = _Groups(expert.astype(I32), st.astype(I32), nb.astype(I32), rg, sb, False)
    own_row = (g_start[flat_e] + rank // rg) * rg + rank % rg
    return own_row.reshape(n, TOP_K).astype(I32), slot_tok, groups


RG = 2080
TN = 512
TN_PAIR = 256
TN_DOWN = 2048
TK_DOWN = 1024
TR = 208
RG_MOE = 2560
SB_MOE = 256
RET_CHUNK = 256
DSA_TQ = 128
DSA_TQ_S = 128
DSA_TK = 512
BAND_HB = 4
BAND_TQ = 128
BAND_PARTS = (LEFT_CHUNKS * CHUNK) // BAND_TQ + 1


def kernel(x_prompt, x_sample, p_prompt, p_sample, state_ret, cache_dsa_k, cache_dsa_v, cache_dsa_kidx,
           cache_chk_k, cache_chk_v, ln_mix_g, ln_mix_b, ln_ffn_g, ln_ffn_b, ple_proj, ple_gate,
           w_in_even, ret_gn_g, kidx_ln_g, kidx_ln_b, w_out_even, ffn_w_gate, ffn_w_up, ffn_w_down,
           w_in_odd, rel_bias, w_out_odd, router_w, exp_w_gate, exp_w_up, exp_w_down):
    seq, d = x_prompt.shape[1], x_prompt.shape[2]
    nb_s, t_s = x_sample.shape[0], x_sample.shape[1]
    n_s = nb_s * t_s
    m = seq + n_s
    past = cache_dsa_k.shape[2]
    ret_w = H_RET * HEAD_DIM
    dsa_w = H_DSA * HEAD_DIM
    kv_w = KVH_DSA * HEAD_DIM

    x = jnp.concatenate([x_prompt[0], x_sample.reshape(n_s, d)], 0)
    p = jnp.concatenate([p_prompt[:, 0], p_sample.reshape(DEPTH, n_s, -1)], 1).astype(BF16)
    pos = jnp.concatenate([jnp.arange(seq, dtype=I32), past + jnp.tile(jnp.arange(t_s, dtype=I32), nb_s)])
    tab_ret = _rot_tables(pos, HEAD_DIM, HEAD_DIM, RET_THETA)
    tab_dsa = _rot_tables(pos, HEAD_DIM, ROPE_DIM, ROPE_THETA)
    tab_idx = _rot_tables(pos, D_IDX, IDX_ROPE_DIM, ROPE_THETA)

    tokens = [_dense_groups(m, RG, lead=i) for i in range(DEPTH)]
    h = _rg_matmul(tokens[0], x.astype(BF16), w_in_even, tn=TN)
    c_dq = 4 * ret_w
    c_dk = c_dq + dsa_w
    c_dv = c_dk + kv_w
    c_iq = c_dv + kv_w
    c_ik = c_iq + H_IDX * D_IDX

    mix_in = jnp.zeros((m, ret_w + dsa_w), BF16)
    state0_p = jnp.zeros((1, H_RET, HEAD_DIM, HEAD_DIM), F32)
    mix_in, st_p = _retention(h, tab_ret, _log_gamma_table(RET_CHUNK), ret_gn_g[0], state0_p, mix_in,
                              C=RET_CHUNK, n_seq=1, n_chunks=seq // RET_CHUNK, row0=0)
    mix_in, st_s = _retention(h, tab_ret, _log_gamma_table(LANES), ret_gn_g[0], state_ret[0], mix_in,
                              C=t_s, n_seq=nb_s, n_chunks=1, row0=seq)

    (dq,) = _rotary_cols(h, c_dq // dsa_w, dsa_w, tab_dsa, ROPE_DIM // 2, [BF16], tr=TR)
    dk, dk_b = _rotary_cols(h, c_dk // kv_w, kv_w, tab_dsa, ROPE_DIM // 2, [F32, BF16], tr=TR)
    iq_lo, = _rotary_cols(h, c_iq // 1024, 1024, tab_idx, IDX_ROPE_DIM // 2, [BF16], tr=TR)
    iq_hi, = _rotary_cols(h, c_iq // 1024 + 1, 1024, tab_idx, IDX_ROPE_DIM // 2, [BF16], tr=TR)
    iq = jnp.concatenate([iq_lo, iq_hi], 1)
    ik, ik2, iw = _indexer_keys(h, c_ik // LANES, kidx_ln_g[0], kidx_ln_b[0], tab_idx, tr=TR)
    dv = h[:, c_dv:c_dv + kv_w]
    dv_b = dv.astype(BF16)
    iwt = iw.T

    n_sel_p = min(TOPK_MAX, seq // 4)
    mix_in = _dsa(dq, iq, iwt, dk_b[:seq], dv_b[:seq], ik2[:seq], mix_in, TQ=DSA_TQ, TK=DSA_TK,
                  n_blocks=seq // DSA_TQ, n_sel=n_sel_p, q_pos0=0, q_stride=DSA_TQ, causal=True,
                  q_row0=0, out_row0=0, out_col0=ret_w, per_block_keys=False)

    def pad_queries(a):
        a = a[seq:].reshape(nb_s, t_s, -1)
        return jnp.pad(a, ((0, 0), (0, DSA_TQ_S - t_s), (0, 0))).reshape(nb_s * DSA_TQ_S, -1)

    def with_cache(cache, new):
        return jnp.concatenate([cache.reshape(nb_s, past, -1).astype(BF16), new[seq:].reshape(nb_s, t_s, -1)], 1)

    kidx_c = cache_dsa_kidx[0].astype(BF16)
    l_s = past + t_s
    do_s = _dsa(pad_queries(dq), pad_queries(iq), pad_queries(iw).T,
                with_cache(cache_dsa_k[0], dk_b), with_cache(cache_dsa_v[0], dv_b),
                with_cache(jnp.concatenate([kidx_c, kidx_c], -1), ik2),
                jnp.zeros((nb_s * DSA_TQ_S, dsa_w), BF16), TQ=DSA_TQ_S, TK=l_s, n_blocks=nb_s,
                n_sel=min(TOPK_MAX, l_s // 4), q_pos0=past, q_stride=0, causal=False,
                q_row0=0, out_row0=0, out_col0=0, per_block_keys=True)
    do_s = do_s.reshape(nb_s, DSA_TQ_S, dsa_w)[:, :t_s].reshape(n_s, dsa_w)
    mix_in = lax.dynamic_update_slice(mix_in, do_s, (seq, ret_w))

    mix = _rg_matmul(tokens[0], mix_in, w_out_even, tn=TN)
    x1, x1b = _residual_layer_norm(x, mix, ln_mix_g[0], ln_mix_b[0], tr=TR)
    act = _rg_swiglu(tokens[0], x1b, ffn_w_gate, ffn_w_up, tn=TN_PAIR)
    ffn = _rg_down(tokens[0], act, ffn_w_down, tn=TN_DOWN, tk=TK_DOWN)
    x2, x2b = _residual_layer_norm(x1, ffn, ln_ffn_g[0], ln_ffn_b[0], tr=TR)
    x3, x3b = _rg_ple(tokens[0], x2b, ple_gate, p[0], ple_proj, x2, tn=TN_PAIR)

    qkv = _rg_matmul(tokens[0], x3b, w_in_odd, tn=TN)
    chk_w = H_CHK * HEAD_DIM
    p_band = cache_chk_k.shape[2]
    bias_p = _expand_bias(rel_bias[0], BAND_TQ, BAND_PARTS * BAND_TQ, tn=2048)
    bias_s = _expand_bias(rel_bias[0], t_s, p_band + t_s, tn=(t_s * (p_band + t_s)) // 2)
    att = _band_prompt(qkv, bias_p, jnp.zeros((m, chk_w), BF16), n_blocks=seq // BAND_TQ, HB=BAND_HB, TQ=BAND_TQ,
                       NP=BAND_PARTS)
    att = _band_sample(qkv, cache_chk_k[0].reshape(nb_s, p_band, chk_w), cache_chk_v[0].reshape(nb_s, p_band, chk_w),
                       bias_s, att, n_seq=nb_s, T=t_s, row0=seq, HB=BAND_HB)
    mix = _rg_matmul(tokens[0], att, w_out_odd, tn=TN)
    x4, x4b = _residual_layer_norm(x3, mix, ln_mix_g[1], ln_mix_b[1], tr=TR)

    logits = _rg_matmul(tokens[0], x4b, router_w, tn=N_EXPERTS)
    top_e, gates = _router_top2(logits, tr=TR)
    tok_slot, slot_tok, experts = _routing(top_e, SB_MOE, RG_MOE)
    xs = _gather_rows(slot_tok, x4, tb=SB_MOE)
    acts = _rg_swiglu(experts, xs, exp_w_gate[0], exp_w_up[0], tn=TN_PAIR)
    yb = _rg_down(experts, acts, exp_w_down[0], tn=TN_DOWN, tk=TK_DOWN)
    x5, x5b = _moe_combine_ln(tok_slot, yb, gates, x4, ln_ffn_g[1], ln_ffn_b[1], tb=TR)
    y, _ = _rg_ple(tokens[1], x5b, ple_gate, p[1], ple_proj, x5, tn=TN_PAIR)

    keep = min(LEFT_CHUNKS * CHUNK, seq)
    ck = qkv[:, chk_w:2 * chk_w]
    cv = qkv[:, 2 * chk_w:]

    def heads(a, nh):
        return a.reshape(a.shape[0], nh, HEAD_DIM)

    return (y[:seq][None], y[seq:].reshape(nb_s, t_s, d),
            st_p.reshape(1, 1, H_RET, HEAD_DIM, HEAD_DIM), st_s.reshape(1, nb_s, H_RET, HEAD_DIM, HEAD_DIM),
            heads(dk[:seq], KVH_DSA)[None, None], heads(dv[:seq], KVH_DSA)[None, None], ik[:seq][None, None],
            heads(dk[seq:], KVH_DSA).reshape(1, nb_s, t_s, KVH_DSA, HEAD_DIM),
            heads(dv[seq:], KVH_DSA).reshape(1, nb_s, t_s, KVH_DSA, HEAD_DIM),
            ik[seq:].reshape(1, nb_s, t_s, D_IDX),
            heads(ck[seq - keep:seq], H_CHK)[None, None], heads(cv[seq - keep:seq], H_CHK)[None, None],
            heads(ck[seq:], H_CHK).reshape(1, nb_s, t_s, H_CHK, HEAD_DIM),
            heads(cv[seq:], H_CHK).reshape(1, nb_s, t_s, H_CHK, HEAD_DIM))
```

```python
import functools
import math

import numpy as np
import jax
import jax.numpy as jnp
from jax import lax
from jax.experimental import pallas as pl
from jax.experimental.pallas import tpu as pltpu

F32 = jnp.float32
BF16 = jnp.bfloat16
I32 = jnp.int32

CHUNK = 64
HEAD_DIM = 128
H_RET = 16
H_DSA = 16
KVH_DSA = 4
DSA_GROUP = H_DSA // KVH_DSA
H_IDX = 32
D_IDX = 64
TOPK_MAX = 256
H_CHK = 32
LEFT_CHUNKS = 8
REL_CLIP = 128
N_EXPERTS = 8
TOP_K = 2
RET_THETA = 10000.0
ROPE_THETA = 500000.0
ROPE_DIM = HEAD_DIM // 4
IDX_ROPE_DIM = D_IDX // 4
LN_EPS = 1e-5
DEPTH = 2
ALPHA = (2.0 * DEPTH) ** 0.25

LANES = 128
DMA_PRIORITIES = 2
VMEM_LIMIT = 56 * 1024 * 1024
NEG_BIG = -1e30
EXP2_SCALE = HEAD_DIM ** -0.5 * math.log2(math.e)
INT_MIN = -(2 ** 31)
INT_MAX = 2 ** 31 - 1


def _params(*sem):
    return pltpu.CompilerParams(dimension_semantics=sem, vmem_limit_bytes=VMEM_LIMIT)


class _Groups:
    def __init__(self, e, st, nb, rg, sb, dense):
        self.e, self.st, self.nb, self.rg, self.sb, self.dense = e, st, nb, rg, sb, dense
        self.n = e.shape[0]

    def rows(self, width, col, buffers=None, own=False):
        kw = {} if buffers is None else dict(pipeline_mode=pl.Buffered(buffers))
        if own:
            return pl.BlockSpec((self.rg, width), lambda s, *a: (s, col(a[-1][s] > 0, *a[:-3])), **kw)
        if self.dense:
            return pl.BlockSpec((self.rg, width), lambda s, *a: (a[-2][s], col(a[-1][s] > 0, *a[:-3])), **kw)
        unit = self.sb
        return pl.BlockSpec((pl.Element(self.rg), pl.Element(width)),
                            lambda s, *a: (a[-2][s] * unit, col(a[-1][s] > 0, *a[:-3]) * width), **kw)

    def weights(self, k_rows, width, kcol):
        return pl.BlockSpec((None, k_rows, width), lambda s, *a: (a[-3][s], *kcol(a[-1][s] > 0, *a[:-3])))


def _dense_groups(m, rg, lead=0):
    n = m // rg
    return _Groups(jnp.full((n,), lead, I32), jnp.arange(n, dtype=I32), jnp.ones((n,), I32), rg, rg, True)


def _live_rows(nb_ref, n_sub, sb, compute):
    nblk = nb_ref[pl.program_id(0)]
    if n_sub == 1:
        @pl.when(nblk > 0)
        def _():
            compute(pl.ds(0, sb))

        return nblk

    def run_pair(r, c):
        compute(pl.ds(pl.multiple_of(r * (2 * sb), 2 * sb), 2 * sb))
        return c

    lax.fori_loop(0, lax.shift_right_logical(nblk, 1), run_pair, 0)

    @pl.when(nblk & 1 == 1)
    def _():
        compute(pl.ds(pl.multiple_of((nblk - 1) * sb, sb), sb))

    return nblk


def _row_loops(nb_ref, n_sub, sb, compute, o_refs):
    nblk = _live_rows(nb_ref, n_sub, sb, compute)

    def clear(r, c):
        for o_ref in o_refs:
            o_ref[pl.ds(pl.multiple_of(r * sb, sb), sb), :] = jnp.zeros((sb, o_ref.shape[1]), o_ref.dtype)
        return c

    lax.fori_loop(nblk, n_sub, clear, 0)


def _bf16_dot(x, w_ref):
    return jnp.dot(x, w_ref[...].astype(BF16), preferred_element_type=F32)


def _rg_mm_body(e_ref, st_ref, nb_ref, x_ref, w_ref, o_ref, *, sb):
    def compute(rows):
        o_ref[rows, :] = _bf16_dot(x_ref[rows, :], w_ref).astype(o_ref.dtype)

    _row_loops(nb_ref, x_ref.shape[0] // sb, sb, compute, [o_ref])


def _rg_swiglu_body(e_ref, st_ref, nb_ref, x_ref, wg_ref, wu_ref, o_ref, *, sb):
    def compute(rows):
        x = x_ref[rows, :]
        g = _bf16_dot(x, wg_ref)
        u = _bf16_dot(x, wu_ref)
        o_ref[rows, :] = (jax.nn.silu(g) * u).astype(o_ref.dtype)

    _row_loops(nb_ref, x_ref.shape[0] // sb, sb, compute, [o_ref])


def _rg_ple_body(e_ref, st_ref, nb_ref, x_ref, w_ref, p_ref, pw_ref, r_ref, of_ref, ob_ref, *, sb):
    def compute(rows):
        gate = _bf16_dot(x_ref[rows, :], w_ref)
        proj = _bf16_dot(p_ref[rows, :], pw_ref)
        y = r_ref[rows, :] + jax.nn.sigmoid(gate) * proj
        of_ref[rows, :] = y
        ob_ref[rows, :] = y.astype(BF16)

    _row_loops(nb_ref, x_ref.shape[0] // sb, sb, compute, [of_ref, ob_ref])


def _rg_down_body(e_ref, st_ref, nb_ref, a_ref, w_ref, o_ref, *, sb):
    @pl.when(pl.program_id(2) == 0)
    def _():
        o_ref[...] = jnp.zeros_like(o_ref)

    def compute(rows):
        o_ref[rows, :] += _bf16_dot(a_ref[rows, :], w_ref)

    _live_rows(nb_ref, a_ref.shape[0] // sb, sb, compute)


def _last_if_dead(n_blocks):
    return lambda live, j: jnp.where(live, j, n_blocks - 1)


def _rg_matmul(g, x, w, *, tn, out_dtype=F32, stripes=None):
    K = x.shape[1]
    j0, J = (0, pl.cdiv(w.shape[-1], tn)) if stripes is None else stripes
    N = w.shape[-1] if stripes is None else J * tn
    col = _last_if_dead(J)
    return pl.pallas_call(
        functools.partial(_rg_mm_body, sb=g.sb),
        grid_spec=pltpu.PrefetchScalarGridSpec(
            num_scalar_prefetch=3, grid=(g.n, J),
            in_specs=[g.rows(K, lambda live, j: 0, buffers=1),
                      g.weights(K, tn, lambda live, j: (0, j0 + col(live, j)))],
            out_specs=g.rows(tn, lambda live, j: j, own=True)),
        out_shape=jax.ShapeDtypeStruct((g.n * g.rg, N), out_dtype),
        compiler_params=_params("arbitrary", "arbitrary"),
        name="rg_mm",
    )(g.e, g.st, g.nb, x, w)


def _rg_swiglu(g, x, wg, wu, *, tn):
    K = x.shape[1]
    N = wg.shape[-1]
    col = _last_if_dead(N // tn)
    wspec = g.weights(K, tn, lambda live, j: (0, col(live, j)))
    return pl.pallas_call(
        functools.partial(_rg_swiglu_body, sb=g.sb),
        grid_spec=pltpu.PrefetchScalarGridSpec(
            num_scalar_prefetch=3, grid=(g.n, N // tn),
            in_specs=[g.rows(K, lambda live, j: 0, buffers=1), wspec, wspec],
            out_specs=g.rows(tn, lambda live, j: j, own=True)),
        out_shape=jax.ShapeDtypeStruct((g.n * g.rg, N), BF16),
        compiler_params=_params("arbitrary", "arbitrary"),
        name="rg_swiglu",
    )(g.e, g.st, g.nb, x, wg, wu)


def _rg_ple(g, x, w, p, pw, resid, *, tn):
    K, KP = x.shape[1], p.shape[1]
    N = w.shape[-1]
    col = _last_if_dead(N // tn)
    wcol = lambda live, j: (0, col(live, j))
    tile = g.rows(tn, lambda live, j: j, own=True)
    return pl.pallas_call(
        functools.partial(_rg_ple_body, sb=g.sb),
        grid_spec=pltpu.PrefetchScalarGridSpec(
            num_scalar_prefetch=3, grid=(g.n, N // tn),
            in_specs=[g.rows(K, lambda live, j: 0, buffers=1), g.weights(K, tn, wcol),
                      g.rows(KP, lambda live, j: 0, buffers=1), g.weights(KP, tn, wcol), tile],
            out_specs=[tile, tile]),
        out_shape=[jax.ShapeDtypeStruct((g.n * g.rg, N), F32), jax.ShapeDtypeStruct((g.n * g.rg, N), BF16)],
        compiler_params=_params("arbitrary", "arbitrary"),
        name="rg_ple",
    )(g.e, g.st, g.nb, x, w, p, pw, resid)


def _rg_down(g, a, w, *, tn, tk):
    K = a.shape[1]
    N = w.shape[-1]
    NT, KC = N // tn, K // tk
    ncol = lambda live, n, kc: jnp.where(live, n, NT - 1)
    kcol = lambda live, n, kc: jnp.where(live, kc, KC - 1)
    return pl.pallas_call(
        functools.partial(_rg_down_body, sb=g.sb),
        grid_spec=pltpu.PrefetchScalarGridSpec(
            num_scalar_prefetch=3, grid=(g.n, NT, KC),
            in_specs=[g.rows(tk, kcol, own=True),
                      g.weights(tk, tn, lambda live, n, kc: (kcol(live, n, kc), ncol(live, n, kc)))],
            out_specs=g.rows(tn, lambda live, n, kc: n, buffers=1, own=True)),
        out_shape=jax.ShapeDtypeStruct((g.n * g.rg, N), F32),
        compiler_params=_params("arbitrary", "arbitrary", "arbitrary"),
        name="rg_down",
    )(g.e, g.st, g.nb, a, w)


def _layer_norm_rows(z, g, b):
    mu = jnp.mean(z, -1, keepdims=True)
    d = z - mu
    var = jnp.mean(d * d, -1, keepdims=True)
    return d * lax.rsqrt(var + LN_EPS) * g + b


def _ln_body(a_ref, b_ref, g_ref, beta_ref, of_ref, ob_ref):
    y = _layer_norm_rows(ALPHA * a_ref[...] + b_ref[...], g_ref[...], beta_ref[...])
    of_ref[...] = y
    ob_ref[...] = y.astype(BF16)


def _residual_layer_norm(a, b, g, beta, *, tr):
    M, D = a.shape
    row = pl.BlockSpec((tr, D), lambda i: (i, 0))
    vec = pl.BlockSpec((1, D), lambda i: (0, 0))
    return pl.pallas_call(
        _ln_body,
        grid=(M // tr,),
        in_specs=[row, row, vec, vec],
        out_specs=[row, row],
        out_shape=[jax.ShapeDtypeStruct((M, D), F32), jax.ShapeDtypeStruct((M, D), BF16)],
        compiler_params=_params("parallel"),
        name="res_ln",
    )(a, b, g.reshape(1, D), beta.reshape(1, D))


def _stack_body(a_ref, b_ref, of_ref, ob_ref, *, n_a):
    def emit(src_ref):
        y = src_ref[...]
        of_ref[...] = y
        ob_ref[...] = y.astype(BF16)

    pl.when(pl.program_id(0) < n_a)(functools.partial(emit, a_ref))
    pl.when(pl.program_id(0) >= n_a)(functools.partial(emit, b_ref))


def _stack_rows(a, b, *, tr):
    n_a, n_b = a.shape[0] // tr, b.shape[0] // tr
    D = a.shape[1]
    row = pl.BlockSpec((tr, D), lambda i: (i, 0))
    return pl.pallas_call(
        functools.partial(_stack_body, n_a=n_a),
        grid=(n_a + n_b,),
        in_specs=[pl.BlockSpec((tr, D), lambda i: (jnp.minimum(i, n_a - 1), 0)),
                  pl.BlockSpec((tr, D), lambda i: (jnp.maximum(i - n_a, 0), 0))],
        out_specs=[row, row],
        out_shape=[jax.ShapeDtypeStruct((a.shape[0] + b.shape[0], D), F32),
                   jax.ShapeDtypeStruct((a.shape[0] + b.shape[0], D), BF16)],
        compiler_params=_params("arbitrary"),
        name="stack_rows",
    )(a, b)


def _rotate(x, c, s_up, s_down, half):
    w = x.shape[1]
    reps = w // LANES
    if reps > 1:
        c, s_up, s_down = (jnp.concatenate([t] * reps, axis=1) for t in (c, s_up, s_down))
    return x * c + pltpu.roll(x, w - half, 1) * s_up + pltpu.roll(x, half, 1) * s_down


def _rot_body(x_ref, c_ref, su_ref, sd_ref, *o_refs, half):
    y = _rotate(x_ref[...], c_ref[...], su_ref[...], sd_ref[...], half)
    for o_ref in o_refs:
        o_ref[...] = y.astype(o_ref.dtype)


def _rotary_cols(h, col_block, width, tables, half, out_dtypes, *, tr):
    M = h.shape[0]
    tab = pl.BlockSpec((tr, LANES), lambda i: (i, 0))
    return pl.pallas_call(
        functools.partial(_rot_body, half=half),
        grid=(M // tr,),
        in_specs=[pl.BlockSpec((tr, width), lambda i: (i, col_block)), tab, tab, tab],
        out_specs=[pl.BlockSpec((tr, width), lambda i: (i, 0)) for _ in out_dtypes],
        out_shape=[jax.ShapeDtypeStruct((M, width), dt) for dt in out_dtypes],
        compiler_params=_params("parallel"),
        name="rotary",
    )(h, *tables)


def _ik_body(x_ref, g_ref, b_ref, c_ref, su_ref, sd_ref, ik_ref, ik2_ref, iw_ref):
    x = x_ref[...]
    is_key = lax.broadcasted_iota(I32, x.shape, 1) < D_IDX
    mu = jnp.sum(jnp.where(is_key, x, 0.0), -1, keepdims=True) / D_IDX
    d = jnp.where(is_key, x - mu, 0.0)
    var = jnp.sum(d * d, -1, keepdims=True) / D_IDX
    y = jnp.where(is_key, d * lax.rsqrt(var + LN_EPS) * g_ref[...] + b_ref[...], 0.0)
    y = _rotate(y, c_ref[...], su_ref[...], sd_ref[...], IDX_ROPE_DIM // 2)
    y = jnp.where(is_key, y, 0.0)
    ik_ref[...] = y[:, :D_IDX]
    ik2_ref[...] = (y + pltpu.roll(y, D_IDX, 1)).astype(BF16)
    iw_ref[...] = pltpu.roll(x, LANES - D_IDX, 1)[:, :H_IDX]


def _indexer_keys(h, col_block, g, b, tables, *, tr):
    M = h.shape[0]
    pad = LANES - D_IDX
    tab = pl.BlockSpec((tr, LANES), lambda i: (i, 0))
    vec = pl.BlockSpec((1, LANES), lambda i: (0, 0))
    return pl.pallas_call(
        _ik_body,
        grid=(M // tr,),
        in_specs=[pl.BlockSpec((tr, LANES), lambda i: (i, col_block)), vec, vec, tab, tab, tab],
        out_specs=[pl.BlockSpec((tr, D_IDX), lambda i: (i, 0)), pl.BlockSpec((tr, LANES), lambda i: (i, 0)),
                   pl.BlockSpec((tr, H_IDX), lambda i: (i, 0))],
        out_shape=[jax.ShapeDtypeStruct((M, D_IDX), F32), jax.ShapeDtypeStruct((M, LANES), BF16),
                   jax.ShapeDtypeStruct((M, H_IDX), F32)],
        compiler_params=_params("parallel"),
        name="indexer_keys",
    )(h, jnp.pad(g, (0, pad)).reshape(1, LANES), jnp.pad(b, (0, pad)).reshape(1, LANES), *tables)


def _router_body(l_ref, e_ref, g_ref):
    lg = l_ref[...]
    idx = lax.broadcasted_iota(I32, lg.shape, 1)
    m1 = jnp.max(lg, -1, keepdims=True)
    e1 = jnp.min(jnp.where(lg == m1, idx, N_EXPERTS), -1, keepdims=True)
    rest = jnp.where(idx == e1, -jnp.inf, lg)
    m2 = jnp.max(rest, -1, keepdims=True)
    e2 = jnp.min(jnp.where(rest == m2, idx, N_EXPERTS), -1, keepdims=True)
    ex2 = jnp.exp(m2 - m1)
    den = 1.0 + ex2
    e_ref[...] = jnp.concatenate([e1, e2], axis=1)
    g_ref[...] = jnp.concatenate([1.0 / den, ex2 / den], axis=1)


def _router_top2(logits, *, tr):
    M, E = logits.shape
    return pl.pallas_call(
        _router_body,
        grid=(M // tr,),
        in_specs=[pl.BlockSpec((tr, E), lambda i: (i, 0))],
        out_specs=[pl.BlockSpec((tr, TOP_K), lambda i: (i, 0)), pl.BlockSpec((tr, TOP_K), lambda i: (i, 0))],
        out_shape=[jax.ShapeDtypeStruct((M, TOP_K), I32), jax.ShapeDtypeStruct((M, TOP_K), F32)],
        compiler_params=_params("parallel"),
        name="router_top2",
    )(logits)


def _row_copy(src_hbm, row, dst, r, sem):
    return pltpu.make_async_copy(src_hbm.at[pl.ds(row, 1), :], dst.at[pl.ds(r, 1), :], sem)


def _gather_body(tok_ref, x_hbm, o_ref, buf, sem, *, tb):
    base = pl.program_id(0) * tb

    def start(r2, _):
        for u in range(DMA_PRIORITIES):
            r = r2 * DMA_PRIORITIES + u
            _row_copy(x_hbm, tok_ref[base + r], buf, r, sem).start(priority=u)
        return 0

    def wait(r, _):
        _row_copy(x_hbm, 0, buf, r, sem).wait()
        return 0

    lax.fori_loop(0, tb // DMA_PRIORITIES, start, 0)
    lax.fori_loop(0, tb, wait, 0)
    o_ref[...] = buf[...].astype(BF16)


def _gather_rows(slot_tok, x, *, tb):
    A = slot_tok.shape[0]
    D = x.shape[1]
    return pl.pallas_call(
        functools.partial(_gather_body, tb=tb),
        grid_spec=pltpu.PrefetchScalarGridSpec(
            num_scalar_prefetch=1,
            grid=(A // tb,),
            in_specs=[pl.BlockSpec(memory_space=pl.ANY)],
            out_specs=pl.BlockSpec((tb, D), lambda i, tok: (i, 0)),
            scratch_shapes=[pltpu.VMEM((tb, D), F32), pltpu.SemaphoreType.DMA(())]),
        out_shape=jax.ShapeDtypeStruct((A, D), BF16),
        compiler_params=_params("arbitrary"),
        name="moe_gather",
    )(slot_tok, x)


def _combine_body(slot_ref, y_hbm, gate_ref, x_ref, g_ref, beta_ref, of_ref, ob_ref, buf, sem, *, tb):
    base = pl.program_id(0) * tb

    def start(r, _):
        for k in range(TOP_K):
            _row_copy(y_hbm, slot_ref[(base + r) * TOP_K + k], buf.at[k], r, sem).start(priority=k % DMA_PRIORITIES)
        return 0

    def wait(r, _):
        for k in range(TOP_K):
            _row_copy(y_hbm, 0, buf.at[k], r, sem).wait()
        return 0

    lax.fori_loop(0, tb, start, 0)
    lax.fori_loop(0, tb, wait, 0)
    gates = gate_ref[...]
    y = buf[0] * gates[:, 0:1] + buf[1] * gates[:, 1:2]
    z = _layer_norm_rows(ALPHA * x_ref[...] + y, g_ref[...], beta_ref[...])
    of_ref[...] = z
    ob_ref[...] = z.astype(BF16)


def _moe_combine_ln(tok_slot, yb, gates, x, g, beta, *, tb):
    M, D = x.shape
    row = pl.BlockSpec((tb, D), lambda i, s: (i, 0))
    vec = pl.BlockSpec((1, D), lambda i, s: (0, 0))
    return pl.pallas_call(
        functools.partial(_combine_body, tb=tb),
        grid_spec=pltpu.PrefetchScalarGridSpec(
            num_scalar_prefetch=1,
            grid=(M // tb,),
            in_specs=[pl.BlockSpec(memory_space=pl.ANY), pl.BlockSpec((tb, TOP_K), lambda i, s: (i, 0)),
                      row, vec, vec],
            out_specs=[row, row],
            scratch_shapes=[pltpu.VMEM((TOP_K, tb, D), F32), pltpu.SemaphoreType.DMA(())]),
        out_shape=[jax.ShapeDtypeStruct((M, D), F32), jax.ShapeDtypeStruct((M, D), BF16)],
        compiler_params=_params("arbitrary"),
        name="moe_combine",
    )(tok_slot.reshape(-1), yb, gates, x, g.reshape(1, D), beta.reshape(1, D))


def _ret_body(q_ref, k_ref, v_ref, gate_ref, c_ref, su_ref, sd_ref, lg_ref, gn_ref, s0_ref, _, o_ref, st_ref,
              s_scr, *, C):
    @pl.when(pl.program_id(1) == 0)
    def _():
        s_scr[...] = s0_ref[...]

    c, su, sd = c_ref[...], su_ref[...], sd_ref[...]
    q = _rotate(q_ref[...], c, su, sd, HEAD_DIM // 2)
    k = _rotate(k_ref[...], c, su, sd, HEAD_DIM // 2) * HEAD_DIM ** -0.5
    vb = v_ref[...].astype(BF16)
    lg_row = lg_ref[0:1, :C]
    lg_lane = lg_ref[0:1, :LANES]
    n_col = lax.broadcasted_iota(I32, (C, C), 0)
    m_row = lax.broadcasted_iota(I32, (C, C), 1)
    diff = (n_col - m_row).astype(F32)
    intra = jnp.where(diff >= 0, jnp.exp(lg_row * jnp.maximum(diff, 0.0)), 0.0)
    n_idx = lax.broadcasted_iota(I32, (C, LANES), 0).astype(F32)
    q_dec = jnp.exp(lg_lane * (n_idx + 1.0))
    k_dec = jnp.exp(lg_lane * (C - 1.0 - n_idx))
    c_dec = jnp.exp(lg_lane * C)

    qb = q.astype(BF16)
    s = lax.dot_general(qb, k.astype(BF16), (((1,), (1,)), ((), ())), preferred_element_type=F32) * intra
    state = s_scr[...]
    o = (jnp.dot(s.astype(BF16), vb, preferred_element_type=F32)
         + jnp.dot(qb, state.astype(BF16), preferred_element_type=F32) * q_dec)
    kd = (k * k_dec).astype(BF16)
    state = state * c_dec + lax.dot_general(kd, vb, (((0,), (0,)), ((), ())), preferred_element_type=F32)
    s_scr[...] = state
    st_ref[...] = state

    mu = jnp.mean(o, -1, keepdims=True)
    d = o - mu
    var = jnp.mean(d * d, -1, keepdims=True)
    on = d * lax.rsqrt(var + LN_EPS) * gn_ref[...]
    o_ref[...] = (on * jax.nn.silu(gate_ref[...])).astype(o_ref.dtype)


def _retention(h, tables, lg_tab, gn_g, state0, out_buf, *, C, n_seq, n_chunks, row0):
    rb0 = row0 // C

    def rows(sh, c):
        return rb0 + (sh // H_RET) * n_chunks + c

    def hcol(off):
        return pl.BlockSpec((C, HEAD_DIM), lambda sh, c, off=off: (rows(sh, c), off + sh % H_RET))

    tab = pl.BlockSpec((C, LANES), lambda sh, c: (rows(sh, c), 0))
    in_specs = [hcol(0), hcol(H_RET), hcol(2 * H_RET), hcol(3 * H_RET), tab, tab, tab,
                pl.BlockSpec((None, 8, lg_tab.shape[2]), lambda sh, c: (sh % H_RET, 0, 0)),
                pl.BlockSpec((1, HEAD_DIM), lambda sh, c: (0, sh % H_RET)),
                pl.BlockSpec((None, HEAD_DIM, HEAD_DIM), lambda sh, c: (sh, 0, 0)),
                pl.BlockSpec(memory_space=pl.ANY)]
    state_shape = (n_seq * H_RET, HEAD_DIM, HEAD_DIM)
    return pl.pallas_call(
        functools.partial(_ret_body, C=C),
        grid=(n_seq * H_RET, n_chunks),
        in_specs=in_specs,
        out_specs=[pl.BlockSpec((C, HEAD_DIM), lambda sh, c: (rows(sh, c), sh % H_RET)),
                   pl.BlockSpec((None, HEAD_DIM, HEAD_DIM), lambda sh, c: (sh, 0, 0))],
        out_shape=[jax.ShapeDtypeStruct(out_buf.shape, out_buf.dtype), jax.ShapeDtypeStruct(state_shape, F32)],
        scratch_shapes=[pltpu.VMEM((HEAD_DIM, HEAD_DIM), F32)],
        input_output_aliases={len(in_specs) - 1: 0},
        compiler_params=_params("arbitrary", "arbitrary"),
        name="retention",
    )(h, h, h, h, *tables, lg_tab, gn_g.reshape(1, -1), state0.reshape(state_shape), out_buf)


def _dsa_body(q_ref, iq_ref, iwt_ref, k_ref, v_ref, ik2_ref, *rest, TQ, TK, L, n_sel, q_pos0, q_stride, causal):
    o_ref, key_scr, iqm_scr, q4_scr, m_scr, l_scr, acc_scr = rest[-7:]
    G = DSA_GROUP
    i = pl.program_id(0)
    q_pos = q_pos0 + i * q_stride + lax.broadcasted_iota(I32, (1, TQ), 1)
    q_lim = ((q_pos >> 6) + 1) << 6
    if causal:
        n_tiles = jnp.minimum(((i + 1) * q_stride + TK - 1) // TK, L // TK)
    else:
        n_tiles = L // TK

    lane = lax.broadcasted_iota(I32, (TQ, LANES), 1)
    for j in range(H_IDX // 2):
        pair = iq_ref[:, j * LANES:(j + 1) * LANES]
        iqm_scr[j, :TQ, :] = jnp.where(lane < D_IDX, pair, jnp.zeros_like(pair))
        iqm_scr[j, TQ:, :] = jnp.where(lane >= D_IDX, pair, jnp.zeros_like(pair))
    iwt = iwt_ref[...] * (H_IDX ** -0.5 * D_IDX ** -0.5)

    def key_pos(kt):
        return kt * TK + lax.broadcasted_iota(I32, (TK, TQ), 0)

    def score_tile(kt, _):
        ik2 = ik2_ref[pl.ds(kt * TK, TK), :]
        acc = jnp.zeros((TK, TQ), F32)
        for j in range(H_IDX // 2):
            a = lax.dot_general(ik2, iqm_scr[j], (((1,), (1,)), ((), ())), preferred_element_type=F32)
            acc = (acc + jnp.maximum(a[:, :TQ], 0.0) * iwt[2 * j:2 * j + 1, :]
                   + jnp.maximum(a[:, TQ:], 0.0) * iwt[2 * j + 1:2 * j + 2, :])
        acc = jnp.where(key_pos(kt) < q_lim, acc, -jnp.inf)
        bits = pltpu.bitcast(acc, I32)
        key_scr[pl.ds(kt * TK, TK), :] = bits ^ ((bits >> 31) & INT_MAX)
        return 0

    lax.fori_loop(0, n_tiles, score_tile, 0)

    def bisect(_, carry):
        lo, hi = carry
        mid = lo + lax.shift_right_logical(hi - lo, 1)

        def count_tile(kt, c8):
            ge = (key_scr[pl.ds(kt * TK, TK), :] >= mid).astype(I32)
            return c8 + jnp.sum(ge.reshape(TK // 8, 8, TQ), axis=0)

        c8 = lax.fori_loop(0, n_tiles, count_tile, jnp.zeros((8, TQ), I32))
        enough = jnp.sum(c8, axis=0, keepdims=True) >= n_sel
        return jnp.where(enough, mid, lo), jnp.where(enough, hi, mid)

    thr, _ = lax.fori_loop(0, 32, bisect, (jnp.full((1, TQ), INT_MIN, I32), jnp.full((1, TQ), INT_MAX, I32)))

    for kv in range(KVH_DSA):
        q4_scr[kv] = jnp.concatenate(
            [q_ref[:, (kv * G + g) * HEAD_DIM:(kv * G + g + 1) * HEAD_DIM] for g in range(G)], axis=0)
    m_scr[...] = jnp.full_like(m_scr, NEG_BIG)
    l_scr[...] = jnp.zeros_like(l_scr)
    acc_scr[...] = jnp.zeros_like(acc_scr)

    def attend_tile(kt, _):
        sel = jnp.logical_and(key_scr[pl.ds(kt * TK, TK), :] >= thr, key_pos(kt) < q_lim)
        sel4 = jnp.concatenate([sel] * G, axis=1)
        for kv in range(KVH_DSA):
            cols = slice(kv * HEAD_DIM, (kv + 1) * HEAD_DIM)
            kt_ = k_ref[pl.ds(kt * TK, TK), cols]
            vt_ = v_ref[pl.ds(kt * TK, TK), cols]
            lg = lax.dot_general(kt_, q4_scr[kv], (((1,), (1,)), ((), ())), preferred_element_type=F32)
            lg = jnp.where(sel4, lg, NEG_BIG)
            m_old = m_scr[kv]
            m_new = jnp.maximum(m_old, jnp.max(lg, axis=0, keepdims=True))
            alpha = jnp.exp2((m_old - m_new) * EXP2_SCALE)
            p = jnp.exp2((lg - m_new) * EXP2_SCALE)
            l_scr[kv] = alpha * l_scr[kv] + jnp.sum(p, axis=0, keepdims=True)
            pv = lax.dot_general(vt_, p.astype(BF16), (((0,), (0,)), ((), ())), preferred_element_type=F32)
            acc_scr[kv] = acc_scr[kv] * alpha + pv
            m_scr[kv] = m_new
        return 0

    lax.fori_loop(0, n_tiles, attend_tile, 0)

    for kv in range(KVH_DSA):
        o_t = acc_scr[kv] / l_scr[kv]
        for g in range(G):
            hd = kv * G + g
            o_ref[:, hd * HEAD_DIM:(hd + 1) * HEAD_DIM] = o_t[:, g * TQ:(g + 1) * TQ].T.astype(o_ref.dtype)


def _dsa(q, iq, iwt, k, v, ik2, out_buf, *, TQ, TK, n_blocks, n_sel, q_pos0, q_stride, causal,
         q_row0, out_row0, out_col0, per_block_keys):
    L = k.shape[-2]
    W = H_DSA * HEAD_DIM
    qb0, ob0, oc0 = q_row0 // TQ, out_row0 // TQ, out_col0 // W
    if per_block_keys:
        kspec = lambda a: pl.BlockSpec((None, L, a.shape[-1]), lambda i: (i, 0, 0))
    else:
        kspec = lambda a: pl.BlockSpec((L, a.shape[-1]), lambda i: (0, 0), pipeline_mode=pl.Buffered(1))
    G = DSA_GROUP
    return pl.pallas_call(
        functools.partial(_dsa_body, TQ=TQ, TK=TK, L=L, n_sel=n_sel, q_pos0=q_pos0, q_stride=q_stride,
                          causal=causal),
        grid=(n_blocks,),
        in_specs=[pl.BlockSpec((TQ, W), lambda i: (qb0 + i, 0)),
                  pl.BlockSpec((TQ, H_IDX * D_IDX), lambda i: (qb0 + i, 0)),
                  pl.BlockSpec((H_IDX, TQ), lambda i: (0, qb0 + i)),
                  kspec(k), kspec(v), kspec(ik2),
                  pl.BlockSpec(memory_space=pl.ANY)],
        out_specs=pl.BlockSpec((TQ, W), lambda i: (ob0 + i, oc0)),
        out_shape=jax.ShapeDtypeStruct(out_buf.shape, out_buf.dtype),
        scratch_shapes=[pltpu.VMEM((L, TQ), I32), pltpu.VMEM((H_IDX // 2, 2 * TQ, LANES), BF16),
                        pltpu.VMEM((KVH_DSA, G * TQ, HEAD_DIM), BF16),
                        pltpu.VMEM((KVH_DSA, 1, G * TQ), F32), pltpu.VMEM((KVH_DSA, 1, G * TQ), F32),
                        pltpu.VMEM((KVH_DSA, HEAD_DIM, G * TQ), F32)],
        input_output_aliases={6: 0},
        compiler_params=_params("arbitrary"),
        name="dsa",
    )(q, iq, iwt, k, v, ik2, out_buf)


def _bias_body(rb_ref, idx_ref, o_ref):
    n = rb_ref.shape[1]
    onehot = (lax.broadcasted_iota(I32, (n, idx_ref.shape[1]), 0) == idx_ref[...]).astype(F32)
    o_ref[...] = jnp.dot(rb_ref[...], onehot, preferred_element_type=F32, precision=lax.Precision.HIGHEST)


def _expand_bias(rel_bias, n_q, n_k, *, tn):
    H, R = rel_bias.shape
    RP = 3 * LANES
    t = np.arange(n_q)[:, None]
    s = np.arange(n_k)[None, :]
    idx = (np.clip(LEFT_CHUNKS * CHUNK + t - s, -REL_CLIP, REL_CLIP) + REL_CLIP).reshape(1, -1).astype(np.int32)
    out = pl.pallas_call(
        _bias_body,
        grid=(idx.shape[1] // tn,),
        in_specs=[pl.BlockSpec((H, RP), lambda j: (0, 0)), pl.BlockSpec((1, tn), lambda j: (0, j))],
        out_specs=pl.BlockSpec((H, tn), lambda j: (0, j)),
        out_shape=jax.ShapeDtypeStruct((H, idx.shape[1]), F32),
        compiler_params=_params("parallel"),
        name="rel_bias_expand",
    )(jnp.pad(rel_bias, ((0, 0), (0, RP - R))), jnp.asarray(idx))
    return out.reshape(H, n_q, n_k)


def _band_heads(q, k_parts, v_parts, bias_ref, ok, o_ref, HB):
    for hh in range(HB):
        cols = slice(hh * HEAD_DIM, (hh + 1) * HEAD_DIM)
        qh = q[:, cols].astype(BF16)
        kw = jnp.concatenate([p[:, cols].astype(BF16) for p in k_parts], axis=0)
        vw = jnp.concatenate([p[:, cols].astype(BF16) for p in v_parts], axis=0)
        s = lax.dot_general(qh, kw, (((1,), (1,)), ((), ())), preferred_element_type=F32)
        s = s * HEAD_DIM ** -0.5 + bias_ref[hh]
        if ok is not None:
            s = jnp.where(ok, s, NEG_BIG)
        p = jnp.exp(s - jnp.max(s, -1, keepdims=True))
        pr = (p / jnp.sum(p, -1, keepdims=True)).astype(BF16)
        o_ref[:, cols] = jnp.dot(pr, vw, preferred_element_type=F32).astype(o_ref.dtype)


def _band_prompt_body(q_ref, *rest, HB, TQ, NP):
    k_refs, v_refs = rest[:NP], rest[NP:2 * NP]
    bias_ref, o_ref = rest[2 * NP], rest[2 * NP + 2]
    j = pl.program_id(1)
    nk = NP * TQ
    q_pos = j * TQ + lax.broadcasted_iota(I32, (TQ, nk), 0)
    k_pos = (j - (NP - 1)) * TQ + lax.broadcasted_iota(I32, (TQ, nk), 1)
    qc, kc = q_pos >> 6, k_pos >> 6
    ok = jnp.logical_and(jnp.logical_and(k_pos >= 0, kc <= qc), kc >= qc - LEFT_CHUNKS)
    _band_heads(q_ref[...], [r[...] for r in k_refs], [r[...] for r in v_refs], bias_ref, ok, o_ref, HB)


def _band_prompt(qkv, bias, out_buf, *, n_blocks, HB, TQ, NP):
    W = HB * HEAD_DIM
    n_hg = H_CHK // HB

    def kv_spec(r, third):
        return pl.BlockSpec((TQ, W), lambda hg, j: (jnp.maximum(j - (NP - 1) + r, 0), third * n_hg + hg))

    return pl.pallas_call(
        functools.partial(_band_prompt_body, HB=HB, TQ=TQ, NP=NP),
        grid=(n_hg, n_blocks),
        in_specs=([pl.BlockSpec((TQ, W), lambda hg, j: (j, hg))]
                  + [kv_spec(r, 1) for r in range(NP)] + [kv_spec(r, 2) for r in range(NP)]
                  + [pl.BlockSpec((HB, TQ, NP * TQ), lambda hg, j: (hg, 0, 0)), pl.BlockSpec(memory_space=pl.ANY)]),
        out_specs=pl.BlockSpec((TQ, W), lambda hg, j: (j, hg)),
        out_shape=jax.ShapeDtypeStruct(out_buf.shape, out_buf.dtype),
        input_output_aliases={2 * NP + 2: 0},
        compiler_params=_params("parallel", "parallel"),
        name="band_prompt",
    )(qkv, *([qkv] * (2 * NP)), bias, out_buf)


def _band_sample_body(q_ref, kc_ref, kn_ref, vc_ref, vn_ref, bias_ref, _, o_ref, *, HB):
    _band_heads(q_ref[...], [kc_ref[...], kn_ref[...]], [vc_ref[...], vn_ref[...]], bias_ref, None, o_ref, HB)


def _band_sample(qkv, cache_k, cache_v, bias, out_buf, *, n_seq, T, row0, HB):
    P = cache_k.shape[1]
    W = HB * HEAD_DIM
    n_hg = H_CHK // HB
    rb0 = row0 // T
    new = lambda third: pl.BlockSpec((T, W), lambda b, hg: (rb0 + b, third * n_hg + hg))
    cache = pl.BlockSpec((None, P, W), lambda b, hg: (b, 0, hg))
    return pl.pallas_call(
        functools.partial(_band_sample_body, HB=HB),
        grid=(n_seq, n_hg),
        in_specs=[new(0), cache, new(1), cache, new(2),
                  pl.BlockSpec((HB, T, P + T), lambda b, hg: (hg, 0, 0)), pl.BlockSpec(memory_space=pl.ANY)],
        out_specs=pl.BlockSpec((T, W), lambda b, hg: (rb0 + b, hg)),
        out_shape=jax.ShapeDtypeStruct(out_buf.shape, out_buf.dtype),
        input_output_aliases={6: 0},
        compiler_params=_params("parallel", "parallel"),
        name="band_sample",
    )(qkv, cache_k, qkv, cache_v, qkv, bias, out_buf)


def _rot_tables(pos, head_dim, rot_dim, theta):
    half = rot_dim // 2
    inv = theta ** (-jnp.arange(half, dtype=F32) / half)
    ang = pos.astype(F32)[:, None] * inv[None, :]
    cos, sin = jnp.cos(ang), jnp.sin(ang)
    m = pos.shape[0]
    zh = jnp.zeros((m, half), F32)
    rest0 = jnp.zeros((m, head_dim - rot_dim), F32)
    c = jnp.concatenate([cos, cos, jnp.ones((m, head_dim - rot_dim), F32)], 1)
    s_up = jnp.concatenate([-sin, zh, rest0], 1)
    s_down = jnp.concatenate([zh, sin, rest0], 1)
    reps = LANES // head_dim
    return tuple(jnp.tile(t, (1, reps)) for t in (c, s_up, s_down))


def _log_gamma_table(width):
    lg = jnp.log1p(-(2.0 ** (-5.0 - jnp.arange(H_RET, dtype=F32))))
    return jnp.broadcast_to(lg[:, None, None], (H_RET, 8, width))


def _routing(top_e, sb, rg):
    n = top_e.shape[0]
    a = n * TOP_K
    n_rows = (-(-a // sb) + N_EXPERTS) * sb + rg
    n_groups = N_EXPERTS + a // rg
    flat_e = top_e.reshape(-1)
    onehot = (flat_e[:, None] == jnp.arange(N_EXPERTS, dtype=I32)[None, :]).astype(I32)
    rank = jnp.take_along_axis(jnp.cumsum(onehot, 0), flat_e[:, None], 1)[:, 0] - 1
    counts = jnp.sum(onehot, 0)
    padded = (counts + sb - 1) // sb * sb
    pad_end = jnp.cumsum(padded)
    pad_start = pad_end - padded
    slot = pad_start[flat_e] + rank
    slot_tok = jnp.zeros((n_rows,), I32).at[slot].set(jnp.arange(a, dtype=I32) // TOP_K)

    g_count = (padded + rg - 1) // rg
    g_end = jnp.cumsum(g_count)
    g_start = g_end - g_count
    s = jnp.arange(n_groups, dtype=I32)
    live = s < g_end[-1]
    expert = jnp.minimum(jnp.searchsorted(g_end, jnp.where(live, s, g_end[-1] - 1), side='right'), N_EXPERTS - 1)
    k = s - g_start[expert]
    nb = jnp.where(live, jnp.clip(padded[expert] - k * rg, 0, rg) // sb, 0)
    st = jnp.where(live, pad_start[expert] + k * rg, pad_end[-1]) // sb
    groups = _Groups(expert.astype(I32), st.astype(I32), nb.astype(I32), rg, sb, False)
    own_row = (g_start[flat_e] + rank // rg) * rg + rank % rg
    return own_row.reshape(n, TOP_K).astype(I32), slot_tok, groups


RG = 2080
TN = 512
TN_PAIR = 256
TN_DOWN = 2048
TK_DOWN = 1024
TR = 208
RG_MOE = 2560
SB_MOE = 256
RET_CHUNK = 256
DSA_TQ = 128
DSA_TQ_S = 128
DSA_TK = 512
BAND_HB = 4
BAND_TQ = 128
BAND_PARTS = (LEFT_CHUNKS * CHUNK) // BAND_TQ + 1


def kernel(x_prompt, x_sample, p_prompt, p_sample, state_ret, cache_dsa_k, cache_dsa_v, cache_dsa_kidx,
           cache_chk_k, cache_chk_v, ln_mix_g, ln_mix_b, ln_ffn_g, ln_ffn_b, ple_proj, ple_gate,
           w_in_even, ret_gn_g, kidx_ln_g, kidx_ln_b, w_out_even, ffn_w_gate, ffn_w_up, ffn_w_down,
           w_in_odd, rel_bias, w_out_odd, router_w, exp_w_gate, exp_w_up, exp_w_down):
    seq, d = x_prompt.shape[1], x_prompt.shape[2]
    nb_s, t_s = x_sample.shape[0], x_sample.shape[1]
    n_s = nb_s * t_s
    m = seq + n_s
    past = cache_dsa_k.shape[2]
    ret_w = H_RET * HEAD_DIM
    dsa_w = H_DSA * HEAD_DIM
    kv_w = KVH_DSA * HEAD_DIM

    x, xb = _stack_rows(x_prompt[0], x_sample.reshape(n_s, d), tr=n_s)
    p = jnp.concatenate([p_prompt[:, 0], p_sample.reshape(DEPTH, n_s, -1)], 1).astype(BF16)
    pos = jnp.concatenate([jnp.arange(seq, dtype=I32), past + jnp.tile(jnp.arange(t_s, dtype=I32), nb_s)])
    tab_ret = _rot_tables(pos, HEAD_DIM, HEAD_DIM, RET_THETA)
    tab_dsa = _rot_tables(pos, HEAD_DIM, ROPE_DIM, ROPE_THETA)
    tab_idx = _rot_tables(pos, D_IDX, IDX_ROPE_DIM, ROPE_THETA)

    tokens = [_dense_groups(m, RG, lead=i) for i in range(DEPTH)]
    h = _rg_matmul(tokens[0], xb, w_in_even, tn=TN)
    c_dq = 4 * ret_w
    c_dk = c_dq + dsa_w
    c_dv = c_dk + kv_w
    c_iq = c_dv + kv_w
    c_ik = c_iq + H_IDX * D_IDX

    mix_in = jnp.zeros((m, ret_w + dsa_w), BF16)
    state0_p = jnp.zeros((1, H_RET, HEAD_DIM, HEAD_DIM), F32)
    mix_in, st_p = _retention(h, tab_ret, _log_gamma_table(RET_CHUNK), ret_gn_g[0], state0_p, mix_in,
                              C=RET_CHUNK, n_seq=1, n_chunks=seq // RET_CHUNK, row0=0)
    mix_in, st_s = _retention(h, tab_ret, _log_gamma_table(LANES), ret_gn_g[0], state_ret[0], mix_in,
                              C=t_s, n_seq=nb_s, n_chunks=1, row0=seq)

    (dq,) = _rotary_cols(h, c_dq // dsa_w, dsa_w, tab_dsa, ROPE_DIM // 2, [BF16], tr=TR)
    dk, dk_b = _rotary_cols(h, c_dk // kv_w, kv_w, tab_dsa, ROPE_DIM // 2, [F32, BF16], tr=TR)
    iq_lo, = _rotary_cols(h, c_iq // 1024, 1024, tab_idx, IDX_ROPE_DIM // 2, [BF16], tr=TR)
    iq_hi, = _rotary_cols(h, c_iq // 1024 + 1, 1024, tab_idx, IDX_ROPE_DIM // 2, [BF16], tr=TR)
    iq = jnp.concatenate([iq_lo, iq_hi], 1)
    ik, ik2, iw = _indexer_keys(h, c_ik // LANES, kidx_ln_g[0], kidx_ln_b[0], tab_idx, tr=TR)
    dv = h[:, c_dv:c_dv + kv_w]
    dv_b = dv.astype(BF16)
    iwt = iw.T

    n_sel_p = min(TOPK_MAX, seq // 4)
    mix_in = _dsa(dq, iq, iwt, dk_b[:seq], dv_b[:seq], ik2[:seq], mix_in, TQ=DSA_TQ, TK=DSA_TK,
                  n_blocks=seq // DSA_TQ, n_sel=n_sel_p, q_pos0=0, q_stride=DSA_TQ, causal=True,
                  q_row0=0, out_row0=0, out_col0=ret_w, per_block_keys=False)

    def pad_queries(a):
        a = a[seq:].reshape(nb_s, t_s, -1)
        return jnp.pad(a, ((0, 0), (0, DSA_TQ_S - t_s), (0, 0))).reshape(nb_s * DSA_TQ_S, -1)

    def with_cache(cache, new):
        return jnp.concatenate([cache.reshape(nb_s, past, -1).astype(BF16), new[seq:].reshape(nb_s, t_s, -1)], 1)

    kidx_c = cache_dsa_kidx[0].astype(BF16)
    l_s = past + t_s
    do_s = _dsa(pad_queries(dq), pad_queries(iq), pad_queries(iw).T,
                with_cache(cache_dsa_k[0], dk_b), with_cache(cache_dsa_v[0], dv_b),
                with_cache(jnp.concatenate([kidx_c, kidx_c], -1), ik2),
                jnp.zeros((nb_s * DSA_TQ_S, dsa_w), BF16), TQ=DSA_TQ_S, TK=l_s, n_blocks=nb_s,
                n_sel=min(TOPK_MAX, l_s // 4), q_pos0=past, q_stride=0, causal=False,
                q_row0=0, out_row0=0, out_col0=0, per_block_keys=True)
    do_s = do_s.reshape(nb_s, DSA_TQ_S, dsa_w)[:, :t_s].reshape(n_s, dsa_w)
    mix_in = lax.dynamic_update_slice(mix_in, do_s, (seq, ret_w))

    mix = _rg_matmul(tokens[0], mix_in, w_out_even, tn=TN)
    x1, x1b = _residual_layer_norm(x, mix, ln_mix_g[0], ln_mix_b[0], tr=TR)
    act = _rg_swiglu(tokens[0], x1b, ffn_w_gate, ffn_w_up, tn=TN_PAIR)
    ffn = _rg_down(tokens[0], act, ffn_w_down, tn=TN_DOWN, tk=TK_DOWN)
    x2, x2b = _residual_layer_norm(x1, ffn, ln_ffn_g[0], ln_ffn_b[0], tr=TR)
    x3, x3b = _rg_ple(tokens[0], x2b, ple_gate, p[0], ple_proj, x2, tn=TN_PAIR)

    qkv = _rg_matmul(tokens[0], x3b, w_in_odd, tn=TN, out_dtype=BF16)
    chk_w = H_CHK * HEAD_DIM
    keep = min(LEFT_CHUNKS * CHUNK, seq)
    kv_new = _rg_matmul(_dense_groups(keep + n_s, keep + n_s), x3b[seq - keep:], w_in_odd, tn=TN,
                        stripes=(chk_w // TN, 2 * chk_w // TN))
    p_band = cache_chk_k.shape[2]
    bias_p = _expand_bias(rel_bias[0], BAND_TQ, BAND_PARTS * BAND_TQ, tn=2048)
    bias_s = _expand_bias(rel_bias[0], t_s, p_band + t_s, tn=(t_s * (p_band + t_s)) // 2)
    att = _band_prompt(qkv, bias_p, jnp.zeros((m, chk_w), BF16), n_blocks=seq // BAND_TQ, HB=BAND_HB, TQ=BAND_TQ,
                       NP=BAND_PARTS)
    att = _band_sample(qkv, cache_chk_k[0].reshape(nb_s, p_band, chk_w), cache_chk_v[0].reshape(nb_s, p_band, chk_w),
                       bias_s, att, n_seq=nb_s, T=t_s, row0=seq, HB=BAND_HB)
    mix = _rg_matmul(tokens[0], att, w_out_odd, tn=TN)
    x4, x4b = _residual_layer_norm(x3, mix, ln_mix_g[1], ln_mix_b[1], tr=TR)

    logits = _rg_matmul(tokens[0], x4b, router_w, tn=N_EXPERTS)
    top_e, gates = _router_top2(logits, tr=TR)
    tok_slot, slot_tok, experts = _routing(top_e, SB_MOE, RG_MOE)
    xs = _gather_rows(slot_tok, x4, tb=SB_MOE)
    acts = _rg_swiglu(experts, xs, exp_w_gate[0], exp_w_up[0], tn=TN_PAIR)
    yb = _rg_down(experts, acts, exp_w_down[0], tn=TN_DOWN, tk=TK_DOWN)
    x5, x5b = _moe_combine_ln(tok_slot, yb, gates, x4, ln_ffn_g[1], ln_ffn_b[1], tb=TR)
    y, _ = _rg_ple(tokens[1], x5b, ple_gate, p[1], ple_proj, x5, tn=TN_PAIR)

    ck = kv_new[:, :chk_w]
    cv = kv_new[:, chk_w:]

    def heads(a, nh):
        return a.reshape(a.shape[0], nh, HEAD_DIM)

    return (y[:seq][None], y[seq:].reshape(nb_s, t_s, d),
            st_p.reshape(1, 1, H_RET, HEAD_DIM, HEAD_DIM), st_s.reshape(1, nb_s, H_RET, HEAD_DIM, HEAD_DIM),
            heads(dk[:seq], KVH_DSA)[None, None], heads(dv[:seq], KVH_DSA)[None, None], ik[:seq][None, None],
            heads(dk[seq:], KVH_DSA).reshape(1, nb_s, t_s, KVH_DSA, HEAD_DIM),
            heads(dv[seq:], KVH_DSA).reshape(1, nb_s, t_s, KVH_DSA, HEAD_DIM),
            ik[seq:].reshape(1, nb_s, t_s, D_IDX),
            heads(ck[:keep], H_CHK)[None, None], heads(cv[:keep], H_CHK)[None, None],
            heads(ck[keep:], H_CHK).reshape(1, nb_s, t_s, H_CHK, HEAD_DIM),
            heads(cv[keep:], H_CHK).reshape(1, nb_s, t_s, H_CHK, HEAD_DIM))
```

```python
import functools
import math

import numpy as np
import jax
import jax.numpy as jnp
from jax import lax
from jax.experimental import pallas as pl
from jax.experimental.pallas import tpu as pltpu

F32 = jnp.float32
BF16 = jnp.bfloat16
I32 = jnp.int32

CHUNK = 64
HEAD_DIM = 128
H_RET = 16
H_DSA = 16
KVH_DSA = 4
DSA_GROUP = H_DSA // KVH_DSA
H_IDX = 32
D_IDX = 64
TOPK_MAX = 256
H_CHK = 32
LEFT_CHUNKS = 8
REL_CLIP = 128
N_EXPERTS = 8
TOP_K = 2
RET_THETA = 10000.0
ROPE_THETA = 500000.0
ROPE_DIM = HEAD_DIM // 4
IDX_ROPE_DIM = D_IDX // 4
LN_EPS = 1e-5
DEPTH = 2
ALPHA = (2.0 * DEPTH) ** 0.25

LANES = 128
DMA_PRIORITIES = 2
MAX_PIECE = 8
VMEM_LIMIT = 56 * 1024 * 1024
NEG_BIG = -1e30
EXP2_SCALE = HEAD_DIM ** -0.5 * math.log2(math.e)
INT_MIN = -(2 ** 31)
INT_MAX = 2 ** 31 - 1


def _params(*sem):
    return pltpu.CompilerParams(dimension_semantics=sem, vmem_limit_bytes=VMEM_LIMIT)


class _Groups:
    def __init__(self, e, st, nb, rg, sb, dense):
        self.e, self.st, self.nb, self.rg, self.sb, self.dense = e, st, nb, rg, sb, dense
        self.n = e.shape[0]

    def rows(self, width, col, buffers=None, own=False):
        kw = {} if buffers is None else dict(pipeline_mode=pl.Buffered(buffers))
        if own:
            return pl.BlockSpec((self.rg, width), lambda s, *a: (s, col(a[-1][s] > 0, *a[:-3])), **kw)
        if self.dense:
            return pl.BlockSpec((self.rg, width), lambda s, *a: (a[-2][s], col(a[-1][s] > 0, *a[:-3])), **kw)
        unit = self.sb
        return pl.BlockSpec((pl.Element(self.rg), pl.Element(width)),
                            lambda s, *a: (a[-2][s] * unit, col(a[-1][s] > 0, *a[:-3]) * width), **kw)

    def weights(self, k_rows, width, kcol):
        return pl.BlockSpec((None, k_rows, width), lambda s, *a: (a[-3][s], *kcol(a[-1][s] > 0, *a[:-3])))


def _dense_groups(m, rg, lead=0):
    n = m // rg
    return _Groups(jnp.full((n,), lead, I32), jnp.arange(n, dtype=I32), jnp.ones((n,), I32), rg, rg, True)


def _live_rows(nb_ref, n_sub, sb, compute):
    nblk = nb_ref[pl.program_id(0)]
    if n_sub == 1:
        @pl.when(nblk > 0)
        def _():
            compute(pl.ds(0, sb))

        return nblk

    def run_full(r, c):
        rows = MAX_PIECE * sb
        compute(pl.ds(pl.multiple_of(r * rows, rows), rows))
        return c

    lax.fori_loop(0, nblk // MAX_PIECE, run_full, 0)
    piece = MAX_PIECE // 2
    while piece >= 1:
        def run_piece(piece=piece):
            start = (nblk // (2 * piece)) * (2 * piece * sb)
            compute(pl.ds(pl.multiple_of(start, 2 * piece * sb), piece * sb))

        pl.when((nblk // piece) % 2 == 1)(run_piece)
        piece //= 2
    return nblk


def _row_loops(nb_ref, n_sub, sb, compute, o_refs):
    nblk = _live_rows(nb_ref, n_sub, sb, compute)

    def clear(r, c):
        for o_ref in o_refs:
            o_ref[pl.ds(pl.multiple_of(r * sb, sb), sb), :] = jnp.zeros((sb, o_ref.shape[1]), o_ref.dtype)
        return c

    lax.fori_loop(nblk, n_sub, clear, 0)


def _bf16_dot(x, w_ref):
    return jnp.dot(x, w_ref[...].astype(BF16), preferred_element_type=F32)


def _rg_mm_body(e_ref, st_ref, nb_ref, x_ref, w_ref, o_ref, *, sb):
    def compute(rows):
        o_ref[rows, :] = _bf16_dot(x_ref[rows, :], w_ref).astype(o_ref.dtype)

    _row_loops(nb_ref, x_ref.shape[0] // sb, sb, compute, [o_ref])


def _rg_swiglu_body(e_ref, st_ref, nb_ref, x_ref, wg_ref, wu_ref, o_ref, *, sb):
    def compute(rows):
        x = x_ref[rows, :]
        g = _bf16_dot(x, wg_ref)
        u = _bf16_dot(x, wu_ref)
        o_ref[rows, :] = (jax.nn.silu(g) * u).astype(o_ref.dtype)

    _row_loops(nb_ref, x_ref.shape[0] // sb, sb, compute, [o_ref])


def _rg_ple_body(e_ref, st_ref, nb_ref, x_ref, w_ref, p_ref, pw_ref, r_ref, of_ref, ob_ref, *, sb):
    def compute(rows):
        gate = _bf16_dot(x_ref[rows, :], w_ref)
        proj = _bf16_dot(p_ref[rows, :], pw_ref)
        y = r_ref[rows, :] + jax.nn.sigmoid(gate) * proj
        of_ref[rows, :] = y
        ob_ref[rows, :] = y.astype(BF16)

    _row_loops(nb_ref, x_ref.shape[0] // sb, sb, compute, [of_ref, ob_ref])


def _rg_down_body(e_ref, st_ref, nb_ref, a_ref, w_ref, o_ref, *, sb):
    @pl.when(pl.program_id(2) == 0)
    def _():
        o_ref[...] = jnp.zeros_like(o_ref)

    def compute(rows):
        o_ref[rows, :] += _bf16_dot(a_ref[rows, :], w_ref)

    _live_rows(nb_ref, a_ref.shape[0] // sb, sb, compute)


def _last_if_dead(n_blocks):
    return lambda live, j: jnp.where(live, j, n_blocks - 1)


def _rg_matmul(g, x, w, *, tn, out_dtype=F32, stripes=None):
    K = x.shape[1]
    j0, J = (0, pl.cdiv(w.shape[-1], tn)) if stripes is None else stripes
    N = w.shape[-1] if stripes is None else J * tn
    col = _last_if_dead(J)
    return pl.pallas_call(
        functools.partial(_rg_mm_body, sb=g.sb),
        grid_spec=pltpu.PrefetchScalarGridSpec(
            num_scalar_prefetch=3, grid=(g.n, J),
            in_specs=[g.rows(K, lambda live, j: 0, buffers=1),
                      g.weights(K, tn, lambda live, j: (0, j0 + col(live, j)))],
            out_specs=g.rows(tn, lambda live, j: j, own=True)),
        out_shape=jax.ShapeDtypeStruct((g.n * g.rg, N), out_dtype),
        compiler_params=_params("arbitrary", "arbitrary"),
        name="rg_mm",
    )(g.e, g.st, g.nb, x, w)


def _rg_swiglu(g, x, wg, wu, *, tn):
    K = x.shape[1]
    N = wg.shape[-1]
    col = _last_if_dead(N // tn)
    wspec = g.weights(K, tn, lambda live, j: (0, col(live, j)))
    return pl.pallas_call(
        functools.partial(_rg_swiglu_body, sb=g.sb),
        grid_spec=pltpu.PrefetchScalarGridSpec(
            num_scalar_prefetch=3, grid=(g.n, N // tn),
            in_specs=[g.rows(K, lambda live, j: 0, buffers=1), wspec, wspec],
            out_specs=g.rows(tn, lambda live, j: j, own=True)),
        out_shape=jax.ShapeDtypeStruct((g.n * g.rg, N), BF16),
        compiler_params=_params("arbitrary", "arbitrary"),
        name="rg_swiglu",
    )(g.e, g.st, g.nb, x, wg, wu)


def _rg_ple(g, x, w, p, pw, resid, *, tn):
    K, KP = x.shape[1], p.shape[1]
    N = w.shape[-1]
    col = _last_if_dead(N // tn)
    wcol = lambda live, j: (0, col(live, j))
    tile = g.rows(tn, lambda live, j: j, own=True)
    return pl.pallas_call(
        functools.partial(_rg_ple_body, sb=g.sb),
        grid_spec=pltpu.PrefetchScalarGridSpec(
            num_scalar_prefetch=3, grid=(g.n, N // tn),
            in_specs=[g.rows(K, lambda live, j: 0, buffers=1), g.weights(K, tn, wcol),
                      g.rows(KP, lambda live, j: 0, buffers=1), g.weights(KP, tn, wcol), tile],
            out_specs=[tile, tile]),
        out_shape=[jax.ShapeDtypeStruct((g.n * g.rg, N), F32), jax.ShapeDtypeStruct((g.n * g.rg, N), BF16)],
        compiler_params=_params("arbitrary", "arbitrary"),
        name="rg_ple",
    )(g.e, g.st, g.nb, x, w, p, pw, resid)


def _rg_down(g, a, w, *, tn, tk):
    K = a.shape[1]
    N = w.shape[-1]
    NT, KC = N // tn, K // tk
    ncol = lambda live, n, kc: jnp.where(live, n, NT - 1)
    kcol = lambda live, n, kc: jnp.where(live, kc, KC - 1)
    return pl.pallas_call(
        functools.partial(_rg_down_body, sb=g.sb),
        grid_spec=pltpu.PrefetchScalarGridSpec(
            num_scalar_prefetch=3, grid=(g.n, NT, KC),
            in_specs=[g.rows(tk, kcol, own=True),
                      g.weights(tk, tn, lambda live, n, kc: (kcol(live, n, kc), ncol(live, n, kc)))],
            out_specs=g.rows(tn, lambda live, n, kc: n, buffers=1, own=True)),
        out_shape=jax.ShapeDtypeStruct((g.n * g.rg, N), F32),
        compiler_params=_params("arbitrary", "arbitrary", "arbitrary"),
        name="rg_down",
    )(g.e, g.st, g.nb, a, w)


def _layer_norm_rows(z, g, b):
    mu = jnp.mean(z, -1, keepdims=True)
    d = z - mu
    var = jnp.mean(d * d, -1, keepdims=True)
    return d * lax.rsqrt(var + LN_EPS) * g + b


def _ln_body(a_ref, b_ref, g_ref, beta_ref, of_ref, ob_ref):
    y = _layer_norm_rows(ALPHA * a_ref[...] + b_ref[...], g_ref[...], beta_ref[...])
    of_ref[...] = y
    ob_ref[...] = y.astype(BF16)


def _residual_layer_norm(a, b, g, beta, *, tr):
    M, D = a.shape
    row = pl.BlockSpec((tr, D), lambda i: (i, 0))
    vec = pl.BlockSpec((1, D), lambda i: (0, 0))
    return pl.pallas_call(
        _ln_body,
        grid=(M // tr,),
        in_specs=[row, row, vec, vec],
        out_specs=[row, row],
        out_shape=[jax.ShapeDtypeStruct((M, D), F32), jax.ShapeDtypeStruct((M, D), BF16)],
        compiler_params=_params("parallel"),
        name="res_ln",
    )(a, b, g.reshape(1, D), beta.reshape(1, D))


def _stack_body(a_ref, b_ref, of_ref, ob_ref, *, n_a):
    def emit(src_ref):
        y = src_ref[...]
        of_ref[...] = y
        ob_ref[...] = y.astype(BF16)

    pl.when(pl.program_id(0) < n_a)(functools.partial(emit, a_ref))
    pl.when(pl.program_id(0) >= n_a)(functools.partial(emit, b_ref))


def _stack_rows(a, b, *, tr):
    n_a, n_b = a.shape[0] // tr, b.shape[0] // tr
    D = a.shape[1]
    row = pl.BlockSpec((tr, D), lambda i: (i, 0))
    return pl.pallas_call(
        functools.partial(_stack_body, n_a=n_a),
        grid=(n_a + n_b,),
        in_specs=[pl.BlockSpec((tr, D), lambda i: (jnp.minimum(i, n_a - 1), 0)),
                  pl.BlockSpec((tr, D), lambda i: (jnp.maximum(i - n_a, 0), 0))],
        out_specs=[row, row],
        out_shape=[jax.ShapeDtypeStruct((a.shape[0] + b.shape[0], D), F32),
                   jax.ShapeDtypeStruct((a.shape[0] + b.shape[0], D), BF16)],
        compiler_params=_params("arbitrary"),
        name="stack_rows",
    )(a, b)


def _rotate(x, c, s_up, s_down, half):
    w = x.shape[1]
    reps = w // LANES
    if reps > 1:
        c, s_up, s_down = (jnp.concatenate([t] * reps, axis=1) for t in (c, s_up, s_down))
    return x * c + pltpu.roll(x, w - half, 1) * s_up + pltpu.roll(x, half, 1) * s_down


def _rot_body(x_ref, c_ref, su_ref, sd_ref, *o_refs, half):
    y = _rotate(x_ref[...], c_ref[...], su_ref[...], sd_ref[...], half)
    for o_ref in o_refs:
        o_ref[...] = y.astype(o_ref.dtype)


def _rotary_cols(h, col_block, width, tables, half, out_dtypes, *, tr):
    M = h.shape[0]
    tab = pl.BlockSpec((tr, LANES), lambda i: (i, 0))
    return pl.pallas_call(
        functools.partial(_rot_body, half=half),
        grid=(M // tr,),
        in_specs=[pl.BlockSpec((tr, width), lambda i: (i, col_block)), tab, tab, tab],
        out_specs=[pl.BlockSpec((tr, width), lambda i: (i, 0)) for _ in out_dtypes],
        out_shape=[jax.ShapeDtypeStruct((M, width), dt) for dt in out_dtypes],
        compiler_params=_params("parallel"),
        name="rotary",
    )(h, *tables)


def _copy_body(x_ref, *o_refs):
    for o_ref in o_refs:
        o_ref[...] = x_ref[...].astype(o_ref.dtype)


def _copy_cols(h, col_block, width, out_dtypes, *, tr):
    M = h.shape[0]
    return pl.pallas_call(
        _copy_body,
        grid=(M // tr,),
        in_specs=[pl.BlockSpec((tr, width), lambda i: (i, col_block))],
        out_specs=[pl.BlockSpec((tr, width), lambda i: (i, 0)) for _ in out_dtypes],
        out_shape=[jax.ShapeDtypeStruct((M, width), dt) for dt in out_dtypes],
        compiler_params=_params("parallel"),
        name="copy_cols",
    )(h)


def _ik_body(x_ref, g_ref, b_ref, c_ref, su_ref, sd_ref, ik_ref, ik2_ref, iw_ref):
    x = x_ref[...]
    is_key = lax.broadcasted_iota(I32, x.shape, 1) < D_IDX
    mu = jnp.sum(jnp.where(is_key, x, 0.0), -1, keepdims=True) / D_IDX
    d = jnp.where(is_key, x - mu, 0.0)
    var = jnp.sum(d * d, -1, keepdims=True) / D_IDX
    y = jnp.where(is_key, d * lax.rsqrt(var + LN_EPS) * g_ref[...] + b_ref[...], 0.0)
    y = _rotate(y, c_ref[...], su_ref[...], sd_ref[...], IDX_ROPE_DIM // 2)
    y = jnp.where(is_key, y, 0.0)
    ik_ref[...] = y[:, :D_IDX]
    ik2_ref[...] = (y + pltpu.roll(y, D_IDX, 1)).astype(BF16)
    iw_ref[...] = pltpu.roll(x, LANES - D_IDX, 1)[:, :H_IDX]


def _indexer_keys(h, col_block, g, b, tables, *, tr):
    M = h.shape[0]
    pad = LANES - D_IDX
    tab = pl.BlockSpec((tr, LANES), lambda i: (i, 0))
    vec = pl.BlockSpec((1, LANES), lambda i: (0, 0))
    return pl.pallas_call(
        _ik_body,
        grid=(M // tr,),
        in_specs=[pl.BlockSpec((tr, LANES), lambda i: (i, col_block)), vec, vec, tab, tab, tab],
        out_specs=[pl.BlockSpec((tr, D_IDX), lambda i: (i, 0)), pl.BlockSpec((tr, LANES), lambda i: (i, 0)),
                   pl.BlockSpec((tr, H_IDX), lambda i: (i, 0))],
        out_shape=[jax.ShapeDtypeStruct((M, D_IDX), F32), jax.ShapeDtypeStruct((M, LANES), BF16),
                   jax.ShapeDtypeStruct((M, H_IDX), F32)],
        compiler_params=_params("parallel"),
        name="indexer_keys",
    )(h, jnp.pad(g, (0, pad)).reshape(1, LANES), jnp.pad(b, (0, pad)).reshape(1, LANES), *tables)


def _router_body(l_ref, e_ref, g_ref):
    lg = l_ref[...]
    idx = lax.broadcasted_iota(I32, lg.shape, 1)
    m1 = jnp.max(lg, -1, keepdims=True)
    e1 = jnp.min(jnp.where(lg == m1, idx, N_EXPERTS), -1, keepdims=True)
    rest = jnp.where(idx == e1, -jnp.inf, lg)
    m2 = jnp.max(rest, -1, keepdims=True)
    e2 = jnp.min(jnp.where(rest == m2, idx, N_EXPERTS), -1, keepdims=True)
    ex2 = jnp.exp(m2 - m1)
    den = 1.0 + ex2
    e_ref[...] = jnp.concatenate([e1, e2], axis=1)
    g_ref[...] = jnp.concatenate([1.0 / den, ex2 / den], axis=1)


def _router_top2(logits, *, tr):
    M, E = logits.shape
    return pl.pallas_call(
        _router_body,
        grid=(M // tr,),
        in_specs=[pl.BlockSpec((tr, E), lambda i: (i, 0))],
        out_specs=[pl.BlockSpec((tr, TOP_K), lambda i: (i, 0)), pl.BlockSpec((tr, TOP_K), lambda i: (i, 0))],
        out_shape=[jax.ShapeDtypeStruct((M, TOP_K), I32), jax.ShapeDtypeStruct((M, TOP_K), F32)],
        compiler_params=_params("parallel"),
        name="router_top2",
    )(logits)


def _row_copy(src_hbm, row, dst, r, sem):
    return pltpu.make_async_copy(src_hbm.at[pl.ds(row, 1), :], dst.at[pl.ds(r, 1), :], sem)


def _gather_body(tok_ref, x_hbm, o_ref, buf, sem, *, tb):
    base = pl.program_id(0) * tb

    def start(r2, _):
        for u in range(DMA_PRIORITIES):
            r = r2 * DMA_PRIORITIES + u
            _row_copy(x_hbm, tok_ref[base + r], buf, r, sem).start(priority=u)
        return 0

    def wait(r, _):
        _row_copy(x_hbm, 0, buf, r, sem).wait()
        return 0

    lax.fori_loop(0, tb // DMA_PRIORITIES, start, 0)
    lax.fori_loop(0, tb, wait, 0)
    o_ref[...] = buf[...].astype(BF16)


def _gather_rows(slot_tok, x, *, tb):
    A = slot_tok.shape[0]
    D = x.shape[1]
    return pl.pallas_call(
        functools.partial(_gather_body, tb=tb),
        grid_spec=pltpu.PrefetchScalarGridSpec(
            num_scalar_prefetch=1,
            grid=(A // tb,),
            in_specs=[pl.BlockSpec(memory_space=pl.ANY)],
            out_specs=pl.BlockSpec((tb, D), lambda i, tok: (i, 0)),
            scratch_shapes=[pltpu.VMEM((tb, D), F32), pltpu.SemaphoreType.DMA(())]),
        out_shape=jax.ShapeDtypeStruct((A, D), BF16),
        compiler_params=_params("arbitrary"),
        name="moe_gather",
    )(slot_tok, x)


def _combine_body(slot_ref, y_hbm, gate_ref, x_ref, g_ref, beta_ref, of_ref, ob_ref, buf, sem, *, tb):
    base = pl.program_id(0) * tb

    def start(r, _):
        for k in range(TOP_K):
            _row_copy(y_hbm, slot_ref[(base + r) * TOP_K + k], buf.at[k], r, sem).start(priority=k % DMA_PRIORITIES)
        return 0

    def wait(r, _):
        for k in range(TOP_K):
            _row_copy(y_hbm, 0, buf.at[k], r, sem).wait()
        return 0

    lax.fori_loop(0, tb, start, 0)
    lax.fori_loop(0, tb, wait, 0)
    gates = gate_ref[...]
    y = buf[0] * gates[:, 0:1] + buf[1] * gates[:, 1:2]
    z = _layer_norm_rows(ALPHA * x_ref[...] + y, g_ref[...], beta_ref[...])
    of_ref[...] = z
    ob_ref[...] = z.astype(BF16)


def _moe_combine_ln(tok_slot, yb, gates, x, g, beta, *, tb):
    M, D = x.shape
    row = pl.BlockSpec((tb, D), lambda i, s: (i, 0))
    vec = pl.BlockSpec((1, D), lambda i, s: (0, 0))
    return pl.pallas_call(
        functools.partial(_combine_body, tb=tb),
        grid_spec=pltpu.PrefetchScalarGridSpec(
            num_scalar_prefetch=1,
            grid=(M // tb,),
            in_specs=[pl.BlockSpec(memory_space=pl.ANY), pl.BlockSpec((tb, TOP_K), lambda i, s: (i, 0)),
                      row, vec, vec],
            out_specs=[row, row],
            scratch_shapes=[pltpu.VMEM((TOP_K, tb, D), F32), pltpu.SemaphoreType.DMA(())]),
        out_shape=[jax.ShapeDtypeStruct((M, D), F32), jax.ShapeDtypeStruct((M, D), BF16)],
        compiler_params=_params("arbitrary"),
        name="moe_combine",
    )(tok_slot.reshape(-1), yb, gates, x, g.reshape(1, D), beta.reshape(1, D))


def _ret_body(q_ref, k_ref, v_ref, gate_ref, c_ref, su_ref, sd_ref, lg_ref, gn_ref, s0_ref, _, o_ref, st_ref,
              s_scr, *, C):
    @pl.when(pl.program_id(1) == 0)
    def _():
        s_scr[...] = s0_ref[...]

    c, su, sd = c_ref[...], su_ref[...], sd_ref[...]
    q = _rotate(q_ref[...], c, su, sd, HEAD_DIM // 2)
    k = _rotate(k_ref[...], c, su, sd, HEAD_DIM // 2) * HEAD_DIM ** -0.5
    vb = v_ref[...].astype(BF16)
    lg_row = lg_ref[0:1, :C]
    lg_lane = lg_ref[0:1, :LANES]
    n_col = lax.broadcasted_iota(I32, (C, C), 0)
    m_row = lax.broadcasted_iota(I32, (C, C), 1)
    diff = (n_col - m_row).astype(F32)
    intra = jnp.where(diff >= 0, jnp.exp(lg_row * jnp.maximum(diff, 0.0)), 0.0)
    n_idx = lax.broadcasted_iota(I32, (C, LANES), 0).astype(F32)
    q_dec = jnp.exp(lg_lane * (n_idx + 1.0))
    k_dec = jnp.exp(lg_lane * (C - 1.0 - n_idx))
    c_dec = jnp.exp(lg_lane * C)

    qb = q.astype(BF16)
    s = lax.dot_general(qb, k.astype(BF16), (((1,), (1,)), ((), ())), preferred_element_type=F32) * intra
    state = s_scr[...]
    o = (jnp.dot(s.astype(BF16), vb, preferred_element_type=F32)
         + jnp.dot(qb, state.astype(BF16), preferred_element_type=F32) * q_dec)
    kd = (k * k_dec).astype(BF16)
    state = state * c_dec + lax.dot_general(kd, vb, (((0,), (0,)), ((), ())), preferred_element_type=F32)
    s_scr[...] = state
    st_ref[...] = state

    mu = jnp.mean(o, -1, keepdims=True)
    d = o - mu
    var = jnp.mean(d * d, -1, keepdims=True)
    on = d * lax.rsqrt(var + LN_EPS) * gn_ref[...]
    o_ref[...] = (on * jax.nn.silu(gate_ref[...])).astype(o_ref.dtype)


def _retention(h, tables, lg_tab, gn_g, state0, out_buf, *, C, n_seq, n_chunks, row0):
    rb0 = row0 // C

    def rows(sh, c):
        return rb0 + (sh // H_RET) * n_chunks + c

    def hcol(off):
        return pl.BlockSpec((C, HEAD_DIM), lambda sh, c, off=off: (rows(sh, c), off + sh % H_RET))

    tab = pl.BlockSpec((C, LANES), lambda sh, c: (rows(sh, c), 0))
    in_specs = [hcol(0), hcol(H_RET), hcol(2 * H_RET), hcol(3 * H_RET), tab, tab, tab,
                pl.BlockSpec((None, 8, lg_tab.shape[2]), lambda sh, c: (sh % H_RET, 0, 0)),
                pl.BlockSpec((1, HEAD_DIM), lambda sh, c: (0, sh % H_RET)),
                pl.BlockSpec((None, HEAD_DIM, HEAD_DIM), lambda sh, c: (sh, 0, 0)),
                pl.BlockSpec(memory_space=pl.ANY)]
    state_shape = (n_seq * H_RET, HEAD_DIM, HEAD_DIM)
    return pl.pallas_call(
        functools.partial(_ret_body, C=C),
        grid=(n_seq * H_RET, n_chunks),
        in_specs=in_specs,
        out_specs=[pl.BlockSpec((C, HEAD_DIM), lambda sh, c: (rows(sh, c), sh % H_RET)),
                   pl.BlockSpec((None, HEAD_DIM, HEAD_DIM), lambda sh, c: (sh, 0, 0))],
        out_shape=[jax.ShapeDtypeStruct(out_buf.shape, out_buf.dtype), jax.ShapeDtypeStruct(state_shape, F32)],
        scratch_shapes=[pltpu.VMEM((HEAD_DIM, HEAD_DIM), F32)],
        input_output_aliases={len(in_specs) - 1: 0},
        compiler_params=_params("arbitrary", "arbitrary"),
        name="retention",
    )(h, h, h, h, *tables, lg_tab, gn_g.reshape(1, -1), state0.reshape(state_shape), out_buf)


def _dsa_body(q_ref, iq_ref, iwt_ref, k_ref, v_ref, ik2_ref, *rest, TQ, TK, L, n_sel, q_pos0, q_stride, causal):
    o_ref, key_scr, iqm_scr, q4_scr, m_scr, l_scr, acc_scr = rest[-7:]
    G = DSA_GROUP
    i = pl.program_id(0)
    q_pos = q_pos0 + i * q_stride + lax.broadcasted_iota(I32, (1, TQ), 1)
    q_lim = ((q_pos >> 6) + 1) << 6
    if causal:
        n_tiles = jnp.minimum(((i + 1) * q_stride + TK - 1) // TK, L // TK)
    else:
        n_tiles = L // TK

    lane = lax.broadcasted_iota(I32, (TQ, LANES), 1)
    for j in range(H_IDX // 2):
        pair = iq_ref[:, j * LANES:(j + 1) * LANES]
        iqm_scr[j, :TQ, :] = jnp.where(lane < D_IDX, pair, jnp.zeros_like(pair))
        iqm_scr[j, TQ:, :] = jnp.where(lane >= D_IDX, pair, jnp.zeros_like(pair))
    iwt = iwt_ref[...] * (H_IDX ** -0.5 * D_IDX ** -0.5)

    def key_pos(kt):
        return kt * TK + lax.broadcasted_iota(I32, (TK, TQ), 0)

    def score_tile(kt, _):
        ik2 = ik2_ref[pl.ds(kt * TK, TK), :]
        acc = jnp.zeros((TK, TQ), F32)
        for j in range(H_IDX // 2):
            a = lax.dot_general(ik2, iqm_scr[j], (((1,), (1,)), ((), ())), preferred_element_type=F32)
            acc = (acc + jnp.maximum(a[:, :TQ], 0.0) * iwt[2 * j:2 * j + 1, :]
                   + jnp.maximum(a[:, TQ:], 0.0) * iwt[2 * j + 1:2 * j + 2, :])
        acc = jnp.where(key_pos(kt) < q_lim, acc, -jnp.inf)
        bits = pltpu.bitcast(acc, I32)
        key_scr[pl.ds(kt * TK, TK), :] = bits ^ ((bits >> 31) & INT_MAX)
        return 0

    lax.fori_loop(0, n_tiles, score_tile, 0)

    def bisect(_, carry):
        lo, hi = carry
        mid = lo + lax.shift_right_logical(hi - lo, 1)

        def count_tile(kt, c8):
            ge = (key_scr[pl.ds(kt * TK, TK), :] >= mid).astype(I32)
            return c8 + jnp.sum(ge.reshape(TK // 8, 8, TQ), axis=0)

        c8 = lax.fori_loop(0, n_tiles, count_tile, jnp.zeros((8, TQ), I32))
        enough = jnp.sum(c8, axis=0, keepdims=True) >= n_sel
        return jnp.where(enough, mid, lo), jnp.where(enough, hi, mid)

    thr, _ = lax.fori_loop(0, 32, bisect, (jnp.full((1, TQ), INT_MIN, I32), jnp.full((1, TQ), INT_MAX, I32)))

    for kv in range(KVH_DSA):
        q4_scr[kv] = jnp.concatenate(
            [q_ref[:, (kv * G + g) * HEAD_DIM:(kv * G + g + 1) * HEAD_DIM] for g in range(G)], axis=0)
    m_scr[...] = jnp.full_like(m_scr, NEG_BIG)
    l_scr[...] = jnp.zeros_like(l_scr)
    acc_scr[...] = jnp.zeros_like(acc_scr)

    def attend_tile(kt, _):
        sel = jnp.logical_and(key_scr[pl.ds(kt * TK, TK), :] >= thr, key_pos(kt) < q_lim)
        sel4 = jnp.concatenate([sel] * G, axis=1)
        for kv in range(KVH_DSA):
            cols = slice(kv * HEAD_DIM, (kv + 1) * HEAD_DIM)
            kt_ = k_ref[pl.ds(kt * TK, TK), cols]
            vt_ = v_ref[pl.ds(kt * TK, TK), cols]
            lg = lax.dot_general(kt_, q4_scr[kv], (((1,), (1,)), ((), ())), preferred_element_type=F32)
            lg = jnp.where(sel4, lg, NEG_BIG)
            m_old = m_scr[kv]
            m_new = jnp.maximum(m_old, jnp.max(lg, axis=0, keepdims=True))
            alpha = jnp.exp2((m_old - m_new) * EXP2_SCALE)
            p = jnp.exp2((lg - m_new) * EXP2_SCALE)
            l_scr[kv] = alpha * l_scr[kv] + jnp.sum(p, axis=0, keepdims=True)
            pv = lax.dot_general(vt_, p.astype(BF16), (((0,), (0,)), ((), ())), preferred_element_type=F32)
            acc_scr[kv] = acc_scr[kv] * alpha + pv
            m_scr[kv] = m_new
        return 0

    lax.fori_loop(0, n_tiles, attend_tile, 0)

    for kv in range(KVH_DSA):
        o_t = acc_scr[kv] / l_scr[kv]
        for g in range(G):
            hd = kv * G + g
            o_ref[:, hd * HEAD_DIM:(hd + 1) * HEAD_DIM] = o_t[:, g * TQ:(g + 1) * TQ].T.astype(o_ref.dtype)


def _dsa(q, iq, iwt, k, v, ik2, out_buf, *, TQ, TK, n_blocks, n_sel, q_pos0, q_stride, causal,
         q_row0, out_row0, out_col0, per_block_keys):
    L = k.shape[-2]
    W = H_DSA * HEAD_DIM
    qb0, ob0, oc0 = q_row0 // TQ, out_row0 // TQ, out_col0 // W
    if per_block_keys:
        kspec = lambda a: pl.BlockSpec((None, L, a.shape[-1]), lambda i: (i, 0, 0))
    else:
        kspec = lambda a: pl.BlockSpec((L, a.shape[-1]), lambda i: (0, 0), pipeline_mode=pl.Buffered(1))
    G = DSA_GROUP
    return pl.pallas_call(
        functools.partial(_dsa_body, TQ=TQ, TK=TK, L=L, n_sel=n_sel, q_pos0=q_pos0, q_stride=q_stride,
                          causal=causal),
        grid=(n_blocks,),
        in_specs=[pl.BlockSpec((TQ, W), lambda i: (qb0 + i, 0)),
                  pl.BlockSpec((TQ, H_IDX * D_IDX), lambda i: (qb0 + i, 0)),
                  pl.BlockSpec((H_IDX, TQ), lambda i: (0, qb0 + i)),
                  kspec(k), kspec(v), kspec(ik2),
                  pl.BlockSpec(memory_space=pl.ANY)],
        out_specs=pl.BlockSpec((TQ, W), lambda i: (ob0 + i, oc0)),
        out_shape=jax.ShapeDtypeStruct(out_buf.shape, out_buf.dtype),
        scratch_shapes=[pltpu.VMEM((L, TQ), I32), pltpu.VMEM((H_IDX // 2, 2 * TQ, LANES), BF16),
                        pltpu.VMEM((KVH_DSA, G * TQ, HEAD_DIM), BF16),
                        pltpu.VMEM((KVH_DSA, 1, G * TQ), F32), pltpu.VMEM((KVH_DSA, 1, G * TQ), F32),
                        pltpu.VMEM((KVH_DSA, HEAD_DIM, G * TQ), F32)],
        input_output_aliases={6: 0},
        compiler_params=_params("arbitrary"),
        name="dsa",
    )(q, iq, iwt, k, v, ik2, out_buf)


def _bias_body(rb_ref, idx_ref, o_ref):
    n = rb_ref.shape[1]
    onehot = (lax.broadcasted_iota(I32, (n, idx_ref.shape[1]), 0) == idx_ref[...]).astype(F32)
    o_ref[...] = jnp.dot(rb_ref[...], onehot, preferred_element_type=F32, precision=lax.Precision.HIGHEST)


def _expand_bias(rel_bias, n_q, n_k, *, tn):
    H, R = rel_bias.shape
    RP = 3 * LANES
    t = np.arange(n_q)[:, None]
    s = np.arange(n_k)[None, :]
    idx = (np.clip(LEFT_CHUNKS * CHUNK + t - s, -REL_CLIP, REL_CLIP) + REL_CLIP).reshape(1, -1).astype(np.int32)
    out = pl.pallas_call(
        _bias_body,
        grid=(idx.shape[1] // tn,),
        in_specs=[pl.BlockSpec((H, RP), lambda j: (0, 0)), pl.BlockSpec((1, tn), lambda j: (0, j))],
        out_specs=pl.BlockSpec((H, tn), lambda j: (0, j)),
        out_shape=jax.ShapeDtypeStruct((H, idx.shape[1]), F32),
        compiler_params=_params("parallel"),
        name="rel_bias_expand",
    )(jnp.pad(rel_bias, ((0, 0), (0, RP - R))), jnp.asarray(idx))
    return out.reshape(H, n_q, n_k)


def _head_cols(ref):
    return lambda hh: ref[:, hh * HEAD_DIM:(hh + 1) * HEAD_DIM]


def _band_heads(q_ref, k_parts, v_parts, bias_ref, ok, o_ref, HB):
    def window(parts, hh):
        return jnp.concatenate([part(hh).astype(BF16) for part in parts], axis=0)

    q_head = _head_cols(q_ref)
    logits = [lax.dot_general(q_head(hh).astype(BF16), window(k_parts, hh), (((1,), (1,)), ((), ())),
                              preferred_element_type=F32) for hh in range(HB)]
    probs = []
    for hh in range(HB):
        s = logits[hh] * HEAD_DIM ** -0.5 + bias_ref[hh]
        if ok is not None:
            s = jnp.where(ok, s, NEG_BIG)
        p = jnp.exp(s - jnp.max(s, -1, keepdims=True))
        probs.append((p / jnp.sum(p, -1, keepdims=True)).astype(BF16))
    for hh in range(HB):
        o_ref[:, hh * HEAD_DIM:(hh + 1) * HEAD_DIM] = jnp.dot(
            probs[hh], window(v_parts, hh), preferred_element_type=F32).astype(o_ref.dtype)


def _band_prompt_body(q_ref, *rest, HB, TQ, NP):
    k_refs, v_refs = rest[:NP], rest[NP:2 * NP]
    bias_ref, o_ref = rest[2 * NP], rest[2 * NP + 2]
    j = pl.program_id(1)
    nk = NP * TQ
    q_pos = j * TQ + lax.broadcasted_iota(I32, (TQ, nk), 0)
    k_pos = (j - (NP - 1)) * TQ + lax.broadcasted_iota(I32, (TQ, nk), 1)
    qc, kc = q_pos >> 6, k_pos >> 6
    ok = jnp.logical_and(jnp.logical_and(k_pos >= 0, kc <= qc), kc >= qc - LEFT_CHUNKS)
    _band_heads(q_ref, [_head_cols(r) for r in k_refs], [_head_cols(r) for r in v_refs], bias_ref, ok, o_ref, HB)


def _band_prompt(qkv, bias, out_buf, *, n_blocks, HB, TQ, NP):
    W = HB * HEAD_DIM
    n_hg = H_CHK // HB

    def kv_spec(r, third):
        return pl.BlockSpec((TQ, W), lambda hg, j: (jnp.maximum(j - (NP - 1) + r, 0), third * n_hg + hg))

    return pl.pallas_call(
        functools.partial(_band_prompt_body, HB=HB, TQ=TQ, NP=NP),
        grid=(n_hg, n_blocks),
        in_specs=([pl.BlockSpec((TQ, W), lambda hg, j: (j, hg))]
                  + [kv_spec(r, 1) for r in range(NP)] + [kv_spec(r, 2) for r in range(NP)]
                  + [pl.BlockSpec((HB, TQ, NP * TQ), lambda hg, j: (hg, 0, 0)), pl.BlockSpec(memory_space=pl.ANY)]),
        out_specs=pl.BlockSpec((TQ, W), lambda hg, j: (j, hg)),
        out_shape=jax.ShapeDtypeStruct(out_buf.shape, out_buf.dtype),
        input_output_aliases={2 * NP + 2: 0},
        compiler_params=_params("parallel", "parallel"),
        name="band_prompt",
    )(qkv, *([qkv] * (2 * NP)), bias, out_buf)


def _band_sample_body(q_ref, kc_ref, kn_ref, vc_ref, vn_ref, bias_ref, _, o_ref, *, HB):
    _band_heads(q_ref, [_head_cols(kc_ref), _head_cols(kn_ref)], [_head_cols(vc_ref), _head_cols(vn_ref)], bias_ref,
                None, o_ref, HB)


def _band_sample(qkv, cache_k, cache_v, bias, out_buf, *, n_seq, T, row0, HB):
    P = cache_k.shape[1]
    W = HB * HEAD_DIM
    n_hg = H_CHK // HB
    rb0 = row0 // T
    new = lambda third: pl.BlockSpec((T, W), lambda b, hg: (rb0 + b, third * n_hg + hg))
    cache = pl.BlockSpec((None, P, W), lambda b, hg: (b, 0, hg))
    return pl.pallas_call(
        functools.partial(_band_sample_body, HB=HB),
        grid=(n_seq, n_hg),
        in_specs=[new(0), cache, new(1), cache, new(2),
                  pl.BlockSpec((HB, T, P + T), lambda b, hg: (hg, 0, 0)), pl.BlockSpec(memory_space=pl.ANY)],
        out_specs=pl.BlockSpec((T, W), lambda b, hg: (rb0 + b, hg)),
        out_shape=jax.ShapeDtypeStruct(out_buf.shape, out_buf.dtype),
        input_output_aliases={6: 0},
        compiler_params=_params("parallel", "parallel"),
        name="band_sample",
    )(qkv, cache_k, qkv, cache_v, qkv, bias, out_buf)


def _rot_tables(pos, head_dim, rot_dim, theta):
    half = rot_dim // 2
    inv = theta ** (-jnp.arange(half, dtype=F32) / half)
    ang = pos.astype(F32)[:, None] * inv[None, :]
    cos, sin = jnp.cos(ang), jnp.sin(ang)
    m = pos.shape[0]
    zh = jnp.zeros((m, half), F32)
    rest0 = jnp.zeros((m, head_dim - rot_dim), F32)
    c = jnp.concatenate([cos, cos, jnp.ones((m, head_dim - rot_dim), F32)], 1)
    s_up = jnp.concatenate([-sin, zh, rest0], 1)
    s_down = jnp.concatenate([zh, sin, rest0], 1)
    reps = LANES // head_dim
    return tuple(jnp.tile(t, (1, reps)) for t in (c, s_up, s_down))


def _log_gamma_table(width):
    lg = jnp.log1p(-(2.0 ** (-5.0 - jnp.arange(H_RET, dtype=F32))))
    return jnp.broadcast_to(lg[:, None, None], (H_RET, 8, width))


def _routing(top_e, sb, rg, row_multiple):
    n = top_e.shape[0]
    a = n * TOP_K
    n_rows = -(-((-(-a // sb) + N_EXPERTS) * sb + rg) // row_multiple) * row_multiple
    n_groups = N_EXPERTS + a // rg
    flat_e = top_e.reshape(-1)
    onehot = (flat_e[:, None] == jnp.arange(N_EXPERTS, dtype=I32)[None, :]).astype(I32)
    rank = jnp.take_along_axis(jnp.cumsum(onehot, 0), flat_e[:, None], 1)[:, 0] - 1
    counts = jnp.sum(onehot, 0)
    padded = (counts + sb - 1) // sb * sb
    pad_end = jnp.cumsum(padded)
    pad_start = pad_end - padded
    slot = pad_start[flat_e] + rank
    slot_tok = jnp.zeros((n_rows,), I32).at[slot].set(jnp.arange(a, dtype=I32) // TOP_K)

    g_count = (padded + rg - 1) // rg
    g_end = jnp.cumsum(g_count)
    g_start = g_end - g_count
    s = jnp.arange(n_groups, dtype=I32)
    live = s < g_end[-1]
    expert = jnp.minimum(jnp.searchsorted(g_end, jnp.where(live, s, g_end[-1] - 1), side='right'), N_EXPERTS - 1)
    k = s - g_start[expert]
    nb = jnp.where(live, jnp.clip(padded[expert] - k * rg, 0, rg) // sb, 0)
    st = jnp.where(live, pad_start[expert] + k * rg, pad_end[-1]) // sb
    groups = _Groups(expert.astype(I32), st.astype(I32), nb.astype(I32), rg, sb, False)
    own_row = (g_start[flat_e] + rank // rg) * rg + rank % rg
    return own_row.reshape(n, TOP_K).astype(I32), slot_tok, groups


RG = 2080
TN = 512
TN_PAIR = 256
TN_DOWN = 2048
TK_DOWN = 1024
TR = 208
RG_MOE = 2560
SB_MOE = 64
RET_CHUNK = 256
DSA_TQ = 128
DSA_TQ_S = 128
DSA_TK = 512
GATHER_TB = 256
BAND_HB = 8
BAND_TQ = 128
BAND_PARTS = (LEFT_CHUNKS * CHUNK) // BAND_TQ + 1


def kernel(x_prompt, x_sample, p_prompt, p_sample, state_ret, cache_dsa_k, cache_dsa_v, cache_dsa_kidx,
           cache_chk_k, cache_chk_v, ln_mix_g, ln_mix_b, ln_ffn_g, ln_ffn_b, ple_proj, ple_gate,
           w_in_even, ret_gn_g, kidx_ln_g, kidx_ln_b, w_out_even, ffn_w_gate, ffn_w_up, ffn_w_down,
           w_in_odd, rel_bias, w_out_odd, router_w, exp_w_gate, exp_w_up, exp_w_down):
    seq, d = x_prompt.shape[1], x_prompt.shape[2]
    nb_s, t_s = x_sample.shape[0], x_sample.shape[1]
    n_s = nb_s * t_s
    m = seq + n_s
    past = cache_dsa_k.shape[2]
    ret_w = H_RET * HEAD_DIM
    dsa_w = H_DSA * HEAD_DIM
    kv_w = KVH_DSA * HEAD_DIM

    x, xb = _stack_rows(x_prompt[0], x_sample.reshape(n_s, d), tr=n_s)
    p = jnp.concatenate([p_prompt[:, 0], p_sample.reshape(DEPTH, n_s, -1)], 1).astype(BF16)
    pos = jnp.concatenate([jnp.arange(seq, dtype=I32), past + jnp.tile(jnp.arange(t_s, dtype=I32), nb_s)])
    tab_ret = _rot_tables(pos, HEAD_DIM, HEAD_DIM, RET_THETA)
    tab_dsa = _rot_tables(pos, HEAD_DIM, ROPE_DIM, ROPE_THETA)
    tab_idx = _rot_tables(pos, D_IDX, IDX_ROPE_DIM, ROPE_THETA)

    tokens = [_dense_groups(m, RG, lead=i) for i in range(DEPTH)]
    h = _rg_matmul(tokens[0], xb, w_in_even, tn=TN)
    c_dq = 4 * ret_w
    c_dk = c_dq + dsa_w
    c_dv = c_dk + kv_w
    c_iq = c_dv + kv_w
    c_ik = c_iq + H_IDX * D_IDX

    mix_in = jnp.zeros((m, ret_w + dsa_w), BF16)
    state0_p = jnp.zeros((1, H_RET, HEAD_DIM, HEAD_DIM), F32)
    mix_in, st_p = _retention(h, tab_ret, _log_gamma_table(RET_CHUNK), ret_gn_g[0], state0_p, mix_in,
                              C=RET_CHUNK, n_seq=1, n_chunks=seq // RET_CHUNK, row0=0)
    mix_in, st_s = _retention(h, tab_ret, _log_gamma_table(LANES), ret_gn_g[0], state_ret[0], mix_in,
                              C=t_s, n_seq=nb_s, n_chunks=1, row0=seq)

    (dq,) = _rotary_cols(h, c_dq // dsa_w, dsa_w, tab_dsa, ROPE_DIM // 2, [BF16], tr=TR)
    dk, dk_b = _rotary_cols(h, c_dk // kv_w, kv_w, tab_dsa, ROPE_DIM // 2, [F32, BF16], tr=TR)
    iq_lo, = _rotary_cols(h, c_iq // 1024, 1024, tab_idx, IDX_ROPE_DIM // 2, [BF16], tr=TR)
    iq_hi, = _rotary_cols(h, c_iq // 1024 + 1, 1024, tab_idx, IDX_ROPE_DIM // 2, [BF16], tr=TR)
    iq = jnp.concatenate([iq_lo, iq_hi], 1)
    ik, ik2, iw = _indexer_keys(h, c_ik // LANES, kidx_ln_g[0], kidx_ln_b[0], tab_idx, tr=TR)
    dv, dv_b = _copy_cols(h, c_dv // kv_w, kv_w, [F32, BF16], tr=TR)
    iwt = iw.T

    n_sel_p = min(TOPK_MAX, seq // 4)
    mix_in = _dsa(dq, iq, iwt, dk_b[:seq], dv_b[:seq], ik2[:seq], mix_in, TQ=DSA_TQ, TK=DSA_TK,
                  n_blocks=seq // DSA_TQ, n_sel=n_sel_p, q_pos0=0, q_stride=DSA_TQ, causal=True,
                  q_row0=0, out_row0=0, out_col0=ret_w, per_block_keys=False)

    def pad_queries(a):
        a = a[seq:].reshape(nb_s, t_s, -1)
        return jnp.pad(a, ((0, 0), (0, DSA_TQ_S - t_s), (0, 0))).reshape(nb_s * DSA_TQ_S, -1)

    def with_cache(cache, new):
        return jnp.concatenate([cache.reshape(nb_s, past, -1).astype(BF16), new[seq:].reshape(nb_s, t_s, -1)], 1)

    kidx_c = cache_dsa_kidx[0].astype(BF16)
    l_s = past + t_s
    do_s = _dsa(pad_queries(dq), pad_queries(iq), pad_queries(iw).T,
                with_cache(cache_dsa_k[0], dk_b), with_cache(cache_dsa_v[0], dv_b),
                with_cache(jnp.concatenate([kidx_c, kidx_c], -1), ik2),
                jnp.zeros((nb_s * DSA_TQ_S, dsa_w), BF16), TQ=DSA_TQ_S, TK=l_s, n_blocks=nb_s,
                n_sel=min(TOPK_MAX, l_s // 4), q_pos0=past, q_stride=0, causal=False,
                q_row0=0, out_row0=0, out_col0=0, per_block_keys=True)
    do_s = do_s.reshape(nb_s, DSA_TQ_S, dsa_w)[:, :t_s].reshape(n_s, dsa_w)
    mix_in = lax.dynamic_update_slice(mix_in, do_s, (seq, ret_w))

    mix = _rg_matmul(tokens[0], mix_in, w_out_even, tn=TN)
    x1, x1b = _residual_layer_norm(x, mix, ln_mix_g[0], ln_mix_b[0], tr=TR)
    act = _rg_swiglu(tokens[0], x1b, ffn_w_gate, ffn_w_up, tn=TN_PAIR)
    ffn = _rg_down(tokens[0], act, ffn_w_down, tn=TN_DOWN, tk=TK_DOWN)
    x2, x2b = _residual_layer_norm(x1, ffn, ln_ffn_g[0], ln_ffn_b[0], tr=TR)
    x3, x3b = _rg_ple(tokens[0], x2b, ple_gate, p[0], ple_proj, x2, tn=TN_PAIR)

    qkv = _rg_matmul(tokens[0], x3b, w_in_odd, tn=TN, out_dtype=BF16)
    chk_w = H_CHK * HEAD_DIM
    keep = min(LEFT_CHUNKS * CHUNK, seq)
    kv_new = _rg_matmul(_dense_groups(keep + n_s, keep + n_s), x3b[seq - keep:], w_in_odd, tn=TN,
                        stripes=(chk_w // TN, 2 * chk_w // TN))
    p_band = cache_chk_k.shape[2]
    bias_p = _expand_bias(rel_bias[0], BAND_TQ, BAND_PARTS * BAND_TQ, tn=2048)
    bias_s = _expand_bias(rel_bias[0], t_s, p_band + t_s, tn=(t_s * (p_band + t_s)) // 2)
    att = _band_prompt(qkv, bias_p, jnp.zeros((m, chk_w), BF16), n_blocks=seq // BAND_TQ, HB=BAND_HB, TQ=BAND_TQ,
                       NP=BAND_PARTS)
    att = _band_sample(qkv, cache_chk_k[0].reshape(nb_s, p_band, chk_w), cache_chk_v[0].reshape(nb_s, p_band, chk_w),
                       bias_s, att, n_seq=nb_s, T=t_s, row0=seq, HB=BAND_HB)
    mix = _rg_matmul(tokens[0], att, w_out_odd, tn=TN)
    x4, x4b = _residual_layer_norm(x3, mix, ln_mix_g[1], ln_mix_b[1], tr=TR)

    logits = _rg_matmul(tokens[0], x4b, router_w, tn=N_EXPERTS)
    top_e, gates = _router_top2(logits, tr=TR)
    tok_slot, slot_tok, experts = _routing(top_e, SB_MOE, RG_MOE, GATHER_TB)
    xs = _gather_rows(slot_tok, x4, tb=GATHER_TB)
    acts = _rg_swiglu(experts, xs, exp_w_gate[0], exp_w_up[0], tn=TN_PAIR)
    yb = _rg_down(experts, acts, exp_w_down[0], tn=TN_DOWN, tk=TK_DOWN)
    x5, x5b = _moe_combine_ln(tok_slot, yb, gates, x4, ln_ffn_g[1], ln_ffn_b[1], tb=TR)
    y, _ = _rg_ple(tokens[1], x5b, ple_gate, p[1], ple_proj, x5, tn=TN_PAIR)

    ck = kv_new[:, :chk_w]
    cv = kv_new[:, chk_w:]

    def heads(a, nh):
        return a.reshape(a.shape[0], nh, HEAD_DIM)

    return (y[:seq][None], y[seq:].reshape(nb_s, t_s, d),
            st_p.reshape(1, 1, H_RET, HEAD_DIM, HEAD_DIM), st_s.reshape(1, nb_s, H_RET, HEAD_DIM, HEAD_DIM),
            heads(dk[:seq], KVH_DSA)[None, None], heads(dv[:seq], KVH_DSA)[None, None], ik[:seq][None, None],
            heads(dk[seq:], KVH_DSA).reshape(1, nb_s, t_s, KVH_DSA, HEAD_DIM),
            heads(dv[seq:], KVH_DSA).reshape(1, nb_s, t_s, KVH_DSA, HEAD_DIM),
            ik[seq:].reshape(1, nb_s, t_s, D_IDX),
            heads(ck[:keep], H_CHK)[None, None], heads(cv[:keep], H_CHK)[None, None],
            heads(ck[keep:], H_CHK).reshape(1, nb_s, t_s, H_CHK, HEAD_DIM),
            heads(cv[keep:], H_CHK).reshape(1, nb_s, t_s, H_CHK, HEAD_DIM))
```

```python
import functools
import math

import numpy as np
import jax
import jax.numpy as jnp
from jax import lax
from jax.experimental import pallas as pl
from jax.experimental.pallas import tpu as pltpu

F32 = jnp.float32
BF16 = jnp.bfloat16
I32 = jnp.int32

CHUNK = 64
HEAD_DIM = 128
H_RET = 16
H_DSA = 16
KVH_DSA = 4
DSA_GROUP = H_DSA // KVH_DSA
H_IDX = 32
D_IDX = 64
TOPK_MAX = 256
H_CHK = 32
LEFT_CHUNKS = 8
REL_CLIP = 128
N_EXPERTS = 8
TOP_K = 2
RET_THETA = 10000.0
ROPE_THETA = 500000.0
ROPE_DIM = HEAD_DIM // 4
IDX_ROPE_DIM = D_IDX // 4
LN_EPS = 1e-5
DEPTH = 2
ALPHA = (2.0 * DEPTH) ** 0.25

LANES = 128
DMA_PRIORITIES = 2
MAX_PIECE = 8
VMEM_LIMIT = 56 * 1024 * 1024
NEG_BIG = -1e30
EXP2_SCALE = HEAD_DIM ** -0.5 * math.log2(math.e)
INT_MIN = -(2 ** 31)
INT_MAX = 2 ** 31 - 1


def _params(*sem):
    return pltpu.CompilerParams(dimension_semantics=sem, vmem_limit_bytes=VMEM_LIMIT)


class _Groups:
    def __init__(self, e, st, nb, rg, sb, dense):
        self.e, self.st, self.nb, self.rg, self.sb, self.dense = e, st, nb, rg, sb, dense
        self.n = e.shape[0]

    def rows(self, width, col, buffers=None, own=False):
        kw = {} if buffers is None else dict(pipeline_mode=pl.Buffered(buffers))
        if own:
            return pl.BlockSpec((self.rg, width), lambda s, *a: (s, col(a[-1][s] > 0, *a[:-3])), **kw)
        if self.dense:
            return pl.BlockSpec((self.rg, width), lambda s, *a: (a[-2][s], col(a[-1][s] > 0, *a[:-3])), **kw)
        unit = self.sb
        return pl.BlockSpec((pl.Element(self.rg), pl.Element(width)),
                            lambda s, *a: (a[-2][s] * unit, col(a[-1][s] > 0, *a[:-3]) * width), **kw)

    def weights(self, k_rows, width, kcol):
        return pl.BlockSpec((None, k_rows, width), lambda s, *a: (a[-3][s], *kcol(a[-1][s] > 0, *a[:-3])))


def _dense_groups(m, rg, lead=0):
    n = m // rg
    return _Groups(jnp.full((n,), lead, I32), jnp.arange(n, dtype=I32), jnp.ones((n,), I32), rg, rg, True)


def _live_rows(nb_ref, n_sub, sb, compute):
    nblk = nb_ref[pl.program_id(0)]
    if n_sub == 1:
        @pl.when(nblk > 0)
        def _():
            compute(pl.ds(0, sb))

        return nblk

    def run_full(r, c):
        rows = MAX_PIECE * sb
        compute(pl.ds(pl.multiple_of(r * rows, rows), rows))
        return c

    lax.fori_loop(0, nblk // MAX_PIECE, run_full, 0)
    piece = MAX_PIECE // 2
    while piece >= 1:
        def run_piece(piece=piece):
            start = (nblk // (2 * piece)) * (2 * piece * sb)
            compute(pl.ds(pl.multiple_of(start, 2 * piece * sb), piece * sb))

        pl.when((nblk // piece) % 2 == 1)(run_piece)
        piece //= 2
    return nblk


def _row_loops(nb_ref, n_sub, sb, compute, o_refs):
    nblk = _live_rows(nb_ref, n_sub, sb, compute)

    def clear(r, c):
        for o_ref in o_refs:
            o_ref[pl.ds(pl.multiple_of(r * sb, sb), sb), :] = jnp.zeros((sb, o_ref.shape[1]), o_ref.dtype)
        return c

    lax.fori_loop(nblk, n_sub, clear, 0)


def _bf16_dot(x, w_ref):
    return jnp.dot(x, w_ref[...].astype(BF16), preferred_element_type=F32)


def _rg_mm_body(e_ref, st_ref, nb_ref, x_ref, w_ref, o_ref, *, sb, w_transposed):
    def compute(rows):
        if w_transposed:
            y = lax.dot_general(x_ref[rows, :], w_ref[...].astype(BF16), (((1,), (1,)), ((), ())),
                                preferred_element_type=F32)
        else:
            y = _bf16_dot(x_ref[rows, :], w_ref)
        o_ref[rows, :] = y.astype(o_ref.dtype)

    _row_loops(nb_ref, x_ref.shape[0] // sb, sb, compute, [o_ref])


def _rg_swiglu_body(e_ref, st_ref, nb_ref, x_ref, wg_ref, wu_ref, o_ref, *, sb):
    def compute(rows):
        x = x_ref[rows, :]
        g = _bf16_dot(x, wg_ref)
        u = _bf16_dot(x, wu_ref)
        o_ref[rows, :] = (jax.nn.silu(g) * u).astype(o_ref.dtype)

    _row_loops(nb_ref, x_ref.shape[0] // sb, sb, compute, [o_ref])


def _rg_ple_body(e_ref, st_ref, nb_ref, x_ref, w_ref, p_ref, pw_ref, r_ref, of_ref, ob_ref, *, sb):
    def compute(rows):
        gate = _bf16_dot(x_ref[rows, :], w_ref)
        proj = _bf16_dot(p_ref[rows, :], pw_ref)
        y = r_ref[rows, :] + jax.nn.sigmoid(gate) * proj
        of_ref[rows, :] = y
        ob_ref[rows, :] = y.astype(BF16)

    _row_loops(nb_ref, x_ref.shape[0] // sb, sb, compute, [of_ref, ob_ref])


def _rg_down_body(e_ref, st_ref, nb_ref, a_ref, w_ref, o_ref, *, sb):
    @pl.when(pl.program_id(2) == 0)
    def _():
        o_ref[...] = jnp.zeros_like(o_ref)

    def compute(rows):
        o_ref[rows, :] += _bf16_dot(a_ref[rows, :], w_ref)

    _live_rows(nb_ref, a_ref.shape[0] // sb, sb, compute)


def _last_if_dead(n_blocks):
    return lambda live, j: jnp.where(live, j, n_blocks - 1)


def _rg_matmul(g, x, w, *, tn, out_dtype=F32, stripes=None, w_transposed=False):
    K = x.shape[1]
    n_w = w.shape[-2] if w_transposed else w.shape[-1]
    j0, J = (0, pl.cdiv(n_w, tn)) if stripes is None else stripes
    N = n_w if stripes is None else J * tn
    col = _last_if_dead(J)
    if w_transposed:
        w_spec = g.weights(tn, K, lambda live, j: (j0 + col(live, j), 0))
    else:
        w_spec = g.weights(K, tn, lambda live, j: (0, j0 + col(live, j)))
    return pl.pallas_call(
        functools.partial(_rg_mm_body, sb=g.sb, w_transposed=w_transposed),
        grid_spec=pltpu.PrefetchScalarGridSpec(
            num_scalar_prefetch=3, grid=(g.n, J),
            in_specs=[g.rows(K, lambda live, j: 0, buffers=1), w_spec],
            out_specs=g.rows(tn, lambda live, j: j, own=True)),
        out_shape=jax.ShapeDtypeStruct((g.n * g.rg, N), out_dtype),
        compiler_params=_params("arbitrary", "arbitrary"),
        name="rg_mm",
    )(g.e, g.st, g.nb, x, w)


def _rg_swiglu(g, x, wg, wu, *, tn):
    K = x.shape[1]
    N = wg.shape[-1]
    col = _last_if_dead(N // tn)
    wspec = g.weights(K, tn, lambda live, j: (0, col(live, j)))
    return pl.pallas_call(
        functools.partial(_rg_swiglu_body, sb=g.sb),
        grid_spec=pltpu.PrefetchScalarGridSpec(
            num_scalar_prefetch=3, grid=(g.n, N // tn),
            in_specs=[g.rows(K, lambda live, j: 0, buffers=1), wspec, wspec],
            out_specs=g.rows(tn, lambda live, j: j, own=True)),
        out_shape=jax.ShapeDtypeStruct((g.n * g.rg, N), BF16),
        compiler_params=_params("arbitrary", "arbitrary"),
        name="rg_swiglu",
    )(g.e, g.st, g.nb, x, wg, wu)


def _rg_ple(g, x, w, p, pw, resid, *, tn):
    K, KP = x.shape[1], p.shape[1]
    N = w.shape[-1]
    col = _last_if_dead(N // tn)
    wcol = lambda live, j: (0, col(live, j))
    tile = g.rows(tn, lambda live, j: j, own=True)
    return pl.pallas_call(
        functools.partial(_rg_ple_body, sb=g.sb),
        grid_spec=pltpu.PrefetchScalarGridSpec(
            num_scalar_prefetch=3, grid=(g.n, N // tn),
            in_specs=[g.rows(K, lambda live, j: 0, buffers=1), g.weights(K, tn, wcol),
                      g.rows(KP, lambda live, j: 0, buffers=1), g.weights(KP, tn, wcol), tile],
            out_specs=[tile, tile]),
        out_shape=[jax.ShapeDtypeStruct((g.n * g.rg, N), F32), jax.ShapeDtypeStruct((g.n * g.rg, N), BF16)],
        compiler_params=_params("arbitrary", "arbitrary"),
        name="rg_ple",
    )(g.e, g.st, g.nb, x, w, p, pw, resid)


def _rg_down(g, a, w, *, tn, tk):
    K = a.shape[1]
    N = w.shape[-1]
    NT, KC = N // tn, K // tk
    ncol = lambda live, n, kc: jnp.where(live, n, NT - 1)
    kcol = lambda live, n, kc: jnp.where(live, kc, KC - 1)
    return pl.pallas_call(
        functools.partial(_rg_down_body, sb=g.sb),
        grid_spec=pltpu.PrefetchScalarGridSpec(
            num_scalar_prefetch=3, grid=(g.n, NT, KC),
            in_specs=[g.rows(tk, kcol, own=True),
                      g.weights(tk, tn, lambda live, n, kc: (kcol(live, n, kc), ncol(live, n, kc)))],
            out_specs=g.rows(tn, lambda live, n, kc: n, buffers=1, own=True)),
        out_shape=jax.ShapeDtypeStruct((g.n * g.rg, N), F32),
        compiler_params=_params("arbitrary", "arbitrary", "arbitrary"),
        name="rg_down",
    )(g.e, g.st, g.nb, a, w)


def _layer_norm_rows(z, g, b):
    mu = jnp.mean(z, -1, keepdims=True)
    d = z - mu
    var = jnp.mean(d * d, -1, keepdims=True)
    return d * lax.rsqrt(var + LN_EPS) * g + b


def _ln_body(a_ref, b_ref, g_ref, beta_ref, of_ref, ob_ref):
    y = _layer_norm_rows(ALPHA * a_ref[...] + b_ref[...], g_ref[...], beta_ref[...])
    of_ref[...] = y
    ob_ref[...] = y.astype(BF16)


def _residual_layer_norm(a, b, g, beta, *, tr):
    M, D = a.shape
    row = pl.BlockSpec((tr, D), lambda i: (i, 0))
    vec = pl.BlockSpec((1, D), lambda i: (0, 0))
    return pl.pallas_call(
        _ln_body,
        grid=(M // tr,),
        in_specs=[row, row, vec, vec],
        out_specs=[row, row],
        out_shape=[jax.ShapeDtypeStruct((M, D), F32), jax.ShapeDtypeStruct((M, D), BF16)],
        compiler_params=_params("parallel"),
        name="res_ln",
    )(a, b, g.reshape(1, D), beta.reshape(1, D))


def _stack_body(a_ref, b_ref, of_ref, ob_ref, *, n_a):
    def emit(src_ref):
        y = src_ref[...]
        of_ref[...] = y
        ob_ref[...] = y.astype(BF16)

    pl.when(pl.program_id(0) < n_a)(functools.partial(emit, a_ref))
    pl.when(pl.program_id(0) >= n_a)(functools.partial(emit, b_ref))


def _stack_rows(a, b, *, tr):
    n_a, n_b = a.shape[0] // tr, b.shape[0] // tr
    D = a.shape[1]
    row = pl.BlockSpec((tr, D), lambda i: (i, 0))
    return pl.pallas_call(
        functools.partial(_stack_body, n_a=n_a),
        grid=(n_a + n_b,),
        in_specs=[pl.BlockSpec((tr, D), lambda i: (jnp.minimum(i, n_a - 1), 0)),
                  pl.BlockSpec((tr, D), lambda i: (jnp.maximum(i - n_a, 0), 0))],
        out_specs=[row, row],
        out_shape=[jax.ShapeDtypeStruct((a.shape[0] + b.shape[0], D), F32),
                   jax.ShapeDtypeStruct((a.shape[0] + b.shape[0], D), BF16)],
        compiler_params=_params("arbitrary"),
        name="stack_rows",
    )(a, b)


def _rotate(x, c, s_up, s_down, half):
    w = x.shape[1]
    reps = w // LANES
    if reps > 1:
        c, s_up, s_down = (jnp.concatenate([t] * reps, axis=1) for t in (c, s_up, s_down))
    return x * c + pltpu.roll(x, w - half, 1) * s_up + pltpu.roll(x, half, 1) * s_down


def _rot_body(x_ref, c_ref, su_ref, sd_ref, *o_refs, half):
    y = _rotate(x_ref[...], c_ref[...], su_ref[...], sd_ref[...], half)
    for o_ref in o_refs:
        o_ref[...] = y.astype(o_ref.dtype)


def _rotary_cols(h, col_block, width, tables, half, out_dtypes, *, tr):
    M = h.shape[0]
    tab = pl.BlockSpec((tr, LANES), lambda i: (i, 0))
    return pl.pallas_call(
        functools.partial(_rot_body, half=half),
        grid=(M // tr,),
        in_specs=[pl.BlockSpec((tr, width), lambda i: (i, col_block)), tab, tab, tab],
        out_specs=[pl.BlockSpec((tr, width), lambda i: (i, 0)) for _ in out_dtypes],
        out_shape=[jax.ShapeDtypeStruct((M, width), dt) for dt in out_dtypes],
        compiler_params=_params("parallel"),
        name="rotary",
    )(h, *tables)


def _copy_body(x_ref, *o_refs):
    for o_ref in o_refs:
        o_ref[...] = x_ref[...].astype(o_ref.dtype)


def _copy_cols(h, col_block, width, out_dtypes, *, tr):
    M = h.shape[0]
    return pl.pallas_call(
        _copy_body,
        grid=(M // tr,),
        in_specs=[pl.BlockSpec((tr, width), lambda i: (i, col_block))],
        out_specs=[pl.BlockSpec((tr, width), lambda i: (i, 0)) for _ in out_dtypes],
        out_shape=[jax.ShapeDtypeStruct((M, width), dt) for dt in out_dtypes],
        compiler_params=_params("parallel"),
        name="copy_cols",
    )(h)


def _ik_body(x_ref, g_ref, b_ref, c_ref, su_ref, sd_ref, ik_ref, ik2_ref, iw_ref):
    x = x_ref[...]
    is_key = lax.broadcasted_iota(I32, x.shape, 1) < D_IDX
    mu = jnp.sum(jnp.where(is_key, x, 0.0), -1, keepdims=True) / D_IDX
    d = jnp.where(is_key, x - mu, 0.0)
    var = jnp.sum(d * d, -1, keepdims=True) / D_IDX
    y = jnp.where(is_key, d * lax.rsqrt(var + LN_EPS) * g_ref[...] + b_ref[...], 0.0)
    y = _rotate(y, c_ref[...], su_ref[...], sd_ref[...], IDX_ROPE_DIM // 2)
    y = jnp.where(is_key, y, 0.0)
    ik_ref[...] = y[:, :D_IDX]
    ik2_ref[...] = (y + pltpu.roll(y, D_IDX, 1)).astype(BF16)
    iw_ref[...] = pltpu.roll(x, LANES - D_IDX, 1)[:, :H_IDX]


def _indexer_keys(h, col_block, g, b, tables, *, tr):
    M = h.shape[0]
    pad = LANES - D_IDX
    tab = pl.BlockSpec((tr, LANES), lambda i: (i, 0))
    vec = pl.BlockSpec((1, LANES), lambda i: (0, 0))
    return pl.pallas_call(
        _ik_body,
        grid=(M // tr,),
        in_specs=[pl.BlockSpec((tr, LANES), lambda i: (i, col_block)), vec, vec, tab, tab, tab],
        out_specs=[pl.BlockSpec((tr, D_IDX), lambda i: (i, 0)), pl.BlockSpec((tr, LANES), lambda i: (i, 0)),
                   pl.BlockSpec((tr, H_IDX), lambda i: (i, 0))],
        out_shape=[jax.ShapeDtypeStruct((M, D_IDX), F32), jax.ShapeDtypeStruct((M, LANES), BF16),
                   jax.ShapeDtypeStruct((M, H_IDX), F32)],
        compiler_params=_params("parallel"),
        name="indexer_keys",
    )(h, jnp.pad(g, (0, pad)).reshape(1, LANES), jnp.pad(b, (0, pad)).reshape(1, LANES), *tables)


def _router_body(l_ref, e_ref, g_ref):
    lg = l_ref[...]
    idx = lax.broadcasted_iota(I32, lg.shape, 1)
    m1 = jnp.max(lg, -1, keepdims=True)
    e1 = jnp.min(jnp.where(lg == m1, idx, N_EXPERTS), -1, keepdims=True)
    rest = jnp.where(idx == e1, -jnp.inf, lg)
    m2 = jnp.max(rest, -1, keepdims=True)
    e2 = jnp.min(jnp.where(rest == m2, idx, N_EXPERTS), -1, keepdims=True)
    ex2 = jnp.exp(m2 - m1)
    den = 1.0 + ex2
    e_ref[...] = jnp.concatenate([e1, e2], axis=1)
    g_ref[...] = jnp.concatenate([1.0 / den, ex2 / den], axis=1)


def _router_top2(logits, *, tr):
    M, E = logits.shape
    return pl.pallas_call(
        _router_body,
        grid=(M // tr,),
        in_specs=[pl.BlockSpec((tr, E), lambda i: (i, 0))],
        out_specs=[pl.BlockSpec((tr, TOP_K), lambda i: (i, 0)), pl.BlockSpec((tr, TOP_K), lambda i: (i, 0))],
        out_shape=[jax.ShapeDtypeStruct((M, TOP_K), I32), jax.ShapeDtypeStruct((M, TOP_K), F32)],
        compiler_params=_params("parallel"),
        name="router_top2",
    )(logits)


def _row_copy(src_hbm, row, dst, r, sem):
    return pltpu.make_async_copy(src_hbm.at[pl.ds(row, 1), :], dst.at[pl.ds(r, 1), :], sem)


def _gather_body(tok_ref, n_live_ref, x_hbm, o_ref, buf, sem, *, tb):
    base = pl.program_id(0) * tb

    def start(r2, _):
        for u in range(DMA_PRIORITIES):
            r = r2 * DMA_PRIORITIES + u
            _row_copy(x_hbm, tok_ref[base + r], buf, r, sem).start(priority=u)
        return 0

    def wait(r, _):
        _row_copy(x_hbm, 0, buf, r, sem).wait()
        return 0

    @pl.when(base < n_live_ref[0])
    def _():
        lax.fori_loop(0, tb // DMA_PRIORITIES, start, 0)
        lax.fori_loop(0, tb, wait, 0)
        o_ref[...] = buf[...].astype(BF16)

    @pl.when(base >= n_live_ref[0])
    def _():
        o_ref[...] = jnp.zeros_like(o_ref)


def _gather_rows(slot_tok, n_live, x, *, tb):
    A = slot_tok.shape[0]
    D = x.shape[1]
    return pl.pallas_call(
        functools.partial(_gather_body, tb=tb),
        grid_spec=pltpu.PrefetchScalarGridSpec(
            num_scalar_prefetch=2,
            grid=(A // tb,),
            in_specs=[pl.BlockSpec(memory_space=pl.ANY)],
            out_specs=pl.BlockSpec((tb, D), lambda i, tok, n_live: (i, 0)),
            scratch_shapes=[pltpu.VMEM((tb, D), F32), pltpu.SemaphoreType.DMA(())]),
        out_shape=jax.ShapeDtypeStruct((A, D), BF16),
        compiler_params=_params("arbitrary"),
        name="moe_gather",
    )(slot_tok, n_live, x)


def _combine_body(slot_ref, y_hbm, gate_ref, x_ref, g_ref, beta_ref, of_ref, ob_ref, buf, sem, *, tb):
    base = pl.program_id(0) * tb

    def start(r, _):
        for k in range(TOP_K):
            _row_copy(y_hbm, slot_ref[(base + r) * TOP_K + k], buf.at[k], r, sem).start(priority=k % DMA_PRIORITIES)
        return 0

    def wait(r, _):
        for k in range(TOP_K):
            _row_copy(y_hbm, 0, buf.at[k], r, sem).wait()
        return 0

    lax.fori_loop(0, tb, start, 0)
    lax.fori_loop(0, tb, wait, 0)
    gates = gate_ref[...]
    y = buf[0] * gates[:, 0:1] + buf[1] * gates[:, 1:2]
    z = _layer_norm_rows(ALPHA * x_ref[...] + y, g_ref[...], beta_ref[...])
    of_ref[...] = z
    ob_ref[...] = z.astype(BF16)


def _moe_combine_ln(tok_slot, yb, gates, x, g, beta, *, tb):
    M, D = x.shape
    row = pl.BlockSpec((tb, D), lambda i, s: (i, 0))
    vec = pl.BlockSpec((1, D), lambda i, s: (0, 0))
    return pl.pallas_call(
        functools.partial(_combine_body, tb=tb),
        grid_spec=pltpu.PrefetchScalarGridSpec(
            num_scalar_prefetch=1,
            grid=(M // tb,),
            in_specs=[pl.BlockSpec(memory_space=pl.ANY), pl.BlockSpec((tb, TOP_K), lambda i, s: (i, 0)),
                      row, vec, vec],
            out_specs=[row, row],
            scratch_shapes=[pltpu.VMEM((TOP_K, tb, D), F32), pltpu.SemaphoreType.DMA(())]),
        out_shape=[jax.ShapeDtypeStruct((M, D), F32), jax.ShapeDtypeStruct((M, D), BF16)],
        compiler_params=_params("arbitrary"),
        name="moe_combine",
    )(tok_slot.reshape(-1), yb, gates, x, g.reshape(1, D), beta.reshape(1, D))


def _ret_body(q_ref, k_ref, v_ref, gate_ref, c_ref, su_ref, sd_ref, lg_ref, gn_ref, s0_ref, _, o_ref, st_ref,
              s_scr, *, C, HR):
    @pl.when(pl.program_id(1) == 0)
    def _():
        s_scr[...] = s0_ref[...]

    heads = range(HR)
    cols = [slice(a * HEAD_DIM, (a + 1) * HEAD_DIM) for a in heads]
    c, su, sd = c_ref[...], su_ref[...], sd_ref[...]
    n_col = lax.broadcasted_iota(I32, (C, C), 0)
    m_row = lax.broadcasted_iota(I32, (C, C), 1)
    diff = (n_col - m_row).astype(F32)
    n_idx = lax.broadcasted_iota(I32, (C, LANES), 0).astype(F32)

    qb, k, vb, lg_lane = [], [], [], []
    for a in heads:
        qb.append(_rotate(q_ref[:, cols[a]], c, su, sd, HEAD_DIM // 2).astype(BF16))
        k.append(_rotate(k_ref[:, cols[a]], c, su, sd, HEAD_DIM // 2) * HEAD_DIM ** -0.5)
        vb.append(v_ref[:, cols[a]].astype(BF16))
        lg_lane.append(lg_ref[a, 0:1, :LANES])
    s = []
    for a in heads:
        intra = jnp.where(diff >= 0, jnp.exp(lg_ref[a, 0:1, :C] * jnp.maximum(diff, 0.0)), 0.0)
        s.append(lax.dot_general(qb[a], k[a].astype(BF16), (((1,), (1,)), ((), ())),
                                 preferred_element_type=F32) * intra)
    state = [s_scr[a] for a in heads]
    o = [jnp.dot(s[a].astype(BF16), vb[a], preferred_element_type=F32)
         + jnp.dot(qb[a], state[a].astype(BF16), preferred_element_type=F32) * jnp.exp(lg_lane[a] * (n_idx + 1.0))
         for a in heads]
    for a in heads:
        kd = (k[a] * jnp.exp(lg_lane[a] * (C - 1.0 - n_idx))).astype(BF16)
        new = state[a] * jnp.exp(lg_lane[a] * C) + lax.dot_general(kd, vb[a], (((0,), (0,)), ((), ())),
                                                                    preferred_element_type=F32)
        s_scr[a] = new
        st_ref[a] = new
    for a in heads:
        mu = jnp.mean(o[a], -1, keepdims=True)
        d = o[a] - mu
        var = jnp.mean(d * d, -1, keepdims=True)
        on = d * lax.rsqrt(var + LN_EPS) * gn_ref[:, cols[a]]
        o_ref[:, cols[a]] = (on * jax.nn.silu(gate_ref[:, cols[a]])).astype(o_ref.dtype)


def _retention(h, tables, lg_tab, gn_g, state0, out_buf, *, C, HR, n_seq, n_chunks, row0):
    rb0 = row0 // C
    n_hg = H_RET // HR
    W = HR * HEAD_DIM

    def rows(sg, c):
        return rb0 + (sg // n_hg) * n_chunks + c

    def hcol(which):
        return pl.BlockSpec((C, W), lambda sg, c, which=which: (rows(sg, c), which * n_hg + sg % n_hg))

    tab = pl.BlockSpec((C, LANES), lambda sg, c: (rows(sg, c), 0))
    state = pl.BlockSpec((HR, HEAD_DIM, HEAD_DIM), lambda sg, c: (sg, 0, 0))
    in_specs = [hcol(0), hcol(1), hcol(2), hcol(3), tab, tab, tab,
                pl.BlockSpec((HR, 8, lg_tab.shape[2]), lambda sg, c: (sg % n_hg, 0, 0)),
                pl.BlockSpec((1, W), lambda sg, c: (0, sg % n_hg)), state, pl.BlockSpec(memory_space=pl.ANY)]
    state_shape = (n_seq * H_RET, HEAD_DIM, HEAD_DIM)
    return pl.pallas_call(
        functools.partial(_ret_body, C=C, HR=HR),
        grid=(n_seq * n_hg, n_chunks),
        in_specs=in_specs,
        out_specs=[pl.BlockSpec((C, W), lambda sg, c: (rows(sg, c), sg % n_hg)), state],
        out_shape=[jax.ShapeDtypeStruct(out_buf.shape, out_buf.dtype), jax.ShapeDtypeStruct(state_shape, F32)],
        scratch_shapes=[pltpu.VMEM((HR, HEAD_DIM, HEAD_DIM), F32)],
        input_output_aliases={len(in_specs) - 1: 0},
        compiler_params=_params("arbitrary", "arbitrary"),
        name="retention",
    )(h, h, h, h, *tables, lg_tab, gn_g.reshape(1, -1), state0.reshape(state_shape), out_buf)


def _dsa_body(q_ref, iq_ref, iwt_ref, k_ref, v_ref, ik2_ref, *rest, TQ, TK, L, n_sel, q_pos0, q_stride, causal):
    o_ref, key_scr, iqm_scr, q4_scr, m_scr, l_scr, acc_scr = rest[-7:]
    G = DSA_GROUP
    i = pl.program_id(0)
    q_pos = q_pos0 + i * q_stride + lax.broadcasted_iota(I32, (1, TQ), 1)
    q_lim = ((q_pos >> 6) + 1) << 6
    if causal:
        n_tiles = jnp.minimum(((i + 1) * q_stride + TK - 1) // TK, L // TK)
    else:
        n_tiles = L // TK

    lane = lax.broadcasted_iota(I32, (TQ, LANES), 1)
    for j in range(H_IDX // 2):
        pair = iq_ref[:, j * LANES:(j + 1) * LANES]
        iqm_scr[j, :TQ, :] = jnp.where(lane < D_IDX, pair, jnp.zeros_like(pair))
        iqm_scr[j, TQ:, :] = jnp.where(lane >= D_IDX, pair, jnp.zeros_like(pair))
    iwt = iwt_ref[...] * (H_IDX ** -0.5 * D_IDX ** -0.5)

    def key_pos(kt):
        return kt * TK + lax.broadcasted_iota(I32, (TK, TQ), 0)

    def score_tile(kt, _):
        ik2 = ik2_ref[pl.ds(kt * TK, TK), :]
        acc = jnp.zeros((TK, TQ), F32)
        for j in range(H_IDX // 2):
            a = lax.dot_general(ik2, iqm_scr[j], (((1,), (1,)), ((), ())), preferred_element_type=F32)
            acc = (acc + jnp.maximum(a[:, :TQ], 0.0) * iwt[2 * j:2 * j + 1, :]
                   + jnp.maximum(a[:, TQ:], 0.0) * iwt[2 * j + 1:2 * j + 2, :])
        acc = jnp.where(key_pos(kt) < q_lim, acc, -jnp.inf)
        bits = pltpu.bitcast(acc, I32)
        key_scr[pl.ds(kt * TK, TK), :] = bits ^ ((bits >> 31) & INT_MAX)
        return 0

    lax.fori_loop(0, n_tiles, score_tile, 0)

    def bisect(_, carry):
        lo, hi = carry
        mid = lo + lax.shift_right_logical(hi - lo, 1)

        def count_tile(kt, c8):
            ge = (key_scr[pl.ds(kt * TK, TK), :] >= mid).astype(I32)
            return c8 + jnp.sum(ge.reshape(TK // 8, 8, TQ), axis=0)

        c8 = lax.fori_loop(0, n_tiles, count_tile, jnp.zeros((8, TQ), I32))
        enough = jnp.sum(c8, axis=0, keepdims=True) >= n_sel
        return jnp.where(enough, mid, lo), jnp.where(enough, hi, mid)

    thr, _ = lax.fori_loop(0, 32, bisect, (jnp.full((1, TQ), INT_MIN, I32), jnp.full((1, TQ), INT_MAX, I32)))

    for kv in range(KVH_DSA):
        q4_scr[kv] = jnp.concatenate(
            [q_ref[:, (kv * G + g) * HEAD_DIM:(kv * G + g + 1) * HEAD_DIM] for g in range(G)], axis=0)
    m_scr[...] = jnp.full_like(m_scr, NEG_BIG)
    l_scr[...] = jnp.zeros_like(l_scr)
    acc_scr[...] = jnp.zeros_like(acc_scr)

    def attend_tile(kt, _):
        sel = jnp.logical_and(key_scr[pl.ds(kt * TK, TK), :] >= thr, key_pos(kt) < q_lim)
        sel4 = jnp.concatenate([sel] * G, axis=1)
        rows = pl.ds(kt * TK, TK)
        heads = range(KVH_DSA)
        head_cols = [slice(kv * HEAD_DIM, (kv + 1) * HEAD_DIM) for kv in heads]
        logits = [lax.dot_general(k_ref[rows, head_cols[kv]], q4_scr[kv], (((1,), (1,)), ((), ())),
                                  preferred_element_type=F32) for kv in heads]
        probs, alphas = [], []
        for kv in heads:
            lg = jnp.where(sel4, logits[kv], NEG_BIG)
            m_old = m_scr[kv]
            m_new = jnp.maximum(m_old, jnp.max(lg, axis=0, keepdims=True))
            alpha = jnp.exp2((m_old - m_new) * EXP2_SCALE)
            p = jnp.exp2((lg - m_new) * EXP2_SCALE)
            l_scr[kv] = alpha * l_scr[kv] + jnp.sum(p, axis=0, keepdims=True)
            m_scr[kv] = m_new
            probs.append(p.astype(BF16))
            alphas.append(alpha)
        for kv in heads:
            pv = lax.dot_general(v_ref[rows, head_cols[kv]], probs[kv], (((0,), (0,)), ((), ())),
                                 preferred_element_type=F32)
            acc_scr[kv] = acc_scr[kv] * alphas[kv] + pv
        return 0

    lax.fori_loop(0, n_tiles, attend_tile, 0)

    for kv in range(KVH_DSA):
        o_t = acc_scr[kv] / l_scr[kv]
        for g in range(G):
            hd = kv * G + g
            o_ref[:, hd * HEAD_DIM:(hd + 1) * HEAD_DIM] = o_t[:, g * TQ:(g + 1) * TQ].T.astype(o_ref.dtype)


def _dsa(q, iq, iwt, k, v, ik2, out_buf, *, TQ, TK, n_blocks, n_sel, q_pos0, q_stride, causal,
         q_row0, out_row0, out_col0, per_block_keys):
    L = k.shape[-2]
    W = H_DSA * HEAD_DIM
    qb0, ob0, oc0 = q_row0 // TQ, out_row0 // TQ, out_col0 // W
    if per_block_keys:
        kspec = lambda a: pl.BlockSpec((None, L, a.shape[-1]), lambda i: (i, 0, 0))
    else:
        kspec = lambda a: pl.BlockSpec((L, a.shape[-1]), lambda i: (0, 0), pipeline_mode=pl.Buffered(1))
    G = DSA_GROUP
    return pl.pallas_call(
        functools.partial(_dsa_body, TQ=TQ, TK=TK, L=L, n_sel=n_sel, q_pos0=q_pos0, q_stride=q_stride,
                          causal=causal),
        grid=(n_blocks,),
        in_specs=[pl.BlockSpec((TQ, W), lambda i: (qb0 + i, 0)),
                  pl.BlockSpec((TQ, H_IDX * D_IDX), lambda i: (qb0 + i, 0)),
                  pl.BlockSpec((H_IDX, TQ), lambda i: (0, qb0 + i)),
                  kspec(k), kspec(v), kspec(ik2),
                  pl.BlockSpec(memory_space=pl.ANY)],
        out_specs=pl.BlockSpec((TQ, W), lambda i: (ob0 + i, oc0)),
        out_shape=jax.ShapeDtypeStruct(out_buf.shape, out_buf.dtype),
        scratch_shapes=[pltpu.VMEM((L, TQ), I32), pltpu.VMEM((H_IDX // 2, 2 * TQ, LANES), BF16),
                        pltpu.VMEM((KVH_DSA, G * TQ, HEAD_DIM), BF16),
                        pltpu.VMEM((KVH_DSA, 1, G * TQ), F32), pltpu.VMEM((KVH_DSA, 1, G * TQ), F32),
                        pltpu.VMEM((KVH_DSA, HEAD_DIM, G * TQ), F32)],
        input_output_aliases={6: 0},
        compiler_params=_params("arbitrary"),
        name="dsa",
    )(q, iq, iwt, k, v, ik2, out_buf)


def _bias_body(rb_ref, idx_ref, o_ref):
    n = rb_ref.shape[1]
    onehot = (lax.broadcasted_iota(I32, (n, idx_ref.shape[1]), 0) == idx_ref[...]).astype(F32)
    o_ref[...] = jnp.dot(rb_ref[...], onehot, preferred_element_type=F32, precision=lax.Precision.HIGHEST)


def _expand_bias(rel_bias, n_q, n_k, *, tn):
    H, R = rel_bias.shape
    RP = 3 * LANES
    t = np.arange(n_q)[:, None]
    s = np.arange(n_k)[None, :]
    idx = (np.clip(LEFT_CHUNKS * CHUNK + t - s, -REL_CLIP, REL_CLIP) + REL_CLIP).reshape(1, -1).astype(np.int32)
    out = pl.pallas_call(
        _bias_body,
        grid=(idx.shape[1] // tn,),
        in_specs=[pl.BlockSpec((H, RP), lambda j: (0, 0)), pl.BlockSpec((1, tn), lambda j: (0, j))],
        out_specs=pl.BlockSpec((H, tn), lambda j: (0, j)),
        out_shape=jax.ShapeDtypeStruct((H, idx.shape[1]), F32),
        compiler_params=_params("parallel"),
        name="rel_bias_expand",
    )(jnp.pad(rel_bias, ((0, 0), (0, RP - R))), jnp.asarray(idx))
    return out.reshape(H, n_q, n_k)


def _head_cols(ref):
    return lambda hh: ref[:, hh * HEAD_DIM:(hh + 1) * HEAD_DIM]


def _band_heads(q_ref, k_parts, v_parts, bias_ref, ok, o_ref, HB):
    def window(parts, hh):
        return jnp.concatenate([part(hh).astype(BF16) for part in parts], axis=0)

    q_head = _head_cols(q_ref)
    logits = [lax.dot_general(q_head(hh).astype(BF16), window(k_parts, hh), (((1,), (1,)), ((), ())),
                              preferred_element_type=F32) for hh in range(HB)]
    probs = []
    for hh in range(HB):
        s = logits[hh] * HEAD_DIM ** -0.5 + bias_ref[hh]
        if ok is not None:
            s = jnp.where(ok, s, NEG_BIG)
        p = jnp.exp(s - jnp.max(s, -1, keepdims=True))
        probs.append((p / jnp.sum(p, -1, keepdims=True)).astype(BF16))
    for hh in range(HB):
        o_ref[:, hh * HEAD_DIM:(hh + 1) * HEAD_DIM] = jnp.dot(
            probs[hh], window(v_parts, hh), preferred_element_type=F32).astype(o_ref.dtype)


def _band_prompt_body(q_ref, *rest, HB, TQ, NP):
    k_refs, v_refs = rest[:NP], rest[NP:2 * NP]
    bias_ref, o_ref = rest[2 * NP], rest[2 * NP + 2]
    j = pl.program_id(1)
    nk = NP * TQ
    q_pos = j * TQ + lax.broadcasted_iota(I32, (TQ, nk), 0)
    k_pos = (j - (NP - 1)) * TQ + lax.broadcasted_iota(I32, (TQ, nk), 1)
    qc, kc = q_pos >> 6, k_pos >> 6
    ok = jnp.logical_and(jnp.logical_and(k_pos >= 0, kc <= qc), kc >= qc - LEFT_CHUNKS)
    _band_heads(q_ref, [_head_cols(r) for r in k_refs], [_head_cols(r) for r in v_refs], bias_ref, ok, o_ref, HB)


def _band_prompt(qkv, bias, out_buf, *, n_blocks, HB, TQ, NP):
    W = HB * HEAD_DIM
    n_hg = H_CHK // HB

    def kv_spec(r, third):
        return pl.BlockSpec((TQ, W), lambda hg, j: (jnp.maximum(j - (NP - 1) + r, 0), third * n_hg + hg))

    return pl.pallas_call(
        functools.partial(_band_prompt_body, HB=HB, TQ=TQ, NP=NP),
        grid=(n_hg, n_blocks),
        in_specs=([pl.BlockSpec((TQ, W), lambda hg, j: (j, hg))]
                  + [kv_spec(r, 1) for r in range(NP)] + [kv_spec(r, 2) for r in range(NP)]
                  + [pl.BlockSpec((HB, TQ, NP * TQ), lambda hg, j: (hg, 0, 0)), pl.BlockSpec(memory_space=pl.ANY)]),
        out_specs=pl.BlockSpec((TQ, W), lambda hg, j: (j, hg)),
        out_shape=jax.ShapeDtypeStruct(out_buf.shape, out_buf.dtype),
        input_output_aliases={2 * NP + 2: 0},
        compiler_params=_params("parallel", "parallel"),
        name="band_prompt",
    )(qkv, *([qkv] * (2 * NP)), bias, out_buf)


def _band_sample_body(q_ref, kc_ref, kn_ref, vc_ref, vn_ref, bias_ref, _, o_ref, *, HB):
    _band_heads(q_ref, [_head_cols(kc_ref), _head_cols(kn_ref)], [_head_cols(vc_ref), _head_cols(vn_ref)], bias_ref,
                None, o_ref, HB)


def _band_sample(qkv, cache_k, cache_v, bias, out_buf, *, n_seq, T, row0, HB):
    P = cache_k.shape[1]
    W = HB * HEAD_DIM
    n_hg = H_CHK // HB
    rb0 = row0 // T
    new = lambda third: pl.BlockSpec((T, W), lambda b, hg: (rb0 + b, third * n_hg + hg))
    cache = pl.BlockSpec((None, P, W), lambda b, hg: (b, 0, hg))
    return pl.pallas_call(
        functools.partial(_band_sample_body, HB=HB),
        grid=(n_seq, n_hg),
        in_specs=[new(0), cache, new(1), cache, new(2),
                  pl.BlockSpec((HB, T, P + T), lambda b, hg: (hg, 0, 0)), pl.BlockSpec(memory_space=pl.ANY)],
        out_specs=pl.BlockSpec((T, W), lambda b, hg: (rb0 + b, hg)),
        out_shape=jax.ShapeDtypeStruct(out_buf.shape, out_buf.dtype),
        input_output_aliases={6: 0},
        compiler_params=_params("parallel", "parallel"),
        name="band_sample",
    )(qkv, cache_k, qkv, cache_v, qkv, bias, out_buf)


def _rot_tables(pos, head_dim, rot_dim, theta):
    half = rot_dim // 2
    inv = theta ** (-jnp.arange(half, dtype=F32) / half)
    ang = pos.astype(F32)[:, None] * inv[None, :]
    cos, sin = jnp.cos(ang), jnp.sin(ang)
    m = pos.shape[0]
    zh = jnp.zeros((m, half), F32)
    rest0 = jnp.zeros((m, head_dim - rot_dim), F32)
    c = jnp.concatenate([cos, cos, jnp.ones((m, head_dim - rot_dim), F32)], 1)
    s_up = jnp.concatenate([-sin, zh, rest0], 1)
    s_down = jnp.concatenate([zh, sin, rest0], 1)
    reps = LANES // head_dim
    return tuple(jnp.tile(t, (1, reps)) for t in (c, s_up, s_down))


def _log_gamma_table(width):
    lg = jnp.log1p(-(2.0 ** (-5.0 - jnp.arange(H_RET, dtype=F32))))
    return jnp.broadcast_to(lg[:, None, None], (H_RET, 8, width))


def _routing(top_e, sb, rg, row_multiple):
    n = top_e.shape[0]
    a = n * TOP_K
    n_rows = -(-((-(-a // sb) + N_EXPERTS) * sb + rg) // row_multiple) * row_multiple
    n_groups = N_EXPERTS + a // rg
    flat_e = top_e.reshape(-1)
    onehot = (flat_e[:, None] == jnp.arange(N_EXPERTS, dtype=I32)[None, :]).astype(I32)
    rank = jnp.take_along_axis(jnp.cumsum(onehot, 0), flat_e[:, None], 1)[:, 0] - 1
    counts = jnp.sum(onehot, 0)
    padded = (counts + sb - 1) // sb * sb
    pad_end = jnp.cumsum(padded)
    pad_start = pad_end - padded
    slot = pad_start[flat_e] + rank
    slot_tok = jnp.zeros((n_rows,), I32).at[slot].set(jnp.arange(a, dtype=I32) // TOP_K)

    g_count = (padded + rg - 1) // rg
    g_end = jnp.cumsum(g_count)
    g_start = g_end - g_count
    s = jnp.arange(n_groups, dtype=I32)
    live = s < g_end[-1]
    expert = jnp.minimum(jnp.searchsorted(g_end, jnp.where(live, s, g_end[-1] - 1), side='right'), N_EXPERTS - 1)
    k = s - g_start[expert]
    nb = jnp.where(live, jnp.clip(padded[expert] - k * rg, 0, rg) // sb, 0)
    st = jnp.where(live, pad_start[expert] + k * rg, pad_end[-1]) // sb
    groups = _Groups(expert.astype(I32), st.astype(I32), nb.astype(I32), rg, sb, False)
    own_row = (g_start[flat_e] + rank // rg) * rg + rank % rg
    return own_row.reshape(n, TOP_K).astype(I32), slot_tok, pad_end[-1:].astype(I32), groups


RG = 2080
TN = 512
TN_PAIR = 256
TN_DOWN = 2048
TK_DOWN = 1024
TR = 208
RG_MOE = 2560
SB_MOE = 64
RET_CHUNK = 256
RET_HR = 8
DSA_TQ = 128
DSA_TQ_S = 128
DSA_TK = 512
GATHER_TB = 256
BAND_HB = 8
BAND_TQ = 128
BAND_PARTS = (LEFT_CHUNKS * CHUNK) // BAND_TQ + 1


def kernel(x_prompt, x_sample, p_prompt, p_sample, state_ret, cache_dsa_k, cache_dsa_v, cache_dsa_kidx,
           cache_chk_k, cache_chk_v, ln_mix_g, ln_mix_b, ln_ffn_g, ln_ffn_b, ple_proj, ple_gate,
           w_in_even, ret_gn_g, kidx_ln_g, kidx_ln_b, w_out_even, ffn_w_gate, ffn_w_up, ffn_w_down,
           w_in_odd, rel_bias, w_out_odd, router_w, exp_w_gate, exp_w_up, exp_w_down):
    seq, d = x_prompt.shape[1], x_prompt.shape[2]
    nb_s, t_s = x_sample.shape[0], x_sample.shape[1]
    n_s = nb_s * t_s
    m = seq + n_s
    past = cache_dsa_k.shape[2]
    ret_w = H_RET * HEAD_DIM
    dsa_w = H_DSA * HEAD_DIM
    kv_w = KVH_DSA * HEAD_DIM

    x, xb = _stack_rows(x_prompt[0], x_sample.reshape(n_s, d), tr=n_s)
    p = jnp.concatenate([p_prompt[:, 0], p_sample.reshape(DEPTH, n_s, -1)], 1).astype(BF16)
    pos = jnp.concatenate([jnp.arange(seq, dtype=I32), past + jnp.tile(jnp.arange(t_s, dtype=I32), nb_s)])
    tab_ret = _rot_tables(pos, HEAD_DIM, HEAD_DIM, RET_THETA)
    tab_dsa = _rot_tables(pos, HEAD_DIM, ROPE_DIM, ROPE_THETA)
    tab_idx = _rot_tables(pos, D_IDX, IDX_ROPE_DIM, ROPE_THETA)

    tokens = [_dense_groups(m, RG, lead=i) for i in range(DEPTH)]
    h =_rg_matmul(tokens[0], xb, jnp.swapaxes(w_in_even, 1, 2), tn=TN, w_transposed=True)
    c_dq = 4 * ret_w
    c_dk = c_dq + dsa_w
    c_dv = c_dk + kv_w
    c_iq = c_dv + kv_w
    c_ik = c_iq + H_IDX * D_IDX

    mix_in = jnp.zeros((m, ret_w + dsa_w), BF16)
    state0_p = jnp.zeros((1, H_RET, HEAD_DIM, HEAD_DIM), F32)
    mix_in, st_p = _retention(h, tab_ret, _log_gamma_table(RET_CHUNK), ret_gn_g[0], state0_p, mix_in,
                              C=RET_CHUNK, HR=RET_HR, n_seq=1, n_chunks=seq // RET_CHUNK, row0=0)
    mix_in, st_s = _retention(h, tab_ret, _log_gamma_table(LANES), ret_gn_g[0], state_ret[0], mix_in,
                              C=t_s, HR=RET_HR, n_seq=nb_s, n_chunks=1, row0=seq)

    (dq,) = _rotary_cols(h, c_dq // dsa_w, dsa_w, tab_dsa, ROPE_DIM // 2, [BF16], tr=TR)
    dk, dk_b = _rotary_cols(h, c_dk // kv_w, kv_w, tab_dsa, ROPE_DIM // 2, [F32, BF16], tr=TR)
    iq_lo, = _rotary_cols(h, c_iq // 1024, 1024, tab_idx, IDX_ROPE_DIM // 2, [BF16], tr=TR)
    iq_hi, = _rotary_cols(h, c_iq // 1024 + 1, 1024, tab_idx, IDX_ROPE_DIM // 2, [BF16], tr=TR)
    iq = jnp.concatenate([iq_lo, iq_hi], 1)
    ik, ik2, iw = _indexer_keys(h, c_ik // LANES, kidx_ln_g[0], kidx_ln_b[0], tab_idx, tr=TR)
    dv, dv_b = _copy_cols(h, c_dv // kv_w, kv_w, [F32, BF16], tr=TR)
    iwt = iw.T

    n_sel_p = min(TOPK_MAX, seq // 4)
    mix_in = _dsa(dq, iq, iwt, dk_b[:seq], dv_b[:seq], ik2[:seq], mix_in, TQ=DSA_TQ, TK=DSA_TK,
                  n_blocks=seq // DSA_TQ, n_sel=n_sel_p, q_pos0=0, q_stride=DSA_TQ, causal=True,
                  q_row0=0, out_row0=0, out_col0=ret_w, per_block_keys=False)

    def pad_queries(a):
        a = a[seq:].reshape(nb_s, t_s, -1)
        return jnp.pad(a, ((0, 0), (0, DSA_TQ_S - t_s), (0, 0))).reshape(nb_s * DSA_TQ_S, -1)

    def with_cache(cache, new):
        return jnp.concatenate([cache.reshape(nb_s, past, -1).astype(BF16), new[seq:].reshape(nb_s, t_s, -1)], 1)

    kidx_c = cache_dsa_kidx[0].astype(BF16)
    l_s = past + t_s
    do_s = _dsa(pad_queries(dq), pad_queries(iq), pad_queries(iw).T,
                with_cache(cache_dsa_k[0], dk_b), with_cache(cache_dsa_v[0], dv_b),
                with_cache(jnp.concatenate([kidx_c, kidx_c], -1), ik2),
                jnp.zeros((nb_s * DSA_TQ_S, dsa_w), BF16), TQ=DSA_TQ_S, TK=l_s, n_blocks=nb_s,
                n_sel=min(TOPK_MAX, l_s // 4), q_pos0=past, q_stride=0, causal=False,
                q_row0=0, out_row0=0, out_col0=0, per_block_keys=True)
    do_s = do_s.reshape(nb_s, DSA_TQ_S, dsa_w)[:, :t_s].reshape(n_s, dsa_w)
    mix_in = lax.dynamic_update_slice(mix_in, do_s, (seq, ret_w))

    mix = _rg_matmul(tokens[0], mix_in, w_out_even, tn=TN)
    x1, x1b = _residual_layer_norm(x, mix, ln_mix_g[0], ln_mix_b[0], tr=TR)
    act = _rg_swiglu(tokens[0], x1b, ffn_w_gate, ffn_w_up, tn=TN_PAIR)
    ffn = _rg_down(tokens[0], act, ffn_w_down, tn=TN_DOWN, tk=TK_DOWN)
    x2, x2b = _residual_layer_norm(x1, ffn, ln_ffn_g[0], ln_ffn_b[0], tr=TR)
    x3, x3b = _rg_ple(tokens[0], x2b, ple_gate, p[0], ple_proj, x2, tn=TN_PAIR)

    qkv = _rg_matmul(tokens[0], x3b, w_in_odd, tn=TN, out_dtype=BF16)
    chk_w = H_CHK * HEAD_DIM
    keep = min(LEFT_CHUNKS * CHUNK, seq)
    kv_new = _rg_matmul(_dense_groups(keep + n_s, keep + n_s), x3b[seq - keep:], w_in_odd, tn=TN,
                        stripes=(chk_w // TN, 2 * chk_w // TN))
    p_band = cache_chk_k.shape[2]
    bias_p = _expand_bias(rel_bias[0], BAND_TQ, BAND_PARTS * BAND_TQ, tn=2048)
    bias_s = _expand_bias(rel_bias[0], t_s, p_band + t_s, tn=(t_s * (p_band + t_s)) // 2)
    att = _band_prompt(qkv, bias_p, jnp.zeros((m, chk_w), BF16), n_blocks=seq // BAND_TQ, HB=BAND_HB, TQ=BAND_TQ,
                       NP=BAND_PARTS)
    att = _band_sample(qkv, cache_chk_k[0].reshape(nb_s, p_band, chk_w), cache_chk_v[0].reshape(nb_s, p_band, chk_w),
                       bias_s, att, n_seq=nb_s, T=t_s, row0=seq, HB=BAND_HB)
    mix = _rg_matmul(tokens[0], att, w_out_odd, tn=TN)
    x4, x4b = _residual_layer_norm(x3, mix, ln_mix_g[1], ln_mix_b[1], tr=TR)

    logits = _rg_matmul(tokens[0], x4b, router_w, tn=N_EXPERTS)
    top_e, gates = _router_top2(logits, tr=TR)
    tok_slot, slot_tok, n_slots, experts = _routing(top_e, SB_MOE, RG_MOE, GATHER_TB)
    xs = _gather_rows(slot_tok, n_slots, x4, tb=GATHER_TB)
    acts = _rg_swiglu(experts, xs, exp_w_gate[0], exp_w_up[0], tn=TN_PAIR)
    yb = _rg_down(experts, acts, exp_w_down[0], tn=TN_DOWN, tk=TK_DOWN)
    x5, x5b = _moe_combine_ln(tok_slot, yb, gates, x4, ln_ffn_g[1], ln_ffn_b[1], tb=TR)
    y, _ = _rg_ple(tokens[1], x5b, ple_gate, p[1], ple_proj, x5, tn=TN_PAIR)

    ck = kv_new[:, :chk_w]
    cv = kv_new[:, chk_w:]

    def heads(a, nh):
        return a.reshape(a.shape[0], nh, HEAD_DIM)

    return (y[:seq][None], y[seq:].reshape(nb_s, t_s, d),
            st_p.reshape(1, 1, H_RET, HEAD_DIM, HEAD_DIM), st_s.reshape(1, nb_s, H_RET, HEAD_DIM, HEAD_DIM),
            heads(dk[:seq], KVH_DSA)[None, None], heads(dv[:seq], KVH_DSA)[None, None], ik[:seq][None, None],
            heads(dk[seq:], KVH_DSA).reshape(1, nb_s, t_s, KVH_DSA, HEAD_DIM),
            heads(dv[seq:], KVH_DSA).reshape(1, nb_s, t_s, KVH_DSA, HEAD_DIM),
            ik[seq:].reshape(1, nb_s, t_s, D_IDX),
            heads(ck[:keep], H_CHK)[None, None], heads(cv[:keep], H_CHK)[None, None],
            heads(ck[keep:], H_CHK).reshape(1, nb_s, t_s, H_CHK, HEAD_DIM),
            heads(cv[keep:], H_CHK).reshape(1, nb_s, t_s, H_CHK, HEAD_DIM))
```

```python
import functools
import math

import numpy as np
import jax
import jax.numpy as jnp
from jax import lax
from jax.experimental import pallas as pl
from jax.experimental.pallas import tpu as pltpu

F32 = jnp.float32
BF16 = jnp.bfloat16
I32 = jnp.int32

CHUNK = 64
HEAD_DIM = 128
H_RET = 16
H_DSA = 16
KVH_DSA = 4
DSA_GROUP = H_DSA // KVH_DSA
H_IDX = 32
D_IDX = 64
TOPK_MAX = 256
H_CHK = 32
LEFT_CHUNKS = 8
REL_CLIP = 128
N_EXPERTS = 8
TOP_K = 2
RET_THETA = 10000.0
ROPE_THETA = 500000.0
ROPE_DIM = HEAD_DIM // 4
IDX_ROPE_DIM = D_IDX // 4
LN_EPS = 1e-5
DEPTH = 2
ALPHA = (2.0 * DEPTH) ** 0.25

LANES = 128
DMA_PRIORITIES = 2
MAX_PIECE = 8
VMEM_LIMIT = 56 * 1024 * 1024
NEG_BIG = -1e30
EXP2_SCALE = HEAD_DIM ** -0.5 * math.log2(math.e)
INT_MIN = -(2 ** 31)
INT_MAX = 2 ** 31 - 1


def _params(*sem):
    return pltpu.CompilerParams(dimension_semantics=sem, vmem_limit_bytes=VMEM_LIMIT)


class _Groups:
    def __init__(self, e, st, nb, rg, sb, dense, x_buffers=1):
        self.e, self.st, self.nb, self.rg, self.sb, self.dense, self.x_buffers = e, st, nb, rg, sb, dense, x_buffers
        self.n = e.shape[0]

    def rows(self, width, col, buffers=None, own=False):
        kw = {} if buffers is None else dict(pipeline_mode=pl.Buffered(buffers))
        if own:
            return pl.BlockSpec((self.rg, width), lambda s, *a: (s, col(a[-1][s] > 0, *a[:-3])), **kw)
        if self.dense:
            return pl.BlockSpec((self.rg, width), lambda s, *a: (a[-2][s], col(a[-1][s] > 0, *a[:-3])), **kw)
        unit = self.sb
        return pl.BlockSpec((pl.Element(self.rg), pl.Element(width)),
                            lambda s, *a: (a[-2][s] * unit, col(a[-1][s] > 0, *a[:-3]) * width), **kw)

    def weights(self, k_rows, width, kcol):
        return pl.BlockSpec((None, k_rows, width), lambda s, *a: (a[-3][s], *kcol(a[-1][s] > 0, *a[:-3])))


def _dense_groups(m, rg, lead=0, x_buffers=1):
    n = m // rg
    return _Groups(jnp.full((n,), lead, I32), jnp.arange(n, dtype=I32), jnp.ones((n,), I32), rg, rg, True, x_buffers)


def _live_rows(nb_ref, n_sub, sb, compute):
    nblk = nb_ref[pl.program_id(0)]
    if n_sub == 1:
        @pl.when(nblk > 0)
        def _():
            compute(pl.ds(0, sb))

        return nblk

    def run_full(r, c):
        rows = MAX_PIECE * sb
        compute(pl.ds(pl.multiple_of(r * rows, rows), rows))
        return c

    lax.fori_loop(0, nblk // MAX_PIECE, run_full, 0)
    piece = MAX_PIECE // 2
    while piece >= 1:
        def run_piece(piece=piece):
            start = (nblk // (2 * piece)) * (2 * piece * sb)
            compute(pl.ds(pl.multiple_of(start, 2 * piece * sb), piece * sb))

        pl.when((nblk // piece) % 2 == 1)(run_piece)
        piece //= 2
    return nblk


def _row_loops(nb_ref, n_sub, sb, compute, o_refs):
    nblk = _live_rows(nb_ref, n_sub, sb, compute)

    def clear(r, c):
        for o_ref in o_refs:
            o_ref[pl.ds(pl.multiple_of(r * sb, sb), sb), :] = jnp.zeros((sb, o_ref.shape[1]), o_ref.dtype)
        return c

    lax.fori_loop(nblk, n_sub, clear, 0)


def _bf16_dot(x, w_ref):
    return jnp.dot(x, w_ref[...].astype(BF16), preferred_element_type=F32)


def _rg_mm_body(e_ref, st_ref, nb_ref, x_ref, w_ref, o_ref, *, sb, w_transposed):
    def compute(rows):
        if w_transposed:
            y = lax.dot_general(x_ref[rows, :], w_ref[...].astype(BF16), (((1,), (1,)), ((), ())),
                                preferred_element_type=F32)
        else:
            y = _bf16_dot(x_ref[rows, :], w_ref)
        o_ref[rows, :] = y.astype(o_ref.dtype)

    _row_loops(nb_ref, x_ref.shape[0] // sb, sb, compute, [o_ref])


def _rg_swiglu_body(e_ref, st_ref, nb_ref, x_ref, wg_ref, wu_ref, o_ref, *, sb):
    def compute(rows):
        x = x_ref[rows, :]
        g = _bf16_dot(x, wg_ref)
        u = _bf16_dot(x, wu_ref)
        o_ref[rows, :] = (jax.nn.silu(g) * u).astype(o_ref.dtype)

    _row_loops(nb_ref, x_ref.shape[0] // sb, sb, compute, [o_ref])


def _rg_ple_body(e_ref, st_ref, nb_ref, x_ref, w_ref, p_ref, pw_ref, r_ref, of_ref, ob_ref, *, sb):
    def compute(rows):
        gate = _bf16_dot(x_ref[rows, :], w_ref)
        proj = _bf16_dot(p_ref[rows, :], pw_ref)
        y = r_ref[rows, :] + jax.nn.sigmoid(gate) * proj
        of_ref[rows, :] = y
        ob_ref[rows, :] = y.astype(BF16)

    _row_loops(nb_ref, x_ref.shape[0] // sb, sb, compute, [of_ref, ob_ref])


def _rg_down_body(e_ref, st_ref, nb_ref, a_ref, w_ref, o_ref, *, sb):
    @pl.when(pl.program_id(2) == 0)
    def _():
        o_ref[...] = jnp.zeros_like(o_ref)

    def compute(rows):
        o_ref[rows, :] += _bf16_dot(a_ref[rows, :], w_ref)

    _live_rows(nb_ref, a_ref.shape[0] // sb, sb, compute)


def _last_if_dead(n_blocks):
    return lambda live, j: jnp.where(live, j, n_blocks - 1)


def _rg_matmul(g, x, w, *, tn, out_dtype=F32, stripes=None, w_transposed=False):
    K = x.shape[1]
    n_w = w.shape[-2] if w_transposed else w.shape[-1]
    j0, J = (0, pl.cdiv(n_w, tn)) if stripes is None else stripes
    N = n_w if stripes is None else J * tn
    col = _last_if_dead(J)
    if w_transposed:
        w_spec = g.weights(tn, K, lambda live, j: (j0 + col(live, j), 0))
    else:
        w_spec = g.weights(K, tn, lambda live, j: (0, j0 + col(live, j)))
    return pl.pallas_call(
        functools.partial(_rg_mm_body, sb=g.sb, w_transposed=w_transposed),
        grid_spec=pltpu.PrefetchScalarGridSpec(
            num_scalar_prefetch=3, grid=(g.n, J),
            in_specs=[g.rows(K, lambda live, j: 0, buffers=g.x_buffers), w_spec],
            out_specs=g.rows(tn, lambda live, j: j, own=True)),
        out_shape=jax.ShapeDtypeStruct((g.n * g.rg, N), out_dtype),
        compiler_params=_params("arbitrary", "arbitrary"),
        name="rg_mm",
    )(g.e, g.st, g.nb, x, w)


def _rg_swiglu(g, x, wg, wu, *, tn):
    K = x.shape[1]
    N = wg.shape[-1]
    col = _last_if_dead(N // tn)
    wspec = g.weights(K, tn, lambda live, j: (0, col(live, j)))
    return pl.pallas_call(
        functools.partial(_rg_swiglu_body, sb=g.sb),
        grid_spec=pltpu.PrefetchScalarGridSpec(
            num_scalar_prefetch=3, grid=(g.n, N // tn),
            in_specs=[g.rows(K, lambda live, j: 0, buffers=g.x_buffers), wspec, wspec],
            out_specs=g.rows(tn, lambda live, j: j, own=True)),
        out_shape=jax.ShapeDtypeStruct((g.n * g.rg, N), BF16),
        compiler_params=_params("arbitrary", "arbitrary"),
        name="rg_swiglu",
    )(g.e, g.st, g.nb, x, wg, wu)


def _rg_ple(g, x, w, p, pw, resid, *, tn):
    K, KP = x.shape[1], p.shape[1]
    N = w.shape[-1]
    col = _last_if_dead(N // tn)
    wcol = lambda live, j: (0, col(live, j))
    tile = g.rows(tn, lambda live, j: j, own=True)
    return pl.pallas_call(
        functools.partial(_rg_ple_body, sb=g.sb),
        grid_spec=pltpu.PrefetchScalarGridSpec(
            num_scalar_prefetch=3, grid=(g.n, N // tn),
            in_specs=[g.rows(K, lambda live, j: 0, buffers=g.x_buffers), g.weights(K, tn, wcol),
                      g.rows(KP, lambda live, j: 0, buffers=g.x_buffers), g.weights(KP, tn, wcol), tile],
            out_specs=[tile, tile]),
        out_shape=[jax.ShapeDtypeStruct((g.n * g.rg, N), F32), jax.ShapeDtypeStruct((g.n * g.rg, N), BF16)],
        compiler_params=_params("arbitrary", "arbitrary"),
        name="rg_ple",
    )(g.e, g.st, g.nb, x, w, p, pw, resid)


def _rg_down(g, a, w, *, tn, tk):
    K = a.shape[1]
    N = w.shape[-1]
    NT, KC = N // tn, K // tk
    ncol = lambda live, n, kc: jnp.where(live, n, NT - 1)
    kcol = lambda live, n, kc: jnp.where(live, kc, KC - 1)
    return pl.pallas_call(
        functools.partial(_rg_down_body, sb=g.sb),
        grid_spec=pltpu.PrefetchScalarGridSpec(
            num_scalar_prefetch=3, grid=(g.n, NT, KC),
            in_specs=[g.rows(tk, kcol, own=True),
                      g.weights(tk, tn, lambda live, n, kc: (kcol(live, n, kc), ncol(live, n, kc)))],
            out_specs=g.rows(tn, lambda live, n, kc: n, buffers=1, own=True)),
        out_shape=jax.ShapeDtypeStruct((g.n * g.rg, N), F32),
        compiler_params=_params("arbitrary", "arbitrary", "arbitrary"),
        name="rg_down",
    )(g.e, g.st, g.nb, a, w)


def _layer_norm_rows(z, g, b):
    mu = jnp.mean(z, -1, keepdims=True)
    d = z - mu
    var = jnp.mean(d * d, -1, keepdims=True)
    return d * lax.rsqrt(var + LN_EPS) * g + b


def _ln_body(a_ref, b_ref, g_ref, beta_ref, of_ref, ob_ref):
    y = _layer_norm_rows(ALPHA * a_ref[...] + b_ref[...], g_ref[...], beta_ref[...])
    of_ref[...] = y
    ob_ref[...] = y.astype(BF16)


def _residual_layer_norm(a, b, g, beta, *, tr):
    M, D = a.shape
    row = pl.BlockSpec((tr, D), lambda i: (i, 0))
    vec = pl.BlockSpec((1, D), lambda i: (0, 0))
    return pl.pallas_call(
        _ln_body,
        grid=(M // tr,),
        in_specs=[row, row, vec, vec],
        out_specs=[row, row],
        out_shape=[jax.ShapeDtypeStruct((M, D), F32), jax.ShapeDtypeStruct((M, D), BF16)],
        compiler_params=_params("parallel"),
        name="res_ln",
    )(a, b, g.reshape(1, D), beta.reshape(1, D))


def _stack_body(a_ref, b_ref, of_ref, ob_ref, *, n_a):
    def emit(src_ref):
        y = src_ref[...]
        of_ref[...] = y
        ob_ref[...] = y.astype(BF16)

    pl.when(pl.program_id(0) < n_a)(functools.partial(emit, a_ref))
    pl.when(pl.program_id(0) >= n_a)(functools.partial(emit, b_ref))


def _stack_rows(a, b, *, tr):
    n_a, n_b = a.shape[0] // tr, b.shape[0] // tr
    D = a.shape[1]
    row = pl.BlockSpec((tr, D), lambda i: (i, 0))
    return pl.pallas_call(
        functools.partial(_stack_body, n_a=n_a),
        grid=(n_a + n_b,),
        in_specs=[pl.BlockSpec((tr, D), lambda i: (jnp.minimum(i, n_a - 1), 0)),
                  pl.BlockSpec((tr, D), lambda i: (jnp.maximum(i - n_a, 0), 0))],
        out_specs=[row, row],
        out_shape=[jax.ShapeDtypeStruct((a.shape[0] + b.shape[0], D), F32),
                   jax.ShapeDtypeStruct((a.shape[0] + b.shape[0], D), BF16)],
        compiler_params=_params("arbitrary"),
        name="stack_rows",
    )(a, b)


def _rotate(x, c, s_up, s_down, half):
    w = x.shape[1]
    reps = w // LANES
    if reps > 1:
        c, s_up, s_down = (jnp.concatenate([t] * reps, axis=1) for t in (c, s_up, s_down))
    return x * c + pltpu.roll(x, w - half, 1) * s_up + pltpu.roll(x, half, 1) * s_down


def _rot_body(x_ref, c_ref, su_ref, sd_ref, *o_refs, half):
    y = _rotate(x_ref[...], c_ref[...], su_ref[...], sd_ref[...], half)
    for o_ref in o_refs:
        o_ref[...] = y.astype(o_ref.dtype)


def _rotary_cols(h, col_block, width, tables, half, out_dtypes, *, tr):
    M = h.shape[0]
    tab = pl.BlockSpec((tr, LANES), lambda i: (i, 0))
    return pl.pallas_call(
        functools.partial(_rot_body, half=half),
        grid=(M // tr,),
        in_specs=[pl.BlockSpec((tr, width), lambda i: (i, col_block)), tab, tab, tab],
        out_specs=[pl.BlockSpec((tr, width), lambda i: (i, 0)) for _ in out_dtypes],
        out_shape=[jax.ShapeDtypeStruct((M, width), dt) for dt in out_dtypes],
        compiler_params=_params("parallel"),
        name="rotary",
    )(h, *tables)


def _copy_body(x_ref, *o_refs):
    for o_ref in o_refs:
        o_ref[...] = x_ref[...].astype(o_ref.dtype)


def _copy_cols(h, col_block, width, out_dtypes, *, tr):
    M = h.shape[0]
    return pl.pallas_call(
        _copy_body,
        grid=(M // tr,),
        in_specs=[pl.BlockSpec((tr, width), lambda i: (i, col_block))],
        out_specs=[pl.BlockSpec((tr, width), lambda i: (i, 0)) for _ in out_dtypes],
        out_shape=[jax.ShapeDtypeStruct((M, width), dt) for dt in out_dtypes],
        compiler_params=_params("parallel"),
        name="copy_cols",
    )(h)


def _ik_body(x_ref, g_ref, b_ref, c_ref, su_ref, sd_ref, ik_ref, ik2_ref, iw_ref):
    x = x_ref[...]
    is_key = lax.broadcasted_iota(I32, x.shape, 1) < D_IDX
    mu = jnp.sum(jnp.where(is_key, x, 0.0), -1, keepdims=True) / D_IDX
    d = jnp.where(is_key, x - mu, 0.0)
    var = jnp.sum(d * d, -1, keepdims=True) / D_IDX
    y = jnp.where(is_key, d * lax.rsqrt(var + LN_EPS) * g_ref[...] + b_ref[...], 0.0)
    y = _rotate(y, c_ref[...], su_ref[...], sd_ref[...], IDX_ROPE_DIM // 2)
    y = jnp.where(is_key, y, 0.0)
    ik_ref[...] = y[:, :D_IDX]
    ik2_ref[...] = (y + pltpu.roll(y, D_IDX, 1)).astype(BF16)
    iw_ref[...] = pltpu.roll(x, LANES - D_IDX, 1)[:, :H_IDX]


def _indexer_keys(h, col_block, g, b, tables, *, tr):
    M = h.shape[0]
    pad = LANES - D_IDX
    tab = pl.BlockSpec((tr, LANES), lambda i: (i, 0))
    vec = pl.BlockSpec((1, LANES), lambda i: (0, 0))
    return pl.pallas_call(
        _ik_body,
        grid=(M // tr,),
        in_specs=[pl.BlockSpec((tr, LANES), lambda i: (i, col_block)), vec, vec, tab, tab, tab],
        out_specs=[pl.BlockSpec((tr, D_IDX), lambda i: (i, 0)), pl.BlockSpec((tr, LANES), lambda i: (i, 0)),
                   pl.BlockSpec((tr, H_IDX), lambda i: (i, 0))],
        out_shape=[jax.ShapeDtypeStruct((M, D_IDX), F32), jax.ShapeDtypeStruct((M, LANES), BF16),
                   jax.ShapeDtypeStruct((M, H_IDX), F32)],
        compiler_params=_params("parallel"),
        name="indexer_keys",
    )(h, jnp.pad(g, (0, pad)).reshape(1, LANES), jnp.pad(b, (0, pad)).reshape(1, LANES), *tables)


def _router_body(l_ref, e_ref, g_ref):
    lg = l_ref[...]
    idx = lax.broadcasted_iota(I32, lg.shape, 1)
    m1 = jnp.max(lg, -1, keepdims=True)
    e1 = jnp.min(jnp.where(lg == m1, idx, N_EXPERTS), -1, keepdims=True)
    rest = jnp.where(idx == e1, -jnp.inf, lg)
    m2 = jnp.max(rest, -1, keepdims=True)
    e2 = jnp.min(jnp.where(rest == m2, idx, N_EXPERTS), -1, keepdims=True)
    ex2 = jnp.exp(m2 - m1)
    den = 1.0 + ex2
    e_ref[...] = jnp.concatenate([e1, e2], axis=1)
    g_ref[...] = jnp.concatenate([1.0 / den, ex2 / den], axis=1)


def _router_top2(logits, *, tr):
    M, E = logits.shape
    return pl.pallas_call(
        _router_body,
        grid=(M // tr,),
        in_specs=[pl.BlockSpec((tr, E), lambda i: (i, 0))],
        out_specs=[pl.BlockSpec((tr, TOP_K), lambda i: (i, 0)), pl.BlockSpec((tr, TOP_K), lambda i: (i, 0))],
        out_shape=[jax.ShapeDtypeStruct((M, TOP_K), I32), jax.ShapeDtypeStruct((M, TOP_K), F32)],
        compiler_params=_params("parallel"),
        name="router_top2",
    )(logits)


def _row_copy(src_hbm, row, dst, r, sem):
    return pltpu.make_async_copy(src_hbm.at[pl.ds(row, 1), :], dst.at[pl.ds(r, 1), :], sem)


def _gather_body(tok_ref, n_live_ref, x_hbm, o_ref, buf, sem, *, tb):
    base = pl.program_id(0) * tb

    def start(r2, _):
        for u in range(DMA_PRIORITIES):
            r = r2 * DMA_PRIORITIES + u
            _row_copy(x_hbm, tok_ref[base + r], buf, r, sem).start(priority=u)
        return 0

    def wait(r, _):
        _row_copy(x_hbm, 0, buf, r, sem).wait()
        return 0

    @pl.when(base < n_live_ref[0])
    def _():
        lax.fori_loop(0, tb // DMA_PRIORITIES, start, 0)
        lax.fori_loop(0, tb, wait, 0)
        o_ref[...] = buf[...].astype(BF16)

    @pl.when(base >= n_live_ref[0])
    def _():
        o_ref[...] = jnp.zeros_like(o_ref)


def _gather_rows(slot_tok, n_live, x, *, tb):
    A = slot_tok.shape[0]
    D = x.shape[1]
    return pl.pallas_call(
        functools.partial(_gather_body, tb=tb),
        grid_spec=pltpu.PrefetchScalarGridSpec(
            num_scalar_prefetch=2,
            grid=(A // tb,),
            in_specs=[pl.BlockSpec(memory_space=pl.ANY)],
            out_specs=pl.BlockSpec((tb, D), lambda i, tok, n_live: (i, 0)),
            scratch_shapes=[pltpu.VMEM((tb, D), F32), pltpu.SemaphoreType.DMA(())]),
        out_shape=jax.ShapeDtypeStruct((A, D), BF16),
        compiler_params=_params("arbitrary"),
        name="moe_gather",
    )(slot_tok, n_live, x)


def _combine_body(slot_ref, y_hbm, gate_ref, x_ref, g_ref, beta_ref, of_ref, ob_ref, buf, sem, *, tb):
    base = pl.program_id(0) * tb

    def start(r, _):
        for k in range(TOP_K):
            _row_copy(y_hbm, slot_ref[(base + r) * TOP_K + k], buf.at[k], r, sem).start(priority=k % DMA_PRIORITIES)
        return 0

    def wait(r, _):
        for k in range(TOP_K):
            _row_copy(y_hbm, 0, buf.at[k], r, sem).wait()
        return 0

    lax.fori_loop(0, tb, start, 0)
    lax.fori_loop(0, tb, wait, 0)
    gates = gate_ref[...]
    y = buf[0] * gates[:, 0:1] + buf[1] * gates[:, 1:2]
    z = _layer_norm_rows(ALPHA * x_ref[...] + y, g_ref[...], beta_ref[...])
    of_ref[...] = z
    ob_ref[...] = z.astype(BF16)


def _moe_combine_ln(tok_slot, yb, gates, x, g, beta, *, tb):
    M, D = x.shape
    row = pl.BlockSpec((tb, D), lambda i, s: (i, 0))
    vec = pl.BlockSpec((1, D), lambda i, s: (0, 0))
    return pl.pallas_call(
        functools.partial(_combine_body, tb=tb),
        grid_spec=pltpu.PrefetchScalarGridSpec(
            num_scalar_prefetch=1,
            grid=(M // tb,),
            in_specs=[pl.BlockSpec(memory_space=pl.ANY), pl.BlockSpec((tb, TOP_K), lambda i, s: (i, 0)),
                      row, vec, vec],
            out_specs=[row, row],
            scratch_shapes=[pltpu.VMEM((TOP_K, tb, D), F32), pltpu.SemaphoreType.DMA(())]),
        out_shape=[jax.ShapeDtypeStruct((M, D), F32), jax.ShapeDtypeStruct((M, D), BF16)],
        compiler_params=_params("arbitrary"),
        name="moe_combine",
    )(tok_slot.reshape(-1), yb, gates, x, g.reshape(1, D), beta.reshape(1, D))


def _ret_body(q_ref, k_ref, v_ref, gate_ref, c_ref, su_ref, sd_ref, lg_ref, gn_ref, s0_ref, _, o_ref, st_ref,
              s_scr, *, C, HR):
    @pl.when(pl.program_id(1) == 0)
    def _():
        s_scr[...] = s0_ref[...]

    heads = range(HR)
    cols = [slice(a * HEAD_DIM, (a + 1) * HEAD_DIM) for a in heads]
    c, su, sd = c_ref[...], su_ref[...], sd_ref[...]
    n_col = lax.broadcasted_iota(I32, (C, C), 0)
    m_row = lax.broadcasted_iota(I32, (C, C), 1)
    diff = (n_col - m_row).astype(F32)
    n_idx = lax.broadcasted_iota(I32, (C, LANES), 0).astype(F32)

    qb, k, vb, lg_lane = [], [], [], []
    for a in heads:
        qb.append(_rotate(q_ref[:, cols[a]], c, su, sd, HEAD_DIM // 2).astype(BF16))
        k.append(_rotate(k_ref[:, cols[a]], c, su, sd, HEAD_DIM // 2) * HEAD_DIM ** -0.5)
        vb.append(v_ref[:, cols[a]].astype(BF16))
        lg_lane.append(lg_ref[a, 0:1, :LANES])
    s = []
    for a in heads:
        intra = jnp.where(diff >= 0, jnp.exp(lg_ref[a, 0:1, :C] * jnp.maximum(diff, 0.0)), 0.0)
        s.append(lax.dot_general(qb[a], k[a].astype(BF16), (((1,), (1,)), ((), ())),
                                 preferred_element_type=F32) * intra)
    state = [s_scr[a] for a in heads]
    o = [jnp.dot(s[a].astype(BF16), vb[a], preferred_element_type=F32)
         + jnp.dot(qb[a], state[a].astype(BF16), preferred_element_type=F32) * jnp.exp(lg_lane[a] * (n_idx + 1.0))
         for a in heads]
    for a in heads:
        kd = (k[a] * jnp.exp(lg_lane[a] * (C - 1.0 - n_idx))).astype(BF16)
        new = state[a] * jnp.exp(lg_lane[a] * C) + lax.dot_general(kd, vb[a], (((0,), (0,)), ((), ())),
                                                                    preferred_element_type=F32)
        s_scr[a] = new
        st_ref[a] = new
    for a in heads:
        mu = jnp.mean(o[a], -1, keepdims=True)
        d = o[a] - mu
        var = jnp.mean(d * d, -1, keepdims=True)
        on = d * lax.rsqrt(var + LN_EPS) * gn_ref[:, cols[a]]
        o_ref[:, cols[a]] = (on * jax.nn.silu(gate_ref[:, cols[a]])).astype(o_ref.dtype)


def _retention(h, tables, lg_tab, gn_g, state0, out_buf, *, C, HR, n_seq, n_chunks, row0):
    rb0 = row0 // C
    n_hg = H_RET // HR
    W = HR * HEAD_DIM

    def rows(sg, c):
        return rb0 + (sg // n_hg) * n_chunks + c

    def hcol(which):
        return pl.BlockSpec((C, W), lambda sg, c, which=which: (rows(sg, c), which * n_hg + sg % n_hg))

    tab = pl.BlockSpec((C, LANES), lambda sg, c: (rows(sg, c), 0))
    state = pl.BlockSpec((HR, HEAD_DIM, HEAD_DIM), lambda sg, c: (sg, 0, 0))
    in_specs = [hcol(0), hcol(1), hcol(2), hcol(3), tab, tab, tab,
                pl.BlockSpec((HR, 8, lg_tab.shape[2]), lambda sg, c: (sg % n_hg, 0, 0)),
                pl.BlockSpec((1, W), lambda sg, c: (0, sg % n_hg)), state, pl.BlockSpec(memory_space=pl.ANY)]
    state_shape = (n_seq * H_RET, HEAD_DIM, HEAD_DIM)
    return pl.pallas_call(
        functools.partial(_ret_body, C=C, HR=HR),
        grid=(n_seq * n_hg, n_chunks),
        in_specs=in_specs,
        out_specs=[pl.BlockSpec((C, W), lambda sg, c: (rows(sg, c), sg % n_hg)), state],
        out_shape=[jax.ShapeDtypeStruct(out_buf.shape, out_buf.dtype), jax.ShapeDtypeStruct(state_shape, F32)],
        scratch_shapes=[pltpu.VMEM((HR, HEAD_DIM, HEAD_DIM), F32)],
        input_output_aliases={len(in_specs) - 1: 0},
        compiler_params=_params("arbitrary", "arbitrary"),
        name="retention",
    )(h, h, h, h, *tables, lg_tab, gn_g.reshape(1, -1), state0.reshape(state_shape), out_buf)


def _dsa_body(q_ref, iq_ref, iwt_ref, k_ref, v_ref, ik2_ref, *rest, TQ, TK, L, n_sel, q_pos0, q_stride, causal):
    o_ref, key_scr, iqm_scr, q4_scr, m_scr, l_scr, acc_scr = rest[-7:]
    G = DSA_GROUP
    i = pl.program_id(0)
    q_pos = q_pos0 + i * q_stride + lax.broadcasted_iota(I32, (1, TQ), 1)
    q_lim = ((q_pos >> 6) + 1) << 6
    if causal:
        n_tiles = jnp.minimum(((i + 1) * q_stride + TK - 1) // TK, L // TK)
    else:
        n_tiles = L // TK

    lane = lax.broadcasted_iota(I32, (TQ, LANES), 1)
    for j in range(H_IDX // 2):
        pair = iq_ref[:, j * LANES:(j + 1) * LANES]
        iqm_scr[j, :TQ, :] = jnp.where(lane < D_IDX, pair, jnp.zeros_like(pair))
        iqm_scr[j, TQ:, :] = jnp.where(lane >= D_IDX, pair, jnp.zeros_like(pair))
    iwt = iwt_ref[...] * (H_IDX ** -0.5 * D_IDX ** -0.5)

    def key_pos(kt):
        return kt * TK + lax.broadcasted_iota(I32, (TK, TQ), 0)

    def score_tile(kt, _):
        ik2 = ik2_ref[pl.ds(kt * TK, TK), :]
        acc = jnp.zeros((TK, TQ), F32)
        for j in range(H_IDX // 2):
            a = lax.dot_general(ik2, iqm_scr[j], (((1,), (1,)), ((), ())), preferred_element_type=F32)
            acc = (acc + jnp.maximum(a[:, :TQ], 0.0) * iwt[2 * j:2 * j + 1, :]
                   + jnp.maximum(a[:, TQ:], 0.0) * iwt[2 * j + 1:2 * j + 2, :])
        acc = jnp.where(key_pos(kt) < q_lim, acc, -jnp.inf)
        bits = pltpu.bitcast(acc, I32)
        key_scr[pl.ds(kt * TK, TK), :] = bits ^ ((bits >> 31) & INT_MAX)
        return 0

    lax.fori_loop(0, n_tiles, score_tile, 0)

    def bisect(_, carry):
        lo, hi = carry
        mid = lo + lax.shift_right_logical(hi - lo, 1)

        def count_tile(kt, c8):
            ge = (key_scr[pl.ds(kt * TK, TK), :] >= mid).astype(I32)
            return c8 + jnp.sum(ge.reshape(TK // 8, 8, TQ), axis=0)

        c8 = lax.fori_loop(0, n_tiles, count_tile, jnp.zeros((8, TQ), I32))
        enough = jnp.sum(c8, axis=0, keepdims=True) >= n_sel
        return jnp.where(enough, mid, lo), jnp.where(enough, hi, mid)

    thr, _ = lax.fori_loop(0, 32, bisect, (jnp.full((1, TQ), INT_MIN, I32), jnp.full((1, TQ), INT_MAX, I32)))

    for kv in range(KVH_DSA):
        q4_scr[kv] = jnp.concatenate(
            [q_ref[:, (kv * G + g) * HEAD_DIM:(kv * G + g + 1) * HEAD_DIM] for g in range(G)], axis=0)
    m_scr[...] = jnp.full_like(m_scr, NEG_BIG)
    l_scr[...] = jnp.zeros_like(l_scr)
    acc_scr[...] = jnp.zeros_like(acc_scr)

    def attend_tile(kt, _):
        sel = jnp.logical_and(key_scr[pl.ds(kt * TK, TK), :] >= thr, key_pos(kt) < q_lim)
        sel4 = jnp.concatenate([sel] * G, axis=1)
        rows = pl.ds(kt * TK, TK)
        heads = range(KVH_DSA)
        head_cols = [slice(kv * HEAD_DIM, (kv + 1) * HEAD_DIM) for kv in heads]
        logits = [lax.dot_general(k_ref[rows, head_cols[kv]], q4_scr[kv], (((1,), (1,)), ((), ())),
                                  preferred_element_type=F32) for kv in heads]
        probs, alphas = [], []
        for kv in heads:
            lg = jnp.where(sel4, logits[kv], NEG_BIG)
            m_old = m_scr[kv]
            m_new = jnp.maximum(m_old, jnp.max(lg, axis=0, keepdims=True))
            alpha = jnp.exp2((m_old - m_new) * EXP2_SCALE)
            p = jnp.exp2((lg - m_new) * EXP2_SCALE)
            l_scr[kv] = alpha * l_scr[kv] + jnp.sum(p, axis=0, keepdims=True)
            m_scr[kv] = m_new
            probs.append(p.astype(BF16))
            alphas.append(alpha)
        for kv in heads:
            pv = lax.dot_general(v_ref[rows, head_cols[kv]], probs[kv], (((0,), (0,)), ((), ())),
                                 preferred_element_type=F32)
            acc_scr[kv] = acc_scr[kv] * alphas[kv] + pv
        return 0

    lax.fori_loop(0, n_tiles, attend_tile, 0)

    for kv in range(KVH_DSA):
        o_t = acc_scr[kv] / l_scr[kv]
        for g in range(G):
            hd = kv * G + g
            o_ref[:, hd * HEAD_DIM:(hd + 1) * HEAD_DIM] = o_t[:, g * TQ:(g + 1) * TQ].T.astype(o_ref.dtype)


def _dsa(q, iq, iwt, k, v, ik2, out_buf, *, TQ, TK, n_blocks, n_sel, q_pos0, q_stride, causal,
         q_row0, out_row0, out_col0, per_block_keys):
    L = k.shape[-2]
    W = H_DSA * HEAD_DIM
    qb0, ob0, oc0 = q_row0 // TQ, out_row0 // TQ, out_col0 // W
    if per_block_keys:
        kspec = lambda a: pl.BlockSpec((None, L, a.shape[-1]), lambda i: (i, 0, 0))
    else:
        kspec = lambda a: pl.BlockSpec((L, a.shape[-1]), lambda i: (0, 0), pipeline_mode=pl.Buffered(1))
    G = DSA_GROUP
    return pl.pallas_call(
        functools.partial(_dsa_body, TQ=TQ, TK=TK, L=L, n_sel=n_sel, q_pos0=q_pos0, q_stride=q_stride,
                          causal=causal),
        grid=(n_blocks,),
        in_specs=[pl.BlockSpec((TQ, W), lambda i: (qb0 + i, 0)),
                  pl.BlockSpec((TQ, H_IDX * D_IDX), lambda i: (qb0 + i, 0)),
                  pl.BlockSpec((H_IDX, TQ), lambda i: (0, qb0 + i)),
                  kspec(k), kspec(v), kspec(ik2),
                  pl.BlockSpec(memory_space=pl.ANY)],
        out_specs=pl.BlockSpec((TQ, W), lambda i: (ob0 + i, oc0)),
        out_shape=jax.ShapeDtypeStruct(out_buf.shape, out_buf.dtype),
        scratch_shapes=[pltpu.VMEM((L, TQ), I32), pltpu.VMEM((H_IDX // 2, 2 * TQ, LANES), BF16),
                        pltpu.VMEM((KVH_DSA, G * TQ, HEAD_DIM), BF16),
                        pltpu.VMEM((KVH_DSA, 1, G * TQ), F32), pltpu.VMEM((KVH_DSA, 1, G * TQ), F32),
                        pltpu.VMEM((KVH_DSA, HEAD_DIM, G * TQ), F32)],
        input_output_aliases={6: 0},
        compiler_params=_params("arbitrary"),
        name="dsa",
    )(q, iq, iwt, k, v, ik2, out_buf)


def _bias_body(rb_ref, idx_ref, o_ref):
    n = rb_ref.shape[1]
    onehot = (lax.broadcasted_iota(I32, (n, idx_ref.shape[1]), 0) == idx_ref[...]).astype(F32)
    o_ref[...] = jnp.dot(rb_ref[...], onehot, preferred_element_type=F32, precision=lax.Precision.HIGHEST)


def _expand_bias(rel_bias, n_q, n_k, *, tn):
    H, R = rel_bias.shape
    RP = 3 * LANES
    t = np.arange(n_q)[:, None]
    s = np.arange(n_k)[None, :]
    idx = (np.clip(LEFT_CHUNKS * CHUNK + t - s, -REL_CLIP, REL_CLIP) + REL_CLIP).reshape(1, -1).astype(np.int32)
    out = pl.pallas_call(
        _bias_body,
        grid=(idx.shape[1] // tn,),
        in_specs=[pl.BlockSpec((H, RP), lambda j: (0, 0)), pl.BlockSpec((1, tn), lambda j: (0, j))],
        out_specs=pl.BlockSpec((H, tn), lambda j: (0, j)),
        out_shape=jax.ShapeDtypeStruct((H, idx.shape[1]), F32),
        compiler_params=_params("parallel"),
        name="rel_bias_expand",
    )(jnp.pad(rel_bias, ((0, 0), (0, RP - R))), jnp.asarray(idx))
    return out.reshape(H, n_q, n_k)


def _head_cols(ref):
    return lambda hh: ref[:, hh * HEAD_DIM:(hh + 1) * HEAD_DIM]


def _band_heads(q_ref, k_parts, v_parts, bias_ref, ok, o_ref, HB):
    def window(parts, hh):
        return jnp.concatenate([part(hh).astype(BF16) for part in parts], axis=0)

    q_head = _head_cols(q_ref)
    logits = [lax.dot_general(q_head(hh).astype(BF16), window(k_parts, hh), (((1,), (1,)), ((), ())),
                              preferred_element_type=F32) for hh in range(HB)]
    probs = []
    for hh in range(HB):
        s = logits[hh] * HEAD_DIM ** -0.5 + bias_ref[hh]
        if ok is not None:
            s = jnp.where(ok, s, NEG_BIG)
        p = jnp.exp(s - jnp.max(s, -1, keepdims=True))
        probs.append((p / jnp.sum(p, -1, keepdims=True)).astype(BF16))
    for hh in range(HB):
        o_ref[:, hh * HEAD_DIM:(hh + 1) * HEAD_DIM] = jnp.dot(
            probs[hh], window(v_parts, hh), preferred_element_type=F32).astype(o_ref.dtype)


def _band_prompt_body(q_ref, *rest, HB, TQ, NP):
    k_refs, v_refs = rest[:NP], rest[NP:2 * NP]
    bias_ref, o_ref = rest[2 * NP], rest[2 * NP + 2]
    j = pl.program_id(1)
    nk = NP * TQ
    q_pos = j * TQ + lax.broadcasted_iota(I32, (TQ, nk), 0)
    k_pos = (j - (NP - 1)) * TQ + lax.broadcasted_iota(I32, (TQ, nk), 1)
    qc, kc = q_pos >> 6, k_pos >> 6
    ok = jnp.logical_and(jnp.logical_and(k_pos >= 0, kc <= qc), kc >= qc - LEFT_CHUNKS)
    _band_heads(q_ref, [_head_cols(r) for r in k_refs], [_head_cols(r) for r in v_refs], bias_ref, ok, o_ref, HB)


def _band_prompt(qkv, bias, out_buf, *, n_blocks, HB, TQ, NP):
    W = HB * HEAD_DIM
    n_hg = H_CHK // HB

    def kv_spec(r, third):
        return pl.BlockSpec((TQ, W), lambda hg, j: (jnp.maximum(j - (NP - 1) + r, 0), third * n_hg + hg))

    return pl.pallas_call(
        functools.partial(_band_prompt_body, HB=HB, TQ=TQ, NP=NP),
        grid=(n_hg, n_blocks),
        in_specs=([pl.BlockSpec((TQ, W), lambda hg, j: (j, hg))]
                  + [kv_spec(r, 1) for r in range(NP)] + [kv_spec(r, 2) for r in range(NP)]
                  + [pl.BlockSpec((HB, TQ, NP * TQ), lambda hg, j: (hg, 0, 0)), pl.BlockSpec(memory_space=pl.ANY)]),
        out_specs=pl.BlockSpec((TQ, W), lambda hg, j: (j, hg)),
        out_shape=jax.ShapeDtypeStruct(out_buf.shape, out_buf.dtype),
        input_output_aliases={2 * NP + 2: 0},
        compiler_params=_params("parallel", "parallel"),
        name="band_prompt",
    )(qkv, *([qkv] * (2 * NP)), bias, out_buf)


def _band_sample_body(q_ref, kc_ref, kn_ref, vc_ref, vn_ref, bias_ref, _, o_ref, *, HB):
    _band_heads(q_ref, [_head_cols(kc_ref), _head_cols(kn_ref)], [_head_cols(vc_ref), _head_cols(vn_ref)], bias_ref,
                None, o_ref, HB)


def _band_sample(qkv, cache_k, cache_v, bias, out_buf, *, n_seq, T, row0, HB):
    P = cache_k.shape[1]
    W = HB * HEAD_DIM
    n_hg = H_CHK // HB
    rb0 = row0 // T
    new = lambda third: pl.BlockSpec((T, W), lambda b, hg: (rb0 + b, third * n_hg + hg))
    cache = pl.BlockSpec((None, P, W), lambda b, hg: (b, 0, hg))
    return pl.pallas_call(
        functools.partial(_band_sample_body, HB=HB),
        grid=(n_seq, n_hg),
        in_specs=[new(0), cache, new(1), cache, new(2),
                  pl.BlockSpec((HB, T, P + T), lambda b, hg: (hg, 0, 0)), pl.BlockSpec(memory_space=pl.ANY)],
        out_specs=pl.BlockSpec((T, W), lambda b, hg: (rb0 + b, hg)),
        out_shape=jax.ShapeDtypeStruct(out_buf.shape, out_buf.dtype),
        input_output_aliases={6: 0},
        compiler_params=_params("parallel", "parallel"),
        name="band_sample",
    )(qkv, cache_k, qkv, cache_v, qkv, bias, out_buf)


def _rot_tables(pos, head_dim, rot_dim, theta):
    half = rot_dim // 2
    inv = theta ** (-jnp.arange(half, dtype=F32) / half)
    ang = pos.astype(F32)[:, None] * inv[None, :]
    cos, sin = jnp.cos(ang), jnp.sin(ang)
    m = pos.shape[0]
    zh = jnp.zeros((m, half), F32)
    rest0 = jnp.zeros((m, head_dim - rot_dim), F32)
    c = jnp.concatenate([cos, cos, jnp.ones((m, head_dim - rot_dim), F32)], 1)
    s_up = jnp.concatenate([-sin, zh, rest0], 1)
    s_down = jnp.concatenate([zh, sin, rest0], 1)
    reps = LANES // head_dim
    return tuple(jnp.tile(t, (1, reps)) for t in (c, s_up, s_down))


def _log_gamma_table(width):
    lg = jnp.log1p(-(2.0 ** (-5.0 - jnp.arange(H_RET, dtype=F32))))
    return jnp.broadcast_to(lg[:, None, None], (H_RET, 8, width))


def _routing(top_e, sb, rg, row_multiple):
    n = top_e.shape[0]
    a = n * TOP_K
    n_rows = -(-((-(-a // sb) + N_EXPERTS) * sb + rg) // row_multiple) * row_multiple
    n_groups = N_EXPERTS + a // rg
    flat_e = top_e.reshape(-1)
    onehot = (flat_e[:, None] == jnp.arange(N_EXPERTS, dtype=I32)[None, :]).astype(I32)
    rank = jnp.take_along_axis(jnp.cumsum(onehot, 0), flat_e[:, None], 1)[:, 0] - 1
    counts = jnp.sum(onehot, 0)
    padded = (counts + sb - 1) // sb * sb
    pad_end = jnp.cumsum(padded)
    pad_start = pad_end - padded
    slot = pad_start[flat_e] + rank
    slot_tok = jnp.zeros((n_rows,), I32).at[slot].set(jnp.arange(a, dtype=I32) // TOP_K)

    g_count = (padded + rg - 1) // rg
    g_end = jnp.cumsum(g_count)
    g_start = g_end - g_count
    s = jnp.arange(n_groups, dtype=I32)
    live = s < g_end[-1]
    expert = jnp.minimum(jnp.searchsorted(g_end, jnp.where(live, s, g_end[-1] - 1), side='right'), N_EXPERTS - 1)
    k = s - g_start[expert]
    nb = jnp.where(live, jnp.clip(padded[expert] - k * rg, 0, rg) // sb, 0)
    st = jnp.where(live, pad_start[expert] + k * rg, pad_end[-1]) // sb
    groups = _Groups(expert.astype(I32), st.astype(I32), nb.astype(I32), rg, sb, False)
    own_row = (g_start[flat_e] + rank // rg) * rg + rank % rg
    return own_row.reshape(n, TOP_K).astype(I32), slot_tok, pad_end[-1:].astype(I32), groups


RG = 2080
RG_NARROW = 1664
TN = 512
TN_PAIR = 256
TN_DOWN = 2048
TK_DOWN = 1024
TR = 208
RG_MOE = 2560
SB_MOE = 64
RET_CHUNK = 256
RET_HR = 8
DSA_TQ = 128
DSA_TQ_S = 128
DSA_TK = 512
GATHER_TB = 256
BAND_HB = 8
BAND_TQ = 128
BAND_PARTS = (LEFT_CHUNKS * CHUNK) // BAND_TQ + 1


def kernel(x_prompt, x_sample, p_prompt, p_sample, state_ret, cache_dsa_k, cache_dsa_v, cache_dsa_kidx,
           cache_chk_k, cache_chk_v, ln_mix_g, ln_mix_b, ln_ffn_g, ln_ffn_b, ple_proj, ple_gate,
           w_in_even, ret_gn_g, kidx_ln_g, kidx_ln_b, w_out_even, ffn_w_gate, ffn_w_up, ffn_w_down,
           w_in_odd, rel_bias, w_out_odd, router_w, exp_w_gate, exp_w_up, exp_w_down):
    seq, d = x_prompt.shape[1], x_prompt.shape[2]
    nb_s, t_s = x_sample.shape[0], x_sample.shape[1]
    n_s = nb_s * t_s
    m = seq + n_s
    past = cache_dsa_k.shape[2]
    ret_w = H_RET * HEAD_DIM
    dsa_w = H_DSA * HEAD_DIM
    kv_w = KVH_DSA * HEAD_DIM

    x, xb = _stack_rows(x_prompt[0], x_sample.reshape(n_s, d), tr=n_s)
    p = jnp.concatenate([p_prompt[:, 0], p_sample.reshape(DEPTH, n_s, -1)], 1).astype(BF16)
    pos = jnp.concatenate([jnp.arange(seq, dtype=I32), past + jnp.tile(jnp.arange(t_s, dtype=I32), nb_s)])
    tab_ret = _rot_tables(pos, HEAD_DIM, HEAD_DIM, RET_THETA)
    tab_dsa = _rot_tables(pos, HEAD_DIM, ROPE_DIM, ROPE_THETA)
    tab_idx = _rot_tables(pos, D_IDX, IDX_ROPE_DIM, ROPE_THETA)

    tokens = [_dense_groups(m, RG, lead=i) for i in range(DEPTH)]
    tokens_narrow = [_dense_groups(m, RG_NARROW, lead=i, x_buffers=2) for i in range(DEPTH)]
    h = _rg_matmul(tokens[0], xb, jnp.swapaxes(w_in_even, 1, 2), tn=TN, w_transposed=True)
    c_dq = 4 * ret_w
    c_dk = c_dq + dsa_w
    c_dv = c_dk + kv_w
    c_iq = c_dv + kv_w
    c_ik = c_iq + H_IDX * D_IDX

    mix_in = jnp.zeros((m, ret_w + dsa_w), BF16)
    state0_p = jnp.zeros((1, H_RET, HEAD_DIM, HEAD_DIM), F32)
    mix_in, st_p = _retention(h, tab_ret, _log_gamma_table(RET_CHUNK), ret_gn_g[0], state0_p, mix_in,
                              C=RET_CHUNK, HR=RET_HR, n_seq=1, n_chunks=seq // RET_CHUNK, row0=0)
    mix_in, st_s = _retention(h, tab_ret, _log_gamma_table(LANES), ret_gn_g[0], state_ret[0], mix_in,
                              C=t_s, HR=RET_HR, n_seq=nb_s, n_chunks=1, row0=seq)

    (dq,) = _rotary_cols(h, c_dq // dsa_w, dsa_w, tab_dsa, ROPE_DIM // 2, [BF16], tr=TR)
    dk, dk_b = _rotary_cols(h, c_dk // kv_w, kv_w, tab_dsa, ROPE_DIM // 2, [F32, BF16], tr=TR)
    iq_lo, = _rotary_cols(h, c_iq // 1024, 1024, tab_idx, IDX_ROPE_DIM // 2, [BF16], tr=TR)
    iq_hi, = _rotary_cols(h, c_iq // 1024 + 1, 1024, tab_idx, IDX_ROPE_DIM // 2, [BF16], tr=TR)
    iq = jnp.concatenate([iq_lo, iq_hi], 1)
    ik, ik2, iw = _indexer_keys(h, c_ik // LANES, kidx_ln_g[0], kidx_ln_b[0], tab_idx, tr=TR)
    dv, dv_b = _copy_cols(h, c_dv // kv_w, kv_w, [F32, BF16], tr=TR)
    iwt = iw.T

    n_sel_p = min(TOPK_MAX, seq // 4)
    mix_in = _dsa(dq, iq, iwt, dk_b[:seq], dv_b[:seq], ik2[:seq], mix_in, TQ=DSA_TQ, TK=DSA_TK,
                  n_blocks=seq // DSA_TQ, n_sel=n_sel_p, q_pos0=0, q_stride=DSA_TQ, causal=True,
                  q_row0=0, out_row0=0, out_col0=ret_w, per_block_keys=False)

    def pad_queries(a):
        a = a[seq:].reshape(nb_s, t_s, -1)
        return jnp.pad(a, ((0, 0), (0, DSA_TQ_S - t_s), (0, 0))).reshape(nb_s * DSA_TQ_S, -1)

    def with_cache(cache, new):
        return jnp.concatenate([cache.reshape(nb_s, past, -1).astype(BF16), new[seq:].reshape(nb_s, t_s, -1)], 1)

    kidx_c = cache_dsa_kidx[0].astype(BF16)
    l_s = past + t_s
    do_s = _dsa(pad_queries(dq), pad_queries(iq), pad_queries(iw).T,
                with_cache(cache_dsa_k[0], dk_b), with_cache(cache_dsa_v[0], dv_b),
                with_cache(jnp.concatenate([kidx_c, kidx_c], -1), ik2),
                jnp.zeros((nb_s * DSA_TQ_S, dsa_w), BF16), TQ=DSA_TQ_S, TK=l_s, n_blocks=nb_s,
                n_sel=min(TOPK_MAX, l_s // 4), q_pos0=past, q_stride=0, causal=False,
                q_row0=0, out_row0=0, out_col0=0, per_block_keys=True)
    do_s = do_s.reshape(nb_s, DSA_TQ_S, dsa_w)[:, :t_s].reshape(n_s, dsa_w)
    mix_in = lax.dynamic_update_slice(mix_in, do_s, (seq, ret_w))

    mix = _rg_matmul(tokens_narrow[0], mix_in, w_out_even, tn=TN)
    x1, x1b = _residual_layer_norm(x, mix, ln_mix_g[0], ln_mix_b[0], tr=TR)
    act = _rg_swiglu(tokens[0], x1b, ffn_w_gate, ffn_w_up, tn=TN_PAIR)
    ffn = _rg_down(tokens[0], act, ffn_w_down, tn=TN_DOWN, tk=TK_DOWN)
    x2, x2b = _residual_layer_norm(x1, ffn, ln_ffn_g[0], ln_ffn_b[0], tr=TR)
    x3, x3b = _rg_ple(tokens_narrow[0], x2b, ple_gate, p[0], ple_proj, x2, tn=TN_PAIR)

    qkv = _rg_matmul(tokens[0], x3b, w_in_odd, tn=TN, out_dtype=BF16)
    chk_w = H_CHK * HEAD_DIM
    keep = min(LEFT_CHUNKS * CHUNK, seq)
    kv_new = _rg_matmul(_dense_groups(keep + n_s, keep + n_s), x3b[seq - keep:], w_in_odd, tn=TN,
                        stripes=(chk_w // TN, 2 * chk_w // TN))
    p_band = cache_chk_k.shape[2]
    bias_p = _expand_bias(rel_bias[0], BAND_TQ, BAND_PARTS * BAND_TQ, tn=8192)
    bias_s = _expand_bias(rel_bias[0], t_s, p_band + t_s, tn=(t_s * (p_band + t_s)) // 2)
    att = _band_prompt(qkv, bias_p, jnp.zeros((m, chk_w), BF16), n_blocks=seq // BAND_TQ, HB=BAND_HB, TQ=BAND_TQ,
                       NP=BAND_PARTS)
    att = _band_sample(qkv, cache_chk_k[0].reshape(nb_s, p_band, chk_w), cache_chk_v[0].reshape(nb_s, p_band, chk_w),
                       bias_s, att, n_seq=nb_s, T=t_s, row0=seq, HB=BAND_HB)
    mix = _rg_matmul(tokens_narrow[0], att, w_out_odd, tn=TN)
    x4, x4b = _residual_layer_norm(x3, mix, ln_mix_g[1], ln_mix_b[1], tr=TR)

    logits = _rg_matmul(tokens[0], x4b, router_w, tn=N_EXPERTS)
    top_e, gates = _router_top2(logits, tr=TR)
    tok_slot, slot_tok, n_slots, experts = _routing(top_e, SB_MOE, RG_MOE, GATHER_TB)
    xs = _gather_rows(slot_tok, n_slots, x4, tb=GATHER_TB)
    acts = _rg_swiglu(experts, xs, exp_w_gate[0], exp_w_up[0], tn=TN_PAIR)
    yb = _rg_down(experts, acts, exp_w_down[0], tn=TN_DOWN, tk=TK_DOWN)
    x5, x5b = _moe_combine_ln(tok_slot, yb, gates, x4, ln_ffn_g[1], ln_ffn_b[1], tb=TR)
    y, _ = _rg_ple(tokens_narrow[1], x5b, ple_gate, p[1], ple_proj, x5, tn=TN_PAIR)

    ck = kv_new[:, :chk_w]
    cv = kv_new[:, chk_w:]

    def heads(a, nh):
        return a.reshape(a.shape[0], nh, HEAD_DIM)

    return (y[:seq][None], y[seq:].reshape(nb_s, t_s, d),
            st_p.reshape(1, 1, H_RET, HEAD_DIM, HEAD_DIM), st_s.reshape(1, nb_s, H_RET, HEAD_DIM, HEAD_DIM),
            heads(dk[:seq], KVH_DSA)[None, None], heads(dv[:seq], KVH_DSA)[None, None], ik[:seq][None, None],
            heads(dk[seq:], KVH_DSA).reshape(1, nb_s, t_s, KVH_DSA, HEAD_DIM),
            heads(dv[seq:], KVH_DSA).reshape(1, nb_s, t_s, KVH_DSA, HEAD_DIM),
            ik[seq:].reshape(1, nb_s, t_s, D_IDX),
            heads(ck[:keep], H_CHK)[None, None], heads(cv[:keep], H_CHK)[None, None],
            heads(ck[keep:], H_CHK).reshape(1, nb_s, t_s, H_CHK, HEAD_DIM),
            heads(cv[keep:], H_CHK).reshape(1, nb_s, t_s, H_CHK, HEAD_DIM))
```

```python
import functools
import math

import numpy as np
import jax
import jax.numpy as jnp
from jax import lax
from jax.experimental import pallas as pl
from jax.experimental.pallas import tpu as pltpu

F32 = jnp.float32
BF16 = jnp.bfloat16
I32 = jnp.int32

CHUNK = 64
HEAD_DIM = 128
H_RET = 16
H_DSA = 16
KVH_DSA = 4
DSA_GROUP = H_DSA // KVH_DSA
H_IDX = 32
D_IDX = 64
TOPK_MAX = 256
H_CHK = 32
LEFT_CHUNKS = 8
REL_CLIP = 128
N_EXPERTS = 8
TOP_K = 2
RET_THETA = 10000.0
ROPE_THETA = 500000.0
ROPE_DIM = HEAD_DIM // 4
IDX_ROPE_DIM = D_IDX // 4
LN_EPS = 1e-5
DEPTH = 2
ALPHA = (2.0 * DEPTH) ** 0.25

LANES = 128
DMA_PRIORITIES = 2
MAX_PIECE = 8
VMEM_LIMIT = 56 * 1024 * 1024
NEG_BIG = -1e30
EXP2_SCALE = HEAD_DIM ** -0.5 * math.log2(math.e)
INT_MIN = -(2 ** 31)
INT_MAX = 2 ** 31 - 1


def _params(*sem):
    return pltpu.CompilerParams(dimension_semantics=sem, vmem_limit_bytes=VMEM_LIMIT)


class _Groups:
    def __init__(self, e, st, nb, rg, sb, dense, x_buffers=1):
        self.e, self.st, self.nb, self.rg, self.sb, self.dense, self.x_buffers = e, st, nb, rg, sb, dense, x_buffers
        self.n = e.shape[0]

    def rows(self, width, col, buffers=None, own=False):
        kw = {} if buffers is None else dict(pipeline_mode=pl.Buffered(buffers))
        if own:
            return pl.BlockSpec((self.rg, width), lambda s, *a: (s, col(a[-1][s] > 0, *a[:-3])), **kw)
        if self.dense:
            return pl.BlockSpec((self.rg, width), lambda s, *a: (a[-2][s], col(a[-1][s] > 0, *a[:-3])), **kw)
        unit = self.sb
        return pl.BlockSpec((pl.Element(self.rg), pl.Element(width)),
                            lambda s, *a: (a[-2][s] * unit, col(a[-1][s] > 0, *a[:-3]) * width), **kw)

    def weights(self, k_rows, width, kcol):
        return pl.BlockSpec((None, k_rows, width), lambda s, *a: (a[-3][s], *kcol(a[-1][s] > 0, *a[:-3])))


def _dense_groups(m, rg, lead=0, x_buffers=1):
    n = m // rg
    return _Groups(jnp.full((n,), lead, I32), jnp.arange(n, dtype=I32), jnp.ones((n,), I32), rg, rg, True, x_buffers)


def _live_rows(nb_ref, n_sub, sb, compute):
    nblk = nb_ref[pl.program_id(0)]
    if n_sub == 1:
        @pl.when(nblk > 0)
        def _():
            compute(pl.ds(0, sb))

        return nblk

    def run_full(r, c):
        rows = MAX_PIECE * sb
        compute(pl.ds(pl.multiple_of(r * rows, rows), rows))
        return c

    lax.fori_loop(0, nblk // MAX_PIECE, run_full, 0)
    piece = MAX_PIECE // 2
    while piece >= 1:
        def run_piece(piece=piece):
            start = (nblk // (2 * piece)) * (2 * piece * sb)
            compute(pl.ds(pl.multiple_of(start, 2 * piece * sb), piece * sb))

        pl.when((nblk // piece) % 2 == 1)(run_piece)
        piece //= 2
    return nblk


def _row_loops(nb_ref, n_sub, sb, compute, o_refs):
    nblk = _live_rows(nb_ref, n_sub, sb, compute)

    def clear(r, c):
        for o_ref in o_refs:
            o_ref[pl.ds(pl.multiple_of(r * sb, sb), sb), :] = jnp.zeros((sb, o_ref.shape[1]), o_ref.dtype)
        return c

    lax.fori_loop(nblk, n_sub, clear, 0)


def _bf16_dot(x, w_ref):
    return jnp.dot(x, w_ref[...].astype(BF16), preferred_element_type=F32)


def _rg_mm_body(e_ref, st_ref, nb_ref, x_ref, w_ref, o_ref, *, sb, w_transposed):
    def compute(rows):
        if w_transposed:
            y = lax.dot_general(x_ref[rows, :], w_ref[...].astype(BF16), (((1,), (1,)), ((), ())),
                                preferred_element_type=F32)
        else:
            y = _bf16_dot(x_ref[rows, :], w_ref)
        o_ref[rows, :] = y.astype(o_ref.dtype)

    _row_loops(nb_ref, x_ref.shape[0] // sb, sb, compute, [o_ref])


def _rg_swiglu_body(e_ref, st_ref, nb_ref, x_ref, wg_ref, wu_ref, o_ref, *, sb):
    def compute(rows):
        x = x_ref[rows, :]
        g = _bf16_dot(x, wg_ref)
        u = _bf16_dot(x, wu_ref)
        o_ref[rows, :] = (jax.nn.silu(g) * u).astype(o_ref.dtype)

    _row_loops(nb_ref, x_ref.shape[0] // sb, sb, compute, [o_ref])


def _rg_ple_body(e_ref, st_ref, nb_ref, x_ref, w_ref, p_ref, pw_ref, r_ref, of_ref, ob_ref, *, sb):
    def compute(rows):
        gate = _bf16_dot(x_ref[rows, :], w_ref)
        proj = _bf16_dot(p_ref[rows, :], pw_ref)
        y = r_ref[rows, :] + jax.nn.sigmoid(gate) * proj
        of_ref[rows, :] = y
        ob_ref[rows, :] = y.astype(BF16)

    _row_loops(nb_ref, x_ref.shape[0] // sb, sb, compute, [of_ref, ob_ref])


def _rg_down_body(e_ref, st_ref, nb_ref, a_ref, w_ref, o_ref, *, sb):
    @pl.when(pl.program_id(2) == 0)
    def _():
        o_ref[...] = jnp.zeros_like(o_ref)

    def compute(rows):
        o_ref[rows, :] += _bf16_dot(a_ref[rows, :], w_ref)

    _live_rows(nb_ref, a_ref.shape[0] // sb, sb, compute)


def _last_if_dead(n_blocks):
    return lambda live, j: jnp.where(live, j, n_blocks - 1)


def _rg_matmul(g, x, w, *, tn, out_dtype=F32, stripes=None, w_transposed=False):
    K = x.shape[1]
    n_w = w.shape[-2] if w_transposed else w.shape[-1]
    j0, J = (0, pl.cdiv(n_w, tn)) if stripes is None else stripes
    N = n_w if stripes is None else J * tn
    col = _last_if_dead(J)
    if w_transposed:
        w_spec = g.weights(tn, K, lambda live, j: (j0 + col(live, j), 0))
    else:
        w_spec = g.weights(K, tn, lambda live, j: (0, j0 + col(live, j)))
    return pl.pallas_call(
        functools.partial(_rg_mm_body, sb=g.sb, w_transposed=w_transposed),
        grid_spec=pltpu.PrefetchScalarGridSpec(
            num_scalar_prefetch=3, grid=(g.n, J),
            in_specs=[g.rows(K, lambda live, j: 0, buffers=g.x_buffers), w_spec],
            out_specs=g.rows(tn, lambda live, j: j, own=True)),
        out_shape=jax.ShapeDtypeStruct((g.n * g.rg, N), out_dtype),
        compiler_params=_params("arbitrary", "arbitrary"),
        name="rg_mm",
    )(g.e, g.st, g.nb, x, w)


def _rg_swiglu(g, x, wg, wu, *, tn):
    K = x.shape[1]
    N = wg.shape[-1]
    col = _last_if_dead(N // tn)
    wspec = g.weights(K, tn, lambda live, j: (0, col(live, j)))
    return pl.pallas_call(
        functools.partial(_rg_swiglu_body, sb=g.sb),
        grid_spec=pltpu.PrefetchScalarGridSpec(
            num_scalar_prefetch=3, grid=(g.n, N // tn),
            in_specs=[g.rows(K, lambda live, j: 0, buffers=g.x_buffers), wspec, wspec],
            out_specs=g.rows(tn, lambda live, j: j, own=True)),
        out_shape=jax.ShapeDtypeStruct((g.n * g.rg, N), BF16),
        compiler_params=_params("arbitrary", "arbitrary"),
        name="rg_swiglu",
    )(g.e, g.st, g.nb, x, wg, wu)


def _rg_ple(g, x, w, p, pw, resid, *, tn):
    K, KP = x.shape[1], p.shape[1]
    N = w.shape[-1]
    col = _last_if_dead(N // tn)
    wcol = lambda live, j: (0, col(live, j))
    tile = g.rows(tn, lambda live, j: j, own=True)
    return pl.pallas_call(
        functools.partial(_rg_ple_body, sb=g.sb),
        grid_spec=pltpu.PrefetchScalarGridSpec(
            num_scalar_prefetch=3, grid=(g.n, N // tn),
            in_specs=[g.rows(K, lambda live, j: 0, buffers=g.x_buffers), g.weights(K, tn, wcol),
                      g.rows(KP, lambda live, j: 0, buffers=g.x_buffers), g.weights(KP, tn, wcol), tile],
            out_specs=[tile, tile]),
        out_shape=[jax.ShapeDtypeStruct((g.n * g.rg, N), F32), jax.ShapeDtypeStruct((g.n * g.rg, N), BF16)],
        compiler_params=_params("arbitrary", "arbitrary"),
        name="rg_ple",
    )(g.e, g.st, g.nb, x, w, p, pw, resid)


def _rg_down(g, a, w, *, tn, tk):
    K = a.shape[1]
    N = w.shape[-1]
    NT, KC = N // tn, K // tk
    ncol = lambda live, n, kc: jnp.where(live, n, NT - 1)
    kcol = lambda live, n, kc: jnp.where(live, kc, KC - 1)
    return pl.pallas_call(
        functools.partial(_rg_down_body, sb=g.sb),
        grid_spec=pltpu.PrefetchScalarGridSpec(
            num_scalar_prefetch=3, grid=(g.n, NT, KC),
            in_specs=[g.rows(tk, kcol, own=True),
                      g.weights(tk, tn, lambda live, n, kc: (kcol(live, n, kc), ncol(live, n, kc)))],
            out_specs=g.rows(tn, lambda live, n, kc: n, buffers=1, own=True)),
        out_shape=jax.ShapeDtypeStruct((g.n * g.rg, N), F32),
        compiler_params=_params("arbitrary", "arbitrary", "arbitrary"),
        name="rg_down",
    )(g.e, g.st, g.nb, a, w)


def _layer_norm_rows(z, g, b):
    mu = jnp.mean(z, -1, keepdims=True)
    d = z - mu
    var = jnp.mean(d * d, -1, keepdims=True)
    return d * lax.rsqrt(var + LN_EPS) * g + b


def _ln_body(a_ref, b_ref, g_ref, beta_ref, of_ref, ob_ref):
    y = _layer_norm_rows(ALPHA * a_ref[...] + b_ref[...], g_ref[...], beta_ref[...])
    of_ref[...] = y
    ob_ref[...] = y.astype(BF16)


def _residual_layer_norm(a, b, g, beta, *, tr):
    M, D = a.shape
    row = pl.BlockSpec((tr, D), lambda i: (i, 0))
    vec = pl.BlockSpec((1, D), lambda i: (0, 0))
    return pl.pallas_call(
        _ln_body,
        grid=(M // tr,),
        in_specs=[row, row, vec, vec],
        out_specs=[row, row],
        out_shape=[jax.ShapeDtypeStruct((M, D), F32), jax.ShapeDtypeStruct((M, D), BF16)],
        compiler_params=_params("parallel"),
        name="res_ln",
    )(a, b, g.reshape(1, D), beta.reshape(1, D))


def _stack_body(a_ref, b_ref, of_ref, ob_ref, *, n_a):
    def emit(src_ref):
        y = src_ref[...]
        of_ref[...] = y
        ob_ref[...] = y.astype(BF16)

    pl.when(pl.program_id(0) < n_a)(functools.partial(emit, a_ref))
    pl.when(pl.program_id(0) >= n_a)(functools.partial(emit, b_ref))


def _stack_rows(a, b, *, tr):
    n_a, n_b = a.shape[0] // tr, b.shape[0] // tr
    D = a.shape[1]
    row = pl.BlockSpec((tr, D), lambda i: (i, 0))
    return pl.pallas_call(
        functools.partial(_stack_body, n_a=n_a),
        grid=(n_a + n_b,),
        in_specs=[pl.BlockSpec((tr, D), lambda i: (jnp.minimum(i, n_a - 1), 0)),
                  pl.BlockSpec((tr, D), lambda i: (jnp.maximum(i - n_a, 0), 0))],
        out_specs=[row, row],
        out_shape=[jax.ShapeDtypeStruct((a.shape[0] + b.shape[0], D), F32),
                   jax.ShapeDtypeStruct((a.shape[0] + b.shape[0], D), BF16)],
        compiler_params=_params("arbitrary"),
        name="stack_rows",
    )(a, b)


def _rotate(x, c, s_up, s_down, half):
    w = x.shape[1]
    reps = w // LANES
    if reps > 1:
        c, s_up, s_down = (jnp.concatenate([t] * reps, axis=1) for t in (c, s_up, s_down))
    return x * c + pltpu.roll(x, w - half, 1) * s_up + pltpu.roll(x, half, 1) * s_down


def _rot_body(x_ref, c_ref, su_ref, sd_ref, *o_refs, half):
    y = _rotate(x_ref[...], c_ref[...], su_ref[...], sd_ref[...], half)
    for o_ref in o_refs:
        o_ref[...] = y.astype(o_ref.dtype)


def _rotary_cols(h, col_block, width, tables, half, out_dtypes, *, tr):
    M = h.shape[0]
    tab = pl.BlockSpec((tr, LANES), lambda i: (i, 0))
    return pl.pallas_call(
        functools.partial(_rot_body, half=half),
        grid=(M // tr,),
        in_specs=[pl.BlockSpec((tr, width), lambda i: (i, col_block)), tab, tab, tab],
        out_specs=[pl.BlockSpec((tr, width), lambda i: (i, 0)) for _ in out_dtypes],
        out_shape=[jax.ShapeDtypeStruct((M, width), dt) for dt in out_dtypes],
        compiler_params=_params("parallel"),
        name="rotary",
    )(h, *tables)


def _copy_body(x_ref, *o_refs):
    for o_ref in o_refs:
        o_ref[...] = x_ref[...].astype(o_ref.dtype)


def _copy_cols(h, col_block, width, out_dtypes, *, tr):
    M = h.shape[0]
    return pl.pallas_call(
        _copy_body,
        grid=(M // tr,),
        in_specs=[pl.BlockSpec((tr, width), lambda i: (i, col_block))],
        out_specs=[pl.BlockSpec((tr, width), lambda i: (i, 0)) for _ in out_dtypes],
        out_shape=[jax.ShapeDtypeStruct((M, width), dt) for dt in out_dtypes],
        compiler_params=_params("parallel"),
        name="copy_cols",
    )(h)


def _ik_body(x_ref, g_ref, b_ref, c_ref, su_ref, sd_ref, ik_ref, ik2_ref, iw_ref):
    x = x_ref[...]
    is_key = lax.broadcasted_iota(I32, x.shape, 1) < D_IDX
    mu = jnp.sum(jnp.where(is_key, x, 0.0), -1, keepdims=True) / D_IDX
    d = jnp.where(is_key, x - mu, 0.0)
    var = jnp.sum(d * d, -1, keepdims=True) / D_IDX
    y = jnp.where(is_key, d * lax.rsqrt(var + LN_EPS) * g_ref[...] + b_ref[...], 0.0)
    y = _rotate(y, c_ref[...], su_ref[...], sd_ref[...], IDX_ROPE_DIM // 2)
    y = jnp.where(is_key, y, 0.0)
    ik_ref[...] = y[:, :D_IDX]
    ik2_ref[...] = (y + pltpu.roll(y, D_IDX, 1)).astype(BF16)
    iw_ref[...] = pltpu.roll(x, LANES - D_IDX, 1)[:, :H_IDX]


def _indexer_keys(h, col_block, g, b, tables, *, tr):
    M = h.shape[0]
    pad = LANES - D_IDX
    tab = pl.BlockSpec((tr, LANES), lambda i: (i, 0))
    vec = pl.BlockSpec((1, LANES), lambda i: (0, 0))
    return pl.pallas_call(
        _ik_body,
        grid=(M // tr,),
        in_specs=[pl.BlockSpec((tr, LANES), lambda i: (i, col_block)), vec, vec, tab, tab, tab],
        out_specs=[pl.BlockSpec((tr, D_IDX), lambda i: (i, 0)), pl.BlockSpec((tr, LANES), lambda i: (i, 0)),
                   pl.BlockSpec((tr, H_IDX), lambda i: (i, 0))],
        out_shape=[jax.ShapeDtypeStruct((M, D_IDX), F32), jax.ShapeDtypeStruct((M, LANES), BF16),
                   jax.ShapeDtypeStruct((M, H_IDX), F32)],
        compiler_params=_params("parallel"),
        name="indexer_keys",
    )(h, jnp.pad(g, (0, pad)).reshape(1, LANES), jnp.pad(b, (0, pad)).reshape(1, LANES), *tables)


def _router_body(l_ref, e_ref, g_ref):
    lg = l_ref[...]
    idx = lax.broadcasted_iota(I32, lg.shape, 1)
    m1 = jnp.max(lg, -1, keepdims=True)
    e1 = jnp.min(jnp.where(lg == m1, idx, N_EXPERTS), -1, keepdims=True)
    rest = jnp.where(idx == e1, -jnp.inf, lg)
    m2 = jnp.max(rest, -1, keepdims=True)
    e2 = jnp.min(jnp.where(rest == m2, idx, N_EXPERTS), -1, keepdims=True)
    ex2 = jnp.exp(m2 - m1)
    den = 1.0 + ex2
    e_ref[...] = jnp.concatenate([e1, e2], axis=1)
    g_ref[...] = jnp.concatenate([1.0 / den, ex2 / den], axis=1)


def _router_top2(logits, *, tr):
    M, E = logits.shape
    return pl.pallas_call(
        _router_body,
        grid=(M // tr,),
        in_specs=[pl.BlockSpec((tr, E), lambda i: (i, 0))],
        out_specs=[pl.BlockSpec((tr, TOP_K), lambda i: (i, 0)), pl.BlockSpec((tr, TOP_K), lambda i: (i, 0))],
        out_shape=[jax.ShapeDtypeStruct((M, TOP_K), I32), jax.ShapeDtypeStruct((M, TOP_K), F32)],
        compiler_params=_params("parallel"),
        name="router_top2",
    )(logits)


def _row_copy(src_hbm, row, dst, r, sem):
    return pltpu.make_async_copy(src_hbm.at[pl.ds(row, 1), :], dst.at[pl.ds(r, 1), :], sem)


def _gather_body(tok_ref, n_live_ref, x_hbm, o_ref, buf, sem, *, tb):
    base = pl.program_id(0) * tb

    def start(r2, _):
        for u in range(DMA_PRIORITIES):
            r = r2 * DMA_PRIORITIES + u
            _row_copy(x_hbm, tok_ref[base + r], buf, r, sem).start(priority=u)
        return 0

    def wait(r, _):
        _row_copy(x_hbm, 0, buf, r, sem).wait()
        return 0

    @pl.when(base < n_live_ref[0])
    def _():
        lax.fori_loop(0, tb // DMA_PRIORITIES, start, 0)
        lax.fori_loop(0, tb, wait, 0)
        o_ref[...] = buf[...].astype(BF16)

    @pl.when(base >= n_live_ref[0])
    def _():
        o_ref[...] = jnp.zeros_like(o_ref)


def _gather_rows(slot_tok, n_live, x, *, tb):
    A = slot_tok.shape[0]
    D = x.shape[1]
    return pl.pallas_call(
        functools.partial(_gather_body, tb=tb),
        grid_spec=pltpu.PrefetchScalarGridSpec(
            num_scalar_prefetch=2,
            grid=(A // tb,),
            in_specs=[pl.BlockSpec(memory_space=pl.ANY)],
            out_specs=pl.BlockSpec((tb, D), lambda i, tok, n_live: (i, 0)),
            scratch_shapes=[pltpu.VMEM((tb, D), F32), pltpu.SemaphoreType.DMA(())]),
        out_shape=jax.ShapeDtypeStruct((A, D), BF16),
        compiler_params=_params("arbitrary"),
        name="moe_gather",
    )(slot_tok, n_live, x)


def _combine_body(slot_ref, y_hbm, gate_ref, x_ref, g_ref, beta_ref, of_ref, ob_ref, buf, sem, *, tb):
    base = pl.program_id(0) * tb

    def start(r, _):
        for k in range(TOP_K):
            _row_copy(y_hbm, slot_ref[(base + r) * TOP_K + k], buf.at[k], r, sem).start(priority=k % DMA_PRIORITIES)
        return 0

    def wait(r, _):
        for k in range(TOP_K):
            _row_copy(y_hbm, 0, buf.at[k], r, sem).wait()
        return 0

    lax.fori_loop(0, tb, start, 0)
    lax.fori_loop(0, tb, wait, 0)
    gates = gate_ref[...]
    y = buf[0] * gates[:, 0:1] + buf[1] * gates[:, 1:2]
    z = _layer_norm_rows(ALPHA * x_ref[...] + y, g_ref[...], beta_ref[...])
    of_ref[...] = z
    ob_ref[...] = z.astype(BF16)


def _moe_combine_ln(tok_slot, yb, gates, x, g, beta, *, tb):
    M, D = x.shape
    row = pl.BlockSpec((tb, D), lambda i, s: (i, 0))
    vec = pl.BlockSpec((1, D), lambda i, s: (0, 0))
    return pl.pallas_call(
        functools.partial(_combine_body, tb=tb),
        grid_spec=pltpu.PrefetchScalarGridSpec(
            num_scalar_prefetch=1,
            grid=(M // tb,),
            in_specs=[pl.BlockSpec(memory_space=pl.ANY), pl.BlockSpec((tb, TOP_K), lambda i, s: (i, 0)),
                      row, vec, vec],
            out_specs=[row, row],
            scratch_shapes=[pltpu.VMEM((TOP_K, tb, D), F32), pltpu.SemaphoreType.DMA(())]),
        out_shape=[jax.ShapeDtypeStruct((M, D), F32), jax.ShapeDtypeStruct((M, D), BF16)],
        compiler_params=_params("arbitrary"),
        name="moe_combine",
    )(tok_slot.reshape(-1), yb, gates, x, g.reshape(1, D), beta.reshape(1, D))


def _ret_body(q_ref, k_ref, v_ref, gate_ref, c_ref, su_ref, sd_ref, lg_ref, gn_ref, s0_ref, _, o_ref, st_ref,
              s_scr, *, C, HR):
    @pl.when(pl.program_id(1) == 0)
    def _():
        s_scr[...] = s0_ref[...]

    heads = range(HR)
    cols = [slice(a * HEAD_DIM, (a + 1) * HEAD_DIM) for a in heads]
    c, su, sd = c_ref[...], su_ref[...], sd_ref[...]
    n_col = lax.broadcasted_iota(I32, (C, C), 0)
    m_row = lax.broadcasted_iota(I32, (C, C), 1)
    diff = (n_col - m_row).astype(F32)
    n_idx = lax.broadcasted_iota(I32, (C, LANES), 0).astype(F32)

    qb, k, vb, lg_lane = [], [], [], []
    for a in heads:
        qb.append(_rotate(q_ref[:, cols[a]], c, su, sd, HEAD_DIM // 2).astype(BF16))
        k.append(_rotate(k_ref[:, cols[a]], c, su, sd, HEAD_DIM // 2) * HEAD_DIM ** -0.5)
        vb.append(v_ref[:, cols[a]].astype(BF16))
        lg_lane.append(lg_ref[a, 0:1, :LANES])
    s = []
    for a in heads:
        intra = jnp.where(diff >= 0, jnp.exp(lg_ref[a, 0:1, :C] * jnp.maximum(diff, 0.0)), 0.0)
        s.append(lax.dot_general(qb[a], k[a].astype(BF16), (((1,), (1,)), ((), ())),
                                 preferred_element_type=F32) * intra)
    state = [s_scr[a] for a in heads]
    o = [jnp.dot(s[a].astype(BF16), vb[a], preferred_element_type=F32)
         + jnp.dot(qb[a], state[a].astype(BF16), preferred_element_type=F32) * jnp.exp(lg_lane[a] * (n_idx + 1.0))
         for a in heads]
    for a in heads:
        kd = (k[a] * jnp.exp(lg_lane[a] * (C - 1.0 - n_idx))).astype(BF16)
        new = state[a] * jnp.exp(lg_lane[a] * C) + lax.dot_general(kd, vb[a], (((0,), (0,)), ((), ())),
                                                                    preferred_element_type=F32)
        s_scr[a] = new
        st_ref[a] = new
    for a in heads:
        mu = jnp.mean(o[a], -1, keepdims=True)
        d = o[a] - mu
        var = jnp.mean(d * d, -1, keepdims=True)
        on = d * lax.rsqrt(var + LN_EPS) * gn_ref[:, cols[a]]
        o_ref[:, cols[a]] = (on * jax.nn.silu(gate_ref[:, cols[a]])).astype(o_ref.dtype)


def _retention(h, tables, lg_tab, gn_g, state0, out_buf, *, C, HR, n_seq, n_chunks, row0):
    rb0 = row0 // C
    n_hg = H_RET // HR
    W = HR * HEAD_DIM

    def rows(sg, c):
        return rb0 + (sg // n_hg) * n_chunks + c

    def hcol(which):
        return pl.BlockSpec((C, W), lambda sg, c, which=which: (rows(sg, c), which * n_hg + sg % n_hg))

    tab = pl.BlockSpec((C, LANES), lambda sg, c: (rows(sg, c), 0))
    state = pl.BlockSpec((HR, HEAD_DIM, HEAD_DIM), lambda sg, c: (sg, 0, 0))
    in_specs = [hcol(0), hcol(1), hcol(2), hcol(3), tab, tab, tab,
                pl.BlockSpec((HR, 8, lg_tab.shape[2]), lambda sg, c: (sg % n_hg, 0, 0)),
                pl.BlockSpec((1, W), lambda sg, c: (0, sg % n_hg)), state, pl.BlockSpec(memory_space=pl.ANY)]
    state_shape = (n_seq * H_RET, HEAD_DIM, HEAD_DIM)
    return pl.pallas_call(
        functools.partial(_ret_body, C=C, HR=HR),
        grid=(n_seq * n_hg, n_chunks),
        in_specs=in_specs,
        out_specs=[pl.BlockSpec((C, W), lambda sg, c: (rows(sg, c), sg % n_hg)), state],
        out_shape=[jax.ShapeDtypeStruct(out_buf.shape, out_buf.dtype), jax.ShapeDtypeStruct(state_shape, F32)],
        scratch_shapes=[pltpu.VMEM((HR, HEAD_DIM, HEAD_DIM), F32)],
        input_output_aliases={len(in_specs) - 1: 0},
        compiler_params=_params("arbitrary", "arbitrary"),
        name="retention",
    )(h, h, h, h, *tables, lg_tab, gn_g.reshape(1, -1), state0.reshape(state_shape), out_buf)


def _dsa_body(q_ref, iq_ref, iwt_ref, k_ref, v_ref, ik2_ref, *rest, TQ, TK, L, n_sel, q_pos0, q_stride, causal):
    o_ref, key_scr, iqm_scr, q4_scr, m_scr, l_scr, acc_scr = rest[-7:]
    G = DSA_GROUP
    i = pl.program_id(0)
    q_pos = q_pos0 + i * q_stride + lax.broadcasted_iota(I32, (1, TQ), 1)
    q_lim = ((q_pos >> 6) + 1) << 6
    if causal:
        n_tiles = jnp.minimum(((i + 1) * q_stride + TK - 1) // TK, L // TK)
    else:
        n_tiles = L // TK

    lane = lax.broadcasted_iota(I32, (TQ, LANES), 1)
    for j in range(H_IDX // 2):
        pair = iq_ref[:, j * LANES:(j + 1) * LANES]
        iqm_scr[j, :TQ, :] = jnp.where(lane < D_IDX, pair, jnp.zeros_like(pair))
        iqm_scr[j, TQ:, :] = jnp.where(lane >= D_IDX, pair, jnp.zeros_like(pair))
    iwt = iwt_ref[...] * (H_IDX ** -0.5 * D_IDX ** -0.5)

    def key_pos(kt):
        return kt * TK + lax.broadcasted_iota(I32, (TK, TQ), 0)

    def score_tile(kt, _):
        ik2 = ik2_ref[pl.ds(kt * TK, TK), :]
        acc = jnp.zeros((TK, TQ), F32)
        for j in range(H_IDX // 2):
            a = lax.dot_general(ik2, iqm_scr[j], (((1,), (1,)), ((), ())), preferred_element_type=F32)
            acc = (acc + jnp.maximum(a[:, :TQ], 0.0) * iwt[2 * j:2 * j + 1, :]
                   + jnp.maximum(a[:, TQ:], 0.0) * iwt[2 * j + 1:2 * j + 2, :])
        acc = jnp.where(key_pos(kt) < q_lim, acc, -jnp.inf)
        bits = pltpu.bitcast(acc, I32)
        key_scr[pl.ds(kt * TK, TK), :] = bits ^ ((bits >> 31) & INT_MAX)
        return 0

    lax.fori_loop(0, n_tiles, score_tile, 0)

    def bisect(_, carry):
        lo, hi = carry
        mid = lo + lax.shift_right_logical(hi - lo, 1)

        def count_tile(kt, c8):
            ge = (key_scr[pl.ds(kt * TK, TK), :] >= mid).astype(I32)
            return c8 + jnp.sum(ge.reshape(TK // 8, 8, TQ), axis=0)

        c8 = lax.fori_loop(0, n_tiles, count_tile, jnp.zeros((8, TQ), I32))
        enough = jnp.sum(c8, axis=0, keepdims=True) >= n_sel
        return jnp.where(enough, mid, lo), jnp.where(enough, hi, mid)

    thr, _ = lax.fori_loop(0, 32, bisect, (jnp.full((1, TQ), INT_MIN, I32), jnp.full((1, TQ), INT_MAX, I32)))

    for kv in range(KVH_DSA):
        q4_scr[kv] = jnp.concatenate(
            [q_ref[:, (kv * G + g) * HEAD_DIM:(kv * G + g + 1) * HEAD_DIM] for g in range(G)], axis=0)
    m_scr[...] = jnp.full_like(m_scr, NEG_BIG)
    l_scr[...] = jnp.zeros_like(l_scr)
    acc_scr[...] = jnp.zeros_like(acc_scr)

    def attend_tile(kt, _):
        sel = jnp.logical_and(key_scr[pl.ds(kt * TK, TK), :] >= thr, key_pos(kt) < q_lim)
        sel4 = jnp.concatenate([sel] * G, axis=1)
        rows = pl.ds(kt * TK, TK)
        heads = range(KVH_DSA)
        head_cols = [slice(kv * HEAD_DIM, (kv + 1) * HEAD_DIM) for kv in heads]
        logits = [lax.dot_general(k_ref[rows, head_cols[kv]], q4_scr[kv], (((1,), (1,)), ((), ())),
                                  preferred_element_type=F32) for kv in heads]
        probs, alphas = [], []
        for kv in heads:
            lg = jnp.where(sel4, logits[kv], NEG_BIG)
            m_old = m_scr[kv]
            m_new = jnp.maximum(m_old, jnp.max(lg, axis=0, keepdims=True))
            alpha = jnp.exp2((m_old - m_new) * EXP2_SCALE)
            p = jnp.exp2((lg - m_new) * EXP2_SCALE)
            l_scr[kv] = alpha * l_scr[kv] + jnp.sum(p, axis=0, keepdims=True)
            m_scr[kv] = m_new
            probs.append(p.astype(BF16))
            alphas.append(alpha)
        for kv in heads:
            pv = lax.dot_general(v_ref[rows, head_cols[kv]], probs[kv], (((0,), (0,)), ((), ())),
                                 preferred_element_type=F32)
            acc_scr[kv] = acc_scr[kv] * alphas[kv] + pv
        return 0

    lax.fori_loop(0, n_tiles, attend_tile, 0)

    for kv in range(KVH_DSA):
        o_t = acc_scr[kv] / l_scr[kv]
        for g in range(G):
            hd = kv * G + g
            o_ref[:, hd * HEAD_DIM:(hd + 1) * HEAD_DIM] = o_t[:, g * TQ:(g + 1) * TQ].T.astype(o_ref.dtype)


def _dsa(q, iq, iwt, k, v, ik2, out_buf, *, TQ, TK, n_blocks, n_sel, q_pos0, q_stride, causal,
         q_row0, out_row0, out_col0, per_block_keys):
    L = k.shape[-2]
    W = H_DSA * HEAD_DIM
    qb0, ob0, oc0 = q_row0 // TQ, out_row0 // TQ, out_col0 // W
    if per_block_keys:
        kspec = lambda a: pl.BlockSpec((None, L, a.shape[-1]), lambda i: (i, 0, 0))
    else:
        kspec = lambda a: pl.BlockSpec((L, a.shape[-1]), lambda i: (0, 0), pipeline_mode=pl.Buffered(1))
    G = DSA_GROUP
    return pl.pallas_call(
        functools.partial(_dsa_body, TQ=TQ, TK=TK, L=L, n_sel=n_sel, q_pos0=q_pos0, q_stride=q_stride,
                          causal=causal),
        grid=(n_blocks,),
        in_specs=[pl.BlockSpec((TQ, W), lambda i: (qb0 + i, 0)),
                  pl.BlockSpec((TQ, H_IDX * D_IDX), lambda i: (qb0 + i, 0)),
                  pl.BlockSpec((H_IDX, TQ), lambda i: (0, qb0 + i)),
                  kspec(k), kspec(v), kspec(ik2),
                  pl.BlockSpec(memory_space=pl.ANY)],
        out_specs=pl.BlockSpec((TQ, W), lambda i: (ob0 + i, oc0)),
        out_shape=jax.ShapeDtypeStruct(out_buf.shape, out_buf.dtype),
        scratch_shapes=[pltpu.VMEM((L, TQ), I32), pltpu.VMEM((H_IDX // 2, 2 * TQ, LANES), BF16),
                        pltpu.VMEM((KVH_DSA, G * TQ, HEAD_DIM), BF16),
                        pltpu.VMEM((KVH_DSA, 1, G * TQ), F32), pltpu.VMEM((KVH_DSA, 1, G * TQ), F32),
                        pltpu.VMEM((KVH_DSA, HEAD_DIM, G * TQ), F32)],
        input_output_aliases={6: 0},
        compiler_params=_params("arbitrary"),
        name="dsa",
    )(q, iq, iwt, k, v, ik2, out_buf)


def _bias_body(rb_ref, idx_ref, o_ref):
    n = rb_ref.shape[1]
    onehot = (lax.broadcasted_iota(I32, (n, idx_ref.shape[1]), 0) == idx_ref[...]).astype(F32)
    o_ref[...] = jnp.dot(rb_ref[...], onehot, preferred_element_type=F32, precision=lax.Precision.HIGHEST)


def _expand_bias(rel_bias, n_q, n_k, *, tn):
    H, R = rel_bias.shape
    RP = 3 * LANES
    t = np.arange(n_q)[:, None]
    s = np.arange(n_k)[None, :]
    idx = (np.clip(LEFT_CHUNKS * CHUNK + t - s, -REL_CLIP, REL_CLIP) + REL_CLIP).reshape(1, -1).astype(np.int32)
    out = pl.pallas_call(
        _bias_body,
        grid=(idx.shape[1] // tn,),
        in_specs=[pl.BlockSpec((H, RP), lambda j: (0, 0)), pl.BlockSpec((1, tn), lambda j: (0, j))],
        out_specs=pl.BlockSpec((H, tn), lambda j: (0, j)),
        out_shape=jax.ShapeDtypeStruct((H, idx.shape[1]), F32),
        compiler_params=_params("parallel"),
        name="rel_bias_expand",
    )(jnp.pad(rel_bias, ((0, 0), (0, RP - R))), jnp.asarray(idx))
    return out.reshape(H, n_q, n_k)


def _head_cols(ref):
    return lambda hh: ref[:, hh * HEAD_DIM:(hh + 1) * HEAD_DIM]


def _band_heads(q_ref, k_parts, v_parts, bias_ref, ok, o_ref, HB):
    def window(parts, hh):
        return jnp.concatenate([part(hh).astype(BF16) for part in parts], axis=0)

    q_head = _head_cols(q_ref)
    logits = [lax.dot_general(q_head(hh).astype(BF16), window(k_parts, hh), (((1,), (1,)), ((), ())),
                              preferred_element_type=F32) for hh in range(HB)]
    probs = []
    for hh in range(HB):
        s = logits[hh] * HEAD_DIM ** -0.5 + bias_ref[hh]
        if ok is not None:
            s = jnp.where(ok, s, NEG_BIG)
        p = jnp.exp(s - jnp.max(s, -1, keepdims=True))
        probs.append((p / jnp.sum(p, -1, keepdims=True)).astype(BF16))
    for hh in range(HB):
        o_ref[:, hh * HEAD_DIM:(hh + 1) * HEAD_DIM] = jnp.dot(
            probs[hh], window(v_parts, hh), preferred_element_type=F32).astype(o_ref.dtype)


def _band_prompt_body(q_ref, *rest, HB, TQ, NP):
    k_refs, v_refs = rest[:NP], rest[NP:2 * NP]
    bias_ref, o_ref = rest[2 * NP], rest[2 * NP + 2]
    j = pl.program_id(1)
    nk = NP * TQ
    q_pos = j * TQ + lax.broadcasted_iota(I32, (TQ, nk), 0)
    k_pos = (j - (NP - 1)) * TQ + lax.broadcasted_iota(I32, (TQ, nk), 1)
    qc, kc = q_pos >> 6, k_pos >> 6
    ok = jnp.logical_and(jnp.logical_and(k_pos >= 0, kc <= qc), kc >= qc - LEFT_CHUNKS)
    _band_heads(q_ref, [_head_cols(r) for r in k_refs], [_head_cols(r) for r in v_refs], bias_ref, ok, o_ref, HB)


def _band_prompt(qkv, bias, out_buf, *, n_blocks, HB, TQ, NP):
    W = HB * HEAD_DIM
    n_hg = H_CHK // HB

    def kv_spec(r, third):
        return pl.BlockSpec((TQ, W), lambda hg, j: (jnp.maximum(j - (NP - 1) + r, 0), third * n_hg + hg))

    return pl.pallas_call(
        functools.partial(_band_prompt_body, HB=HB, TQ=TQ, NP=NP),
        grid=(n_hg, n_blocks),
        in_specs=([pl.BlockSpec((TQ, W), lambda hg, j: (j, hg))]
                  + [kv_spec(r, 1) for r in range(NP)] + [kv_spec(r, 2) for r in range(NP)]
                  + [pl.BlockSpec((HB, TQ, NP * TQ), lambda hg, j: (hg, 0, 0)), pl.BlockSpec(memory_space=pl.ANY)]),
        out_specs=pl.BlockSpec((TQ, W), lambda hg, j: (j, hg)),
        out_shape=jax.ShapeDtypeStruct(out_buf.shape, out_buf.dtype),
        input_output_aliases={2 * NP + 2: 0},
        compiler_params=_params("parallel", "parallel"),
        name="band_prompt",
    )(qkv, *([qkv] * (2 * NP)), bias, out_buf)


def _band_sample_body(q_ref, kc_ref, kn_ref, vc_ref, vn_ref, bias_ref, _, o_ref, *, HB):
    _band_heads(q_ref, [_head_cols(kc_ref), _head_cols(kn_ref)], [_head_cols(vc_ref), _head_cols(vn_ref)], bias_ref,
                None, o_ref, HB)


def _band_sample(qkv, cache_k, cache_v, bias, out_buf, *, n_seq, T, row0, HB):
    P = cache_k.shape[1]
    W = HB * HEAD_DIM
    n_hg = H_CHK // HB
    rb0 = row0 // T
    new = lambda third: pl.BlockSpec((T, W), lambda b, hg: (rb0 + b, third * n_hg + hg))
    cache = pl.BlockSpec((None, P, W), lambda b, hg: (b, 0, hg))
    return pl.pallas_call(
        functools.partial(_band_sample_body, HB=HB),
        grid=(n_seq, n_hg),
        in_specs=[new(0), cache, new(1), cache, new(2),
                  pl.BlockSpec((HB, T, P + T), lambda b, hg: (hg, 0, 0)), pl.BlockSpec(memory_space=pl.ANY)],
        out_specs=pl.BlockSpec((T, W), lambda b, hg: (rb0 + b, hg)),
        out_shape=jax.ShapeDtypeStruct(out_buf.shape, out_buf.dtype),
        input_output_aliases={6: 0},
        compiler_params=_params("parallel", "parallel"),
        name="band_sample",
    )(qkv, cache_k, qkv, cache_v, qkv, bias, out_buf)


def _rot_tables(pos, head_dim, rot_dim, theta):
    half = rot_dim // 2
    inv = theta ** (-jnp.arange(half, dtype=F32) / half)
    ang = pos.astype(F32)[:, None] * inv[None, :]
    cos, sin = jnp.cos(ang), jnp.sin(ang)
    m = pos.shape[0]
    zh = jnp.zeros((m, half), F32)
    rest0 = jnp.zeros((m, head_dim - rot_dim), F32)
    c = jnp.concatenate([cos, cos, jnp.ones((m, head_dim - rot_dim), F32)], 1)
    s_up = jnp.concatenate([-sin, zh, rest0], 1)
    s_down = jnp.concatenate([zh, sin, rest0], 1)
    reps = LANES // head_dim
    return tuple(jnp.tile(t, (1, reps)) for t in (c, s_up, s_down))


def _log_gamma_table(width):
    lg = jnp.log1p(-(2.0 ** (-5.0 - jnp.arange(H_RET, dtype=F32))))
    return jnp.broadcast_to(lg[:, None, None], (H_RET, 8, width))


def _routing(top_e, sb, rg, row_multiple):
    n = top_e.shape[0]
    a = n * TOP_K
    n_rows = -(-((-(-a // sb) + N_EXPERTS) * sb + rg) // row_multiple) * row_multiple
    flat_e = top_e.reshape(-1)
    onehot = (flat_e[:, None] == jnp.arange(N_EXPERTS, dtype=I32)[None, :]).astype(I32)
    rank = jnp.take_along_axis(jnp.cumsum(onehot, 0), flat_e[:, None], 1)[:, 0] - 1
    counts = jnp.sum(onehot, 0)
    padded = (counts + sb - 1) // sb * sb
    pad_end = jnp.cumsum(padded)
    pad_start = pad_end - padded
    slot = pad_start[flat_e] + rank
    slot_tok = jnp.zeros((n_rows,), I32).at[slot].set(jnp.arange(a, dtype=I32) // TOP_K)

    g_count = (padded + rg - 1) // rg
    g_end = jnp.cumsum(g_count)
    g_start = g_end - g_count

    def groups(n_groups):
        s = jnp.arange(n_groups, dtype=I32)
        live = s < g_end[-1]
        expert = jnp.minimum(jnp.searchsorted(g_end, jnp.where(live, s, g_end[-1] - 1), side='right'), N_EXPERTS - 1)
        k = s - g_start[expert]
        nb = jnp.where(live, jnp.clip(padded[expert] - k * rg, 0, rg) // sb, 0)
        st = jnp.where(live, pad_start[expert] + k * rg, pad_end[-1]) // sb
        return _Groups(expert.astype(I32), st.astype(I32), nb.astype(I32), rg, sb, False)

    own_row = (g_start[flat_e] + rank // rg) * rg + rank % rg
    return own_row.reshape(n, TOP_K).astype(I32), slot_tok, pad_end[-1:].astype(I32), g_end[-1], groups


RG = 2080
RG_NARROW = 1664
TN = 512
TN_PAIR = 256
TN_DOWN = 2048
TK_DOWN = 1024
TR = 208
RG_MOE = 2560
SB_MOE = 64
N_GROUPS_SHORT = N_EXPERTS + 1
RET_CHUNK = 256
RET_HR = 8
DSA_TQ = 128
DSA_TQ_S = 128
DSA_TK = 512
GATHER_TB = 256
BAND_HB = 8
BAND_TQ = 128
BAND_PARTS = (LEFT_CHUNKS * CHUNK) // BAND_TQ + 1


def kernel(x_prompt, x_sample, p_prompt, p_sample, state_ret, cache_dsa_k, cache_dsa_v, cache_dsa_kidx,
           cache_chk_k, cache_chk_v, ln_mix_g, ln_mix_b, ln_ffn_g, ln_ffn_b, ple_proj, ple_gate,
           w_in_even, ret_gn_g, kidx_ln_g, kidx_ln_b, w_out_even, ffn_w_gate, ffn_w_up, ffn_w_down,
           w_in_odd, rel_bias, w_out_odd, router_w, exp_w_gate, exp_w_up, exp_w_down):
    seq, d = x_prompt.shape[1], x_prompt.shape[2]
    nb_s, t_s = x_sample.shape[0], x_sample.shape[1]
    n_s = nb_s * t_s
    m = seq + n_s
    past = cache_dsa_k.shape[2]
    ret_w = H_RET * HEAD_DIM
    dsa_w = H_DSA * HEAD_DIM
    kv_w = KVH_DSA * HEAD_DIM

    x, xb = _stack_rows(x_prompt[0], x_sample.reshape(n_s, d), tr=n_s)
    p = jnp.concatenate([p_prompt[:, 0], p_sample.reshape(DEPTH, n_s, -1)], 1).astype(BF16)
    pos = jnp.concatenate([jnp.arange(seq, dtype=I32), past + jnp.tile(jnp.arange(t_s, dtype=I32), nb_s)])
    tab_ret = _rot_tables(pos, HEAD_DIM, HEAD_DIM, RET_THETA)
    tab_dsa = _rot_tables(pos, HEAD_DIM, ROPE_DIM, ROPE_THETA)
    tab_idx = _rot_tables(pos, D_IDX, IDX_ROPE_DIM, ROPE_THETA)

    tokens = [_dense_groups(m, RG, lead=i) for i in range(DEPTH)]
    tokens_narrow = [_dense_groups(m, RG_NARROW, lead=i, x_buffers=2) for i in range(DEPTH)]
    h = _rg_matmul(tokens[0], xb, jnp.swapaxes(w_in_even, 1, 2), tn=TN, w_transposed=True)
    c_dq = 4 * ret_w
    c_dk = c_dq + dsa_w
    c_dv = c_dk + kv_w
    c_iq = c_dv + kv_w
    c_ik = c_iq + H_IDX * D_IDX

    mix_in = jnp.zeros((m, ret_w + dsa_w), BF16)
    state0_p = jnp.zeros((1, H_RET, HEAD_DIM, HEAD_DIM), F32)
    mix_in, st_p = _retention(h, tab_ret, _log_gamma_table(RET_CHUNK), ret_gn_g[0], state0_p, mix_in,
                              C=RET_CHUNK, HR=RET_HR, n_seq=1, n_chunks=seq // RET_CHUNK, row0=0)
    mix_in, st_s = _retention(h, tab_ret, _log_gamma_table(LANES), ret_gn_g[0], state_ret[0], mix_in,
                              C=t_s, HR=RET_HR, n_seq=nb_s, n_chunks=1, row0=seq)

    (dq,) = _rotary_cols(h, c_dq // dsa_w, dsa_w, tab_dsa, ROPE_DIM // 2, [BF16], tr=TR)
    dk, dk_b = _rotary_cols(h, c_dk // kv_w, kv_w, tab_dsa, ROPE_DIM // 2, [F32, BF16], tr=TR)
    iq_lo, = _rotary_cols(h, c_iq // 1024, 1024, tab_idx, IDX_ROPE_DIM // 2, [BF16], tr=TR)
    iq_hi, = _rotary_cols(h, c_iq // 1024 + 1, 1024, tab_idx, IDX_ROPE_DIM // 2, [BF16], tr=TR)
    iq = jnp.concatenate([iq_lo, iq_hi], 1)
    ik, ik2, iw = _indexer_keys(h, c_ik // LANES, kidx_ln_g[0], kidx_ln_b[0], tab_idx, tr=TR)
    dv, dv_b = _copy_cols(h, c_dv // kv_w, kv_w, [F32, BF16], tr=TR)
    iwt = iw.T

    n_sel_p = min(TOPK_MAX, seq // 4)
    mix_in = _dsa(dq, iq, iwt, dk_b[:seq], dv_b[:seq], ik2[:seq], mix_in, TQ=DSA_TQ, TK=DSA_TK,
                  n_blocks=seq // DSA_TQ, n_sel=n_sel_p, q_pos0=0, q_stride=DSA_TQ, causal=True,
                  q_row0=0, out_row0=0, out_col0=ret_w, per_block_keys=False)

    def pad_queries(a):
        a = a[seq:].reshape(nb_s, t_s, -1)
        return jnp.pad(a, ((0, 0), (0, DSA_TQ_S - t_s), (0, 0))).reshape(nb_s * DSA_TQ_S, -1)

    def with_cache(cache, new):
        return jnp.concatenate([cache.reshape(nb_s, past, -1).astype(BF16), new[seq:].reshape(nb_s, t_s, -1)], 1)

    kidx_c = cache_dsa_kidx[0].astype(BF16)
    l_s = past + t_s
    do_s = _dsa(pad_queries(dq), pad_queries(iq), pad_queries(iw).T,
                with_cache(cache_dsa_k[0], dk_b), with_cache(cache_dsa_v[0], dv_b),
                with_cache(jnp.concatenate([kidx_c, kidx_c], -1), ik2),
                jnp.zeros((nb_s * DSA_TQ_S, dsa_w), BF16), TQ=DSA_TQ_S, TK=l_s, n_blocks=nb_s,
                n_sel=min(TOPK_MAX, l_s // 4), q_pos0=past, q_stride=0, causal=False,
                q_row0=0, out_row0=0, out_col0=0, per_block_keys=True)
    do_s = do_s.reshape(nb_s, DSA_TQ_S, dsa_w)[:, :t_s].reshape(n_s, dsa_w)
    mix_in = lax.dynamic_update_slice(mix_in, do_s, (seq, ret_w))

    mix = _rg_matmul(tokens_narrow[0], mix_in, w_out_even, tn=TN)
    x1, x1b = _residual_layer_norm(x, mix, ln_mix_g[0], ln_mix_b[0], tr=TR)
    act = _rg_swiglu(tokens[0], x1b, ffn_w_gate, ffn_w_up, tn=TN_PAIR)
    ffn = _rg_down(tokens[0], act, ffn_w_down, tn=TN_DOWN, tk=TK_DOWN)
    x2, x2b = _residual_layer_norm(x1, ffn, ln_ffn_g[0], ln_ffn_b[0], tr=TR)
    x3, x3b = _rg_ple(tokens_narrow[0], x2b, ple_gate, p[0], ple_proj, x2, tn=TN_PAIR)

    qkv = _rg_matmul(tokens[0], x3b, w_in_odd, tn=TN, out_dtype=BF16)
    chk_w = H_CHK * HEAD_DIM
    keep = min(LEFT_CHUNKS * CHUNK, seq)
    kv_new = _rg_matmul(_dense_groups(keep + n_s, keep + n_s), x3b[seq - keep:], w_in_odd, tn=TN,
                        stripes=(chk_w // TN, 2 * chk_w // TN))
    p_band = cache_chk_k.shape[2]
    bias_p = _expand_bias(rel_bias[0], BAND_TQ, BAND_PARTS * BAND_TQ, tn=8192)
    bias_s = _expand_bias(rel_bias[0], t_s, p_band + t_s, tn=(t_s * (p_band + t_s)) // 2)
    att = _band_prompt(qkv, bias_p, jnp.zeros((m, chk_w), BF16), n_blocks=seq // BAND_TQ, HB=BAND_HB, TQ=BAND_TQ,
                       NP=BAND_PARTS)
    att = _band_sample(qkv, cache_chk_k[0].reshape(nb_s, p_band, chk_w), cache_chk_v[0].reshape(nb_s, p_band, chk_w),
                       bias_s, att, n_seq=nb_s, T=t_s, row0=seq, HB=BAND_HB)
    mix = _rg_matmul(tokens_narrow[0], att, w_out_odd, tn=TN)
    x4, x4b = _residual_layer_norm(x3, mix, ln_mix_g[1], ln_mix_b[1], tr=TR)

    logits = _rg_matmul(tokens[0], x4b, router_w, tn=N_EXPERTS)
    top_e, gates = _router_top2(logits, tr=TR)
    tok_row, slot_tok, n_slots, n_live_groups, expert_groups = _routing(top_e, SB_MOE, RG_MOE, GATHER_TB)
    xs = _gather_rows(slot_tok, n_slots, x4, tb=GATHER_TB)

    def experts_ffn(n_groups):
        experts = expert_groups(n_groups)
        acts = _rg_swiglu(experts, xs, exp_w_gate[0], exp_w_up[0], tn=TN_PAIR)
        yb = _rg_down(experts, acts, exp_w_down[0], tn=TN_DOWN, tk=TK_DOWN)
        return _moe_combine_ln(tok_row, yb, gates, x4, ln_ffn_g[1], ln_ffn_b[1], tb=TR)

    n_groups_any = N_EXPERTS + (m * TOP_K) // RG_MOE
    x5, x5b = lax.cond(n_live_groups <= N_GROUPS_SHORT, functools.partial(experts_ffn, N_GROUPS_SHORT),
                       functools.partial(experts_ffn, n_groups_any))
    y, _ = _rg_ple(tokens_narrow[1], x5b, ple_gate, p[1], ple_proj, x5, tn=TN_PAIR)

    ck = kv_new[:, :chk_w]
    cv = kv_new[:, chk_w:]

    def heads(a, nh):
        return a.reshape(a.shape[0], nh, HEAD_DIM)

    return (y[:seq][None], y[seq:].reshape(nb_s, t_s, d),
            st_p.reshape(1, 1, H_RET, HEAD_DIM, HEAD_DIM), st_s.reshape(1, nb_s, H_RET, HEAD_DIM, HEAD_DIM),
            heads(dk[:seq], KVH_DSA)[None, None], heads(dv[:seq], KVH_DSA)[None, None], ik[:seq][None, None],
            heads(dk[seq:], KVH_DSA).reshape(1, nb_s, t_s, KVH_DSA, HEAD_DIM),
            heads(dv[seq:], KVH_DSA).reshape(1, nb_s, t_s, KVH_DSA, HEAD_DIM),
            ik[seq:].reshape(1, nb_s, t_s, D_IDX),
            heads(ck[:keep], H_CHK)[None, None], heads(cv[:keep], H_CHK)[None, None],
            heads(ck[keep:], H_CHK).reshape(1, nb_s, t_s, H_CHK, HEAD_DIM),
            heads(cv[keep:], H_CHK).reshape(1, nb_s, t_s, H_CHK, HEAD_DIM))
```

```python
import functools
import math

import numpy as np
import jax
import jax.numpy as jnp
from jax import lax
from jax.experimental import pallas as pl
from jax.experimental.pallas import tpu as pltpu

F32 = jnp.float32
BF16 = jnp.bfloat16
I32 = jnp.int32

CHUNK = 64
HEAD_DIM = 128
H_RET = 16
H_DSA = 16
KVH_DSA = 4
DSA_GROUP = H_DSA // KVH_DSA
H_IDX = 32
D_IDX = 64
TOPK_MAX = 256
H_CHK = 32
LEFT_CHUNKS = 8
REL_CLIP = 128
N_EXPERTS = 8
TOP_K = 2
RET_THETA = 10000.0
ROPE_THETA = 500000.0
ROPE_DIM = HEAD_DIM // 4
IDX_ROPE_DIM = D_IDX // 4
LN_EPS = 1e-5
DEPTH = 2
ALPHA = (2.0 * DEPTH) ** 0.25

LANES = 128
DMA_PRIORITIES = 2
MAX_PIECE = 8
VMEM_LIMIT = 56 * 1024 * 1024
NEG_BIG = -1e30
EXP2_SCALE = HEAD_DIM ** -0.5 * math.log2(math.e)
INT_MIN = -(2 ** 31)
INT_MAX = 2 ** 31 - 1


def _params(*sem):
    return pltpu.CompilerParams(dimension_semantics=sem, vmem_limit_bytes=VMEM_LIMIT)


class _Groups:
    def __init__(self, e, st, nb, rg, sb, dense, x_buffers=1):
        self.e, self.st, self.nb, self.rg, self.sb, self.dense, self.x_buffers = e, st, nb, rg, sb, dense, x_buffers
        self.n = e.shape[0]

    def rows(self, width, col, buffers=None, own=False):
        kw = {} if buffers is None else dict(pipeline_mode=pl.Buffered(buffers))
        if own:
            return pl.BlockSpec((self.rg, width), lambda s, *a: (s, col(a[-1][s] > 0, *a[:-3])), **kw)
        if self.dense:
            return pl.BlockSpec((self.rg, width), lambda s, *a: (a[-2][s], col(a[-1][s] > 0, *a[:-3])), **kw)
        unit = self.sb
        return pl.BlockSpec((pl.Element(self.rg), pl.Element(width)),
                            lambda s, *a: (a[-2][s] * unit, col(a[-1][s] > 0, *a[:-3]) * width), **kw)

    def weights(self, k_rows, width, kcol):
        return pl.BlockSpec((None, k_rows, width), lambda s, *a: (a[-3][s], *kcol(a[-1][s] > 0, *a[:-3])))


def _dense_groups(m, rg, lead=0, x_buffers=1):
    n = m // rg
    return _Groups(jnp.full((n,), lead, I32), jnp.arange(n, dtype=I32), jnp.ones((n,), I32), rg, rg, True, x_buffers)


def _live_rows(nb_ref, n_sub, sb, compute):
    nblk = nb_ref[pl.program_id(0)]
    if n_sub == 1:
        @pl.when(nblk > 0)
        def _():
            compute(pl.ds(0, sb))

        return nblk

    def run_full(r, c):
        rows = MAX_PIECE * sb
        compute(pl.ds(pl.multiple_of(r * rows, rows), rows))
        return c

    lax.fori_loop(0, nblk // MAX_PIECE, run_full, 0)
    piece = MAX_PIECE // 2
    while piece >= 1:
        def run_piece(piece=piece):
            start = (nblk // (2 * piece)) * (2 * piece * sb)
            compute(pl.ds(pl.multiple_of(start, 2 * piece * sb), piece * sb))

        pl.when((nblk // piece) % 2 == 1)(run_piece)
        piece //= 2
    return nblk


def _row_loops(nb_ref, n_sub, sb, compute, o_refs):
    nblk = _live_rows(nb_ref, n_sub, sb, compute)

    def clear(r, c):
        for o_ref in o_refs:
            o_ref[pl.ds(pl.multiple_of(r * sb, sb), sb), :] = jnp.zeros((sb, o_ref.shape[1]), o_ref.dtype)
        return c

    lax.fori_loop(nblk, n_sub, clear, 0)


def _bf16_dot(x, w_ref):
    return jnp.dot(x, w_ref[...].astype(BF16), preferred_element_type=F32)


def _rg_mm_body(e_ref, st_ref, nb_ref, x_ref, w_ref, o_ref, *, sb, w_transposed):
    def compute(rows):
        if w_transposed:
            y = lax.dot_general(x_ref[rows, :], w_ref[...].astype(BF16), (((1,), (1,)), ((), ())),
                                preferred_element_type=F32)
        else:
            y = _bf16_dot(x_ref[rows, :], w_ref)
        o_ref[rows, :] = y.astype(o_ref.dtype)

    _row_loops(nb_ref, x_ref.shape[0] // sb, sb, compute, [o_ref])


def _rg_swiglu_body(e_ref, st_ref, nb_ref, x_ref, wg_ref, wu_ref, o_ref, *, sb):
    def compute(rows):
        x = x_ref[rows, :]
        g = _bf16_dot(x, wg_ref)
        u = _bf16_dot(x, wu_ref)
        o_ref[rows, :] = (jax.nn.silu(g) * u).astype(o_ref.dtype)

    _row_loops(nb_ref, x_ref.shape[0] // sb, sb, compute, [o_ref])


def _rg_ple_body(e_ref, st_ref, nb_ref, x_ref, w_ref, p_ref, pw_ref, r_ref, of_ref, ob_ref, *, sb):
    def compute(rows):
        gate = _bf16_dot(x_ref[rows, :], w_ref)
        proj = _bf16_dot(p_ref[rows, :], pw_ref)
        y = r_ref[rows, :] + jax.nn.sigmoid(gate) * proj
        of_ref[rows, :] = y
        ob_ref[rows, :] = y.astype(BF16)

    _row_loops(nb_ref, x_ref.shape[0] // sb, sb, compute, [of_ref, ob_ref])


def _rg_down_body(e_ref, st_ref, nb_ref, a_ref, w_ref, o_ref, *, sb):
    @pl.when(pl.program_id(2) == 0)
    def _():
        o_ref[...] = jnp.zeros_like(o_ref)

    def compute(rows):
        o_ref[rows, :] += _bf16_dot(a_ref[rows, :], w_ref)

    _live_rows(nb_ref, a_ref.shape[0] // sb, sb, compute)


def _last_if_dead(n_blocks):
    return lambda live, j: jnp.where(live, j, n_blocks - 1)


def _rg_matmul(g, x, w, *, tn, out_dtype=F32, stripes=None, w_transposed=False):
    K = x.shape[1]
    n_w = w.shape[-2] if w_transposed else w.shape[-1]
    j0, J = (0, pl.cdiv(n_w, tn)) if stripes is None else stripes
    N = n_w if stripes is None else J * tn
    col = _last_if_dead(J)
    if w_transposed:
        w_spec = g.weights(tn, K, lambda live, j: (j0 + col(live, j), 0))
    else:
        w_spec = g.weights(K, tn, lambda live, j: (0, j0 + col(live, j)))
    return pl.pallas_call(
        functools.partial(_rg_mm_body, sb=g.sb, w_transposed=w_transposed),
        grid_spec=pltpu.PrefetchScalarGridSpec(
            num_scalar_prefetch=3, grid=(g.n, J),
            in_specs=[g.rows(K, lambda live, j: 0, buffers=g.x_buffers), w_spec],
            out_specs=g.rows(tn, lambda live, j: j, own=True)),
        out_shape=jax.ShapeDtypeStruct((g.n * g.rg, N), out_dtype),
        compiler_params=_params("arbitrary", "arbitrary"),
        name="rg_mm",
    )(g.e, g.st, g.nb, x, w)


def _rg_swiglu(g, x, wg, wu, *, tn):
    K = x.shape[1]
    N = wg.shape[-1]
    col = _last_if_dead(N // tn)
    wspec = g.weights(K, tn, lambda live, j: (0, col(live, j)))
    return pl.pallas_call(
        functools.partial(_rg_swiglu_body, sb=g.sb),
        grid_spec=pltpu.PrefetchScalarGridSpec(
            num_scalar_prefetch=3, grid=(g.n, N // tn),
            in_specs=[g.rows(K, lambda live, j: 0, buffers=g.x_buffers), wspec, wspec],
            out_specs=g.rows(tn, lambda live, j: j, own=True)),
        out_shape=jax.ShapeDtypeStruct((g.n * g.rg, N), BF16),
        compiler_params=_params("arbitrary", "arbitrary"),
        name="rg_swiglu",
    )(g.e, g.st, g.nb, x, wg, wu)


def _rg_ple(g, x, w, p, pw, resid, *, tn):
    K, KP = x.shape[1], p.shape[1]
    N = w.shape[-1]
    col = _last_if_dead(N // tn)
    wcol = lambda live, j: (0, col(live, j))
    tile = g.rows(tn, lambda live, j: j, own=True)
    return pl.pallas_call(
        functools.partial(_rg_ple_body, sb=g.sb),
        grid_spec=pltpu.PrefetchScalarGridSpec(
            num_scalar_prefetch=3, grid=(g.n, N // tn),
            in_specs=[g.rows(K, lambda live, j: 0, buffers=g.x_buffers), g.weights(K, tn, wcol),
                      g.rows(KP, lambda live, j: 0, buffers=g.x_buffers), g.weights(KP, tn, wcol), tile],
            out_specs=[tile, tile]),
        out_shape=[jax.ShapeDtypeStruct((g.n * g.rg, N), F32), jax.ShapeDtypeStruct((g.n * g.rg, N), BF16)],
        compiler_params=_params("arbitrary", "arbitrary"),
        name="rg_ple",
    )(g.e, g.st, g.nb, x, w, p, pw, resid)


def _rg_down(g, a, w, *, tn, tk):
    K = a.shape[1]
    N = w.shape[-1]
    NT, KC = N // tn, K // tk
    ncol = lambda live, n, kc: jnp.where(live, n, NT - 1)
    kcol = lambda live, n, kc: jnp.where(live, kc, KC - 1)
    return pl.pallas_call(
        functools.partial(_rg_down_body, sb=g.sb),
        grid_spec=pltpu.PrefetchScalarGridSpec(
            num_scalar_prefetch=3, grid=(g.n, NT, KC),
            in_specs=[g.rows(tk, kcol, own=True),
                      g.weights(tk, tn, lambda live, n, kc: (kcol(live, n, kc), ncol(live, n, kc)))],
            out_specs=g.rows(tn, lambda live, n, kc: n, buffers=1, own=True)),
        out_shape=jax.ShapeDtypeStruct((g.n * g.rg, N), F32),
        compiler_params=_params("arbitrary", "arbitrary", "arbitrary"),
        name="rg_down",
    )(g.e, g.st, g.nb, a, w)


def _layer_norm_rows(z, g, b):
    mu = jnp.mean(z, -1, keepdims=True)
    d = z - mu
    var = jnp.mean(d * d, -1, keepdims=True)
    return d * lax.rsqrt(var + LN_EPS) * g + b


def _ln_body(a_ref, b_ref, g_ref, beta_ref, of_ref, ob_ref):
    y = _layer_norm_rows(ALPHA * a_ref[...] + b_ref[...], g_ref[...], beta_ref[...])
    of_ref[...] = y
    ob_ref[...] = y.astype(BF16)


def _residual_layer_norm(a, b, g, beta, *, tr):
    M, D = a.shape
    row = pl.BlockSpec((tr, D), lambda i: (i, 0))
    vec = pl.BlockSpec((1, D), lambda i: (0, 0))
    return pl.pallas_call(
        _ln_body,
        grid=(M // tr,),
        in_specs=[row, row, vec, vec],
        out_specs=[row, row],
        out_shape=[jax.ShapeDtypeStruct((M, D), F32), jax.ShapeDtypeStruct((M, D), BF16)],
        compiler_params=_params("parallel"),
        name="res_ln",
    )(a, b, g.reshape(1, D), beta.reshape(1, D))


def _stack_body(a_ref, b_ref, of_ref, ob_ref, *, n_a):
    def emit(src_ref):
        y = src_ref[...]
        of_ref[...] = y
        ob_ref[...] = y.astype(BF16)

    pl.when(pl.program_id(0) < n_a)(functools.partial(emit, a_ref))
    pl.when(pl.program_id(0) >= n_a)(functools.partial(emit, b_ref))


def _stack_rows(a, b, *, tr):
    n_a, n_b = a.shape[0] // tr, b.shape[0] // tr
    D = a.shape[1]
    row = pl.BlockSpec((tr, D), lambda i: (i, 0))
    return pl.pallas_call(
        functools.partial(_stack_body, n_a=n_a),
        grid=(n_a + n_b,),
        in_specs=[pl.BlockSpec((tr, D), lambda i: (jnp.minimum(i, n_a - 1), 0)),
                  pl.BlockSpec((tr, D), lambda i: (jnp.maximum(i - n_a, 0), 0))],
        out_specs=[row, row],
        out_shape=[jax.ShapeDtypeStruct((a.shape[0] + b.shape[0], D), F32),
                   jax.ShapeDtypeStruct((a.shape[0] + b.shape[0], D), BF16)],
        compiler_params=_params("arbitrary"),
        name="stack_rows",
    )(a, b)


def _rotate(x, c, s_up, s_down, half):
    w = x.shape[1]
    reps = w // LANES
    if reps > 1:
        c, s_up, s_down = (jnp.concatenate([t] * reps, axis=1) for t in (c, s_up, s_down))
    return x * c + pltpu.roll(x, w - half, 1) * s_up + pltpu.roll(x, half, 1) * s_down


def _rot_body(x_ref, c_ref, su_ref, sd_ref, *o_refs, half):
    y = _rotate(x_ref[...], c_ref[...], su_ref[...], sd_ref[...], half)
    for o_ref in o_refs:
        o_ref[...] = y.astype(o_ref.dtype)


def _rotary_cols(h, col_block, width, tables, half, out_dtypes, *, tr, n_blocks=1):
    M = h.shape[0]
    tab = pl.BlockSpec((tr, LANES), lambda i, c: (i, 0))
    return pl.pallas_call(
        functools.partial(_rot_body, half=half),
        grid=(M // tr, n_blocks),
        in_specs=[pl.BlockSpec((tr, width), lambda i, c: (i, col_block + c)), tab, tab, tab],
        out_specs=[pl.BlockSpec((tr, width), lambda i, c: (i, c)) for _ in out_dtypes],
        out_shape=[jax.ShapeDtypeStruct((M, width * n_blocks), dt) for dt in out_dtypes],
        compiler_params=_params("parallel", "parallel"),
        name="rotary",
    )(h, *tables)


def _copy_body(x_ref, *o_refs):
    for o_ref in o_refs:
        o_ref[...] = x_ref[...].astype(o_ref.dtype)


def _copy_cols(h, col_block, width, out_dtypes, *, tr):
    M = h.shape[0]
    return pl.pallas_call(
        _copy_body,
        grid=(M // tr,),
        in_specs=[pl.BlockSpec((tr, width), lambda i: (i, col_block))],
        out_specs=[pl.BlockSpec((tr, width), lambda i: (i, 0)) for _ in out_dtypes],
        out_shape=[jax.ShapeDtypeStruct((M, width), dt) for dt in out_dtypes],
        compiler_params=_params("parallel"),
        name="copy_cols",
    )(h)


def _ik_body(x_ref, g_ref, b_ref, c_ref, su_ref, sd_ref, ik_ref, ik2_ref, iw_ref):
    x = x_ref[...]
    is_key = lax.broadcasted_iota(I32, x.shape, 1) < D_IDX
    mu = jnp.sum(jnp.where(is_key, x, 0.0), -1, keepdims=True) / D_IDX
    d = jnp.where(is_key, x - mu, 0.0)
    var = jnp.sum(d * d, -1, keepdims=True) / D_IDX
    y = jnp.where(is_key, d * lax.rsqrt(var + LN_EPS) * g_ref[...] + b_ref[...], 0.0)
    y = _rotate(y, c_ref[...], su_ref[...], sd_ref[...], IDX_ROPE_DIM // 2)
    y = jnp.where(is_key, y, 0.0)
    ik_ref[...] = y[:, :D_IDX]
    ik2_ref[...] = (y + pltpu.roll(y, D_IDX, 1)).astype(BF16)
    iw_ref[...] = pltpu.roll(x, LANES - D_IDX, 1)[:, :H_IDX]


def _indexer_keys(h, col_block, g, b, tables, *, tr):
    M = h.shape[0]
    pad = LANES - D_IDX
    tab = pl.BlockSpec((tr, LANES), lambda i: (i, 0))
    vec = pl.BlockSpec((1, LANES), lambda i: (0, 0))
    return pl.pallas_call(
        _ik_body,
        grid=(M // tr,),
        in_specs=[pl.BlockSpec((tr, LANES), lambda i: (i, col_block)), vec, vec, tab, tab, tab],
        out_specs=[pl.BlockSpec((tr, D_IDX), lambda i: (i, 0)), pl.BlockSpec((tr, LANES), lambda i: (i, 0)),
                   pl.BlockSpec((tr, H_IDX), lambda i: (i, 0))],
        out_shape=[jax.ShapeDtypeStruct((M, D_IDX), F32), jax.ShapeDtypeStruct((M, LANES), BF16),
                   jax.ShapeDtypeStruct((M, H_IDX), F32)],
        compiler_params=_params("parallel"),
        name="indexer_keys",
    )(h, jnp.pad(g, (0, pad)).reshape(1, LANES), jnp.pad(b, (0, pad)).reshape(1, LANES), *tables)


def _router_body(l_ref, e_ref, g_ref):
    lg = l_ref[...]
    idx = lax.broadcasted_iota(I32, lg.shape, 1)
    m1 = jnp.max(lg, -1, keepdims=True)
    e1 = jnp.min(jnp.where(lg == m1, idx, N_EXPERTS), -1, keepdims=True)
    rest = jnp.where(idx == e1, -jnp.inf, lg)
    m2 = jnp.max(rest, -1, keepdims=True)
    e2 = jnp.min(jnp.where(rest == m2, idx, N_EXPERTS), -1, keepdims=True)
    ex2 = jnp.exp(m2 - m1)
    den = 1.0 + ex2
    e_ref[...] = jnp.concatenate([e1, e2], axis=1)
    g_ref[...] = jnp.concatenate([1.0 / den, ex2 / den], axis=1)


def _router_top2(logits, *, tr):
    M, E = logits.shape
    return pl.pallas_call(
        _router_body,
        grid=(M // tr,),
        in_specs=[pl.BlockSpec((tr, E), lambda i: (i, 0))],
        out_specs=[pl.BlockSpec((tr, TOP_K), lambda i: (i, 0)), pl.BlockSpec((tr, TOP_K), lambda i: (i, 0))],
        out_shape=[jax.ShapeDtypeStruct((M, TOP_K), I32), jax.ShapeDtypeStruct((M, TOP_K), F32)],
        compiler_params=_params("parallel"),
        name="router_top2",
    )(logits)


def _row_copy(src_hbm, row, dst, r, sem):
    return pltpu.make_async_copy(src_hbm.at[pl.ds(row, 1), :], dst.at[pl.ds(r, 1), :], sem)


def _gather_body(tok_ref, n_live_ref, x_hbm, o_ref, buf, sem, *, tb):
    base = pl.program_id(0) * tb

    def start(r2, _):
        for u in range(DMA_PRIORITIES):
            r = r2 * DMA_PRIORITIES + u
            _row_copy(x_hbm, tok_ref[base + r], buf, r, sem).start(priority=u)
        return 0

    def wait(r, _):
        _row_copy(x_hbm, 0, buf, r, sem).wait()
        return 0

    @pl.when(base < n_live_ref[0])
    def _():
        lax.fori_loop(0, tb // DMA_PRIORITIES, start, 0)
        lax.fori_loop(0, tb, wait, 0)
        o_ref[...] = buf[...].astype(BF16)

    @pl.when(base >= n_live_ref[0])
    def _():
        o_ref[...] = jnp.zeros_like(o_ref)


def _gather_rows(slot_tok, n_live, x, *, tb):
    A = slot_tok.shape[0]
    D = x.shape[1]
    return pl.pallas_call(
        functools.partial(_gather_body, tb=tb),
        grid_spec=pltpu.PrefetchScalarGridSpec(
            num_scalar_prefetch=2,
            grid=(A // tb,),
            in_specs=[pl.BlockSpec(memory_space=pl.ANY)],
            out_specs=pl.BlockSpec((tb, D), lambda i, tok, n_live: (i, 0)),
            scratch_shapes=[pltpu.VMEM((tb, D), F32), pltpu.SemaphoreType.DMA(())]),
        out_shape=jax.ShapeDtypeStruct((A, D), BF16),
        compiler_params=_params("arbitrary"),
        name="moe_gather",
    )(slot_tok, n_live, x)


def _combine_body(slot_ref, y_hbm, gate_ref, x_ref, g_ref, beta_ref, of_ref, ob_ref, buf, sem, *, tb):
    base = pl.program_id(0) * tb

    def start(r, _):
        for k in range(TOP_K):
            _row_copy(y_hbm, slot_ref[(base + r) * TOP_K + k], buf.at[k], r, sem).start(priority=k % DMA_PRIORITIES)
        return 0

    def wait(r, _):
        for k in range(TOP_K):
            _row_copy(y_hbm, 0, buf.at[k], r, sem).wait()
        return 0

    lax.fori_loop(0, tb, start, 0)
    lax.fori_loop(0, tb, wait, 0)
    gates = gate_ref[...]
    y = buf[0] * gates[:, 0:1] + buf[1] * gates[:, 1:2]
    z = _layer_norm_rows(ALPHA * x_ref[...] + y, g_ref[...], beta_ref[...])
    of_ref[...] = z
    ob_ref[...] = z.astype(BF16)


def _moe_combine_ln(tok_slot, yb, gates, x, g, beta, *, tb):
    M, D = x.shape
    row = pl.BlockSpec((tb, D), lambda i, s: (i, 0))
    vec = pl.BlockSpec((1, D), lambda i, s: (0, 0))
    return pl.pallas_call(
        functools.partial(_combine_body, tb=tb),
        grid_spec=pltpu.PrefetchScalarGridSpec(
            num_scalar_prefetch=1,
            grid=(M // tb,),
            in_specs=[pl.BlockSpec(memory_space=pl.ANY), pl.BlockSpec((tb, TOP_K), lambda i, s: (i, 0)),
                      row, vec, vec],
            out_specs=[row, row],
            scratch_shapes=[pltpu.VMEM((TOP_K, tb, D), F32), pltpu.SemaphoreType.DMA(())]),
        out_shape=[jax.ShapeDtypeStruct((M, D), F32), jax.ShapeDtypeStruct((M, D), BF16)],
        compiler_params=_params("arbitrary"),
        name="moe_combine",
    )(tok_slot.reshape(-1), yb, gates, x, g.reshape(1, D), beta.reshape(1, D))


def _ret_body(q_ref, k_ref, v_ref, gate_ref, c_ref, su_ref, sd_ref, lg_ref, gn_ref, s0_ref, _, o_ref, st_ref,
              s_scr, *, C, HR):
    @pl.when(pl.program_id(1) == 0)
    def _():
        s_scr[...] = s0_ref[...]

    heads = range(HR)
    cols = [slice(a * HEAD_DIM, (a + 1) * HEAD_DIM) for a in heads]
    c, su, sd = c_ref[...], su_ref[...], sd_ref[...]
    n_col = lax.broadcasted_iota(I32, (C, C), 0)
    m_row = lax.broadcasted_iota(I32, (C, C), 1)
    diff = (n_col - m_row).astype(F32)
    n_idx = lax.broadcasted_iota(I32, (C, LANES), 0).astype(F32)

    qb, k, vb, lg_lane = [], [], [], []
    for a in heads:
        qb.append(_rotate(q_ref[:, cols[a]], c, su, sd, HEAD_DIM // 2).astype(BF16))
        k.append(_rotate(k_ref[:, cols[a]], c, su, sd, HEAD_DIM // 2) * HEAD_DIM ** -0.5)
        vb.append(v_ref[:, cols[a]].astype(BF16))
        lg_lane.append(lg_ref[a, 0:1, :LANES])
    s = []
    for a in heads:
        intra = jnp.where(diff >= 0, jnp.exp(lg_ref[a, 0:1, :C] * jnp.maximum(diff, 0.0)), 0.0)
        s.append(lax.dot_general(qb[a], k[a].astype(BF16), (((1,), (1,)), ((), ())),
                                 preferred_element_type=F32) * intra)
    state = [s_scr[a] for a in heads]
    o = [jnp.dot(s[a].astype(BF16), vb[a], preferred_element_type=F32)
         + jnp.dot(qb[a], state[a].astype(BF16), preferred_element_type=F32) * jnp.exp(lg_lane[a] * (n_idx + 1.0))
         for a in heads]
    for a in heads:
        kd = (k[a] * jnp.exp(lg_lane[a] * (C - 1.0 - n_idx))).astype(BF16)
        new = state[a] * jnp.exp(lg_lane[a] * C) + lax.dot_general(kd, vb[a], (((0,), (0,)), ((), ())),
                                                                    preferred_element_type=F32)
        s_scr[a] = new
        st_ref[a] = new
    for a in heads:
        mu = jnp.mean(o[a], -1, keepdims=True)
        d = o[a] - mu
        var = jnp.mean(d * d, -1, keepdims=True)
        on = d * lax.rsqrt(var + LN_EPS) * gn_ref[:, cols[a]]
        o_ref[:, cols[a]] = (on * jax.nn.silu(gate_ref[:, cols[a]])).astype(o_ref.dtype)


def _retention(h, tables, lg_tab, gn_g, state0, out_buf, *, C, HR, n_seq, n_chunks, row0):
    rb0 = row0 // C
    n_hg = H_RET // HR
    W = HR * HEAD_DIM

    def rows(sg, c):
        return rb0 + (sg // n_hg) * n_chunks + c

    def hcol(which):
        return pl.BlockSpec((C, W), lambda sg, c, which=which: (rows(sg, c), which * n_hg + sg % n_hg))

    tab = pl.BlockSpec((C, LANES), lambda sg, c: (rows(sg, c), 0))
    state = pl.BlockSpec((HR, HEAD_DIM, HEAD_DIM), lambda sg, c: (sg, 0, 0))
    in_specs = [hcol(0), hcol(1), hcol(2), hcol(3), tab, tab, tab,
                pl.BlockSpec((HR, 8, lg_tab.shape[2]), lambda sg, c: (sg % n_hg, 0, 0)),
                pl.BlockSpec((1, W), lambda sg, c: (0, sg % n_hg)), state, pl.BlockSpec(memory_space=pl.ANY)]
    state_shape = (n_seq * H_RET, HEAD_DIM, HEAD_DIM)
    return pl.pallas_call(
        functools.partial(_ret_body, C=C, HR=HR),
        grid=(n_seq * n_hg, n_chunks),
        in_specs=in_specs,
        out_specs=[pl.BlockSpec((C, W), lambda sg, c: (rows(sg, c), sg % n_hg)), state],
        out_shape=[jax.ShapeDtypeStruct(out_buf.shape, out_buf.dtype), jax.ShapeDtypeStruct(state_shape, F32)],
        scratch_shapes=[pltpu.VMEM((HR, HEAD_DIM, HEAD_DIM), F32)],
        input_output_aliases={len(in_specs) - 1: 0},
        compiler_params=_params("arbitrary", "arbitrary"),
        name="retention",
    )(h, h, h, h, *tables, lg_tab, gn_g.reshape(1, -1), state0.reshape(state_shape), out_buf)


def _dsa_body(q_ref, iq_ref, iwt_ref, k_ref, v_ref, ik2_ref, *rest, TQ, TK, L, n_sel, q_pos0, q_stride, causal):
    o_ref, key_scr, iqm_scr, q4_scr, m_scr, l_scr, acc_scr = rest[-7:]
    G = DSA_GROUP
    i = pl.program_id(0)
    q_pos = q_pos0 + i * q_stride + lax.broadcasted_iota(I32, (1, TQ), 1)
    q_lim = ((q_pos >> 6) + 1) << 6
    if causal:
        n_tiles = jnp.minimum(((i + 1) * q_stride + TK - 1) // TK, L // TK)
    else:
        n_tiles = L // TK

    lane = lax.broadcasted_iota(I32, (TQ, LANES), 1)
    for j in range(H_IDX // 2):
        pair = iq_ref[:, j * LANES:(j + 1) * LANES]
        iqm_scr[j, :TQ, :] = jnp.where(lane < D_IDX, pair, jnp.zeros_like(pair))
        iqm_scr[j, TQ:, :] = jnp.where(lane >= D_IDX, pair, jnp.zeros_like(pair))
    iwt = iwt_ref[...] * (H_IDX ** -0.5 * D_IDX ** -0.5)

    def key_pos(kt):
        return kt * TK + lax.broadcasted_iota(I32, (TK, TQ), 0)

    def score_tile(kt, _):
        ik2 = ik2_ref[pl.ds(kt * TK, TK), :]
        acc = jnp.zeros((TK, TQ), F32)
        for j in range(H_IDX // 2):
            a = lax.dot_general(ik2, iqm_scr[j], (((1,), (1,)), ((), ())), preferred_element_type=F32)
            acc = (acc + jnp.maximum(a[:, :TQ], 0.0) * iwt[2 * j:2 * j + 1, :]
                   + jnp.maximum(a[:, TQ:], 0.0) * iwt[2 * j + 1:2 * j + 2, :])
        acc = jnp.where(key_pos(kt) < q_lim, acc, -jnp.inf)
        bits = pltpu.bitcast(acc, I32)
        key_scr[pl.ds(kt * TK, TK), :] = bits ^ ((bits >> 31) & INT_MAX)
        return 0

    lax.fori_loop(0, n_tiles, score_tile, 0)

    def bisect(_, carry):
        lo, hi = carry
        mid = lo + lax.shift_right_logical(hi - lo, 1)

        def count_tile(kt, c8):
            ge = (key_scr[pl.ds(kt * TK, TK), :] >= mid).astype(I32)
            return c8 + jnp.sum(ge.reshape(TK // 8, 8, TQ), axis=0)

        c8 = lax.fori_loop(0, n_tiles, count_tile, jnp.zeros((8, TQ), I32))
        enough = jnp.sum(c8, axis=0, keepdims=True) >= n_sel
        return jnp.where(enough, mid, lo), jnp.where(enough, hi, mid)

    thr, _ = lax.fori_loop(0, 32, bisect, (jnp.full((1, TQ), INT_MIN, I32), jnp.full((1, TQ), INT_MAX, I32)))

    for kv in range(KVH_DSA):
        q4_scr[kv] = jnp.concatenate(
            [q_ref[:, (kv * G + g) * HEAD_DIM:(kv * G + g + 1) * HEAD_DIM] for g in range(G)], axis=0)
    m_scr[...] = jnp.full_like(m_scr, NEG_BIG)
    l_scr[...] = jnp.zeros_like(l_scr)
    acc_scr[...] = jnp.zeros_like(acc_scr)

    def attend_tile(kt, _):
        sel = jnp.logical_and(key_scr[pl.ds(kt * TK, TK), :] >= thr, key_pos(kt) < q_lim)
        sel4 = jnp.concatenate([sel] * G, axis=1)
        rows = pl.ds(kt * TK, TK)
        heads = range(KVH_DSA)
        head_cols = [slice(kv * HEAD_DIM, (kv + 1) * HEAD_DIM) for kv in heads]
        logits = [lax.dot_general(k_ref[rows, head_cols[kv]], q4_scr[kv], (((1,), (1,)), ((), ())),
                                  preferred_element_type=F32) for kv in heads]
        probs, alphas = [], []
        for kv in heads:
            lg = jnp.where(sel4, logits[kv], NEG_BIG)
            m_old = m_scr[kv]
            m_new = jnp.maximum(m_old, jnp.max(lg, axis=0, keepdims=True))
            alpha = jnp.exp2((m_old - m_new) * EXP2_SCALE)
            p = jnp.exp2((lg - m_new) * EXP2_SCALE)
            l_scr[kv] = alpha * l_scr[kv] + jnp.sum(p, axis=0, keepdims=True)
            m_scr[kv] = m_new
            probs.append(p.astype(BF16))
            alphas.append(alpha)
        for kv in heads:
            pv = lax.dot_general(v_ref[rows, head_cols[kv]], probs[kv], (((0,), (0,)), ((), ())),
                                 preferred_element_type=F32)
            acc_scr[kv] = acc_scr[kv] * alphas[kv] + pv
        return 0

    lax.fori_loop(0, n_tiles, attend_tile, 0)

    for kv in range(KVH_DSA):
        o_t = acc_scr[kv] / l_scr[kv]
        for g in range(G):
            hd = kv * G + g
            o_ref[:, hd * HEAD_DIM:(hd + 1) * HEAD_DIM] = o_t[:, g * TQ:(g + 1) * TQ].T.astype(o_ref.dtype)


def _dsa(q, iq, iwt, k, v, ik2, out_buf, *, TQ, TK, n_blocks, n_sel, q_pos0, q_stride, causal,
         q_row0, out_row0, out_col0, per_block_keys):
    L = k.shape[-2]
    W = H_DSA * HEAD_DIM
    qb0, ob0, oc0 = q_row0 // TQ, out_row0 // TQ, out_col0 // W
    if per_block_keys:
        kspec = lambda a: pl.BlockSpec((None, L, a.shape[-1]), lambda i: (i, 0, 0))
    else:
        kspec = lambda a: pl.BlockSpec((L, a.shape[-1]), lambda i: (0, 0), pipeline_mode=pl.Buffered(1))
    G = DSA_GROUP
    return pl.pallas_call(
        functools.partial(_dsa_body, TQ=TQ, TK=TK, L=L, n_sel=n_sel, q_pos0=q_pos0, q_stride=q_stride,
                          causal=causal),
        grid=(n_blocks,),
        in_specs=[pl.BlockSpec((TQ, W), lambda i: (qb0 + i, 0)),
                  pl.BlockSpec((TQ, H_IDX * D_IDX), lambda i: (qb0 + i, 0)),
                  pl.BlockSpec((H_IDX, TQ), lambda i: (0, qb0 + i)),
                  kspec(k), kspec(v), kspec(ik2),
                  pl.BlockSpec(memory_space=pl.ANY)],
        out_specs=pl.BlockSpec((TQ, W), lambda i: (ob0 + i, oc0)),
        out_shape=jax.ShapeDtypeStruct(out_buf.shape, out_buf.dtype),
        scratch_shapes=[pltpu.VMEM((L, TQ), I32), pltpu.VMEM((H_IDX // 2, 2 * TQ, LANES), BF16),
                        pltpu.VMEM((KVH_DSA, G * TQ, HEAD_DIM), BF16),
                        pltpu.VMEM((KVH_DSA, 1, G * TQ), F32), pltpu.VMEM((KVH_DSA, 1, G * TQ), F32),
                        pltpu.VMEM((KVH_DSA, HEAD_DIM, G * TQ), F32)],
        input_output_aliases={6: 0},
        compiler_params=_params("arbitrary"),
        name="dsa",
    )(q, iq, iwt, k, v, ik2, out_buf)


def _bias_body(rb_ref, idx_ref, o_ref):
    n = rb_ref.shape[1]
    onehot = (lax.broadcasted_iota(I32, (n, idx_ref.shape[1]), 0) == idx_ref[...]).astype(F32)
    o_ref[...] = jnp.dot(rb_ref[...], onehot, preferred_element_type=F32, precision=lax.Precision.HIGHEST)


def _expand_bias(rel_bias, n_q, n_k, *, tn):
    H, R = rel_bias.shape
    RP = 3 * LANES
    t = np.arange(n_q)[:, None]
    s = np.arange(n_k)[None, :]
    idx = (np.clip(LEFT_CHUNKS * CHUNK + t - s, -REL_CLIP, REL_CLIP) + REL_CLIP).reshape(1, -1).astype(np.int32)
    out = pl.pallas_call(
        _bias_body,
        grid=(idx.shape[1] // tn,),
        in_specs=[pl.BlockSpec((H, RP), lambda j: (0, 0)), pl.BlockSpec((1, tn), lambda j: (0, j))],
        out_specs=pl.BlockSpec((H, tn), lambda j: (0, j)),
        out_shape=jax.ShapeDtypeStruct((H, idx.shape[1]), F32),
        compiler_params=_params("parallel"),
        name="rel_bias_expand",
    )(jnp.pad(rel_bias, ((0, 0), (0, RP - R))), jnp.asarray(idx))
    return out.reshape(H, n_q, n_k)


def _head_cols(ref):
    return lambda hh: ref[:, hh * HEAD_DIM:(hh + 1) * HEAD_DIM]


def _band_heads(q_ref, k_parts, v_parts, bias_ref, ok, o_ref, HB):
    def window(parts, hh):
        return jnp.concatenate([part(hh).astype(BF16) for part in parts], axis=0)

    q_head = _head_cols(q_ref)
    logits = [lax.dot_general(q_head(hh).astype(BF16), window(k_parts, hh), (((1,), (1,)), ((), ())),
                              preferred_element_type=F32) for hh in range(HB)]
    probs = []
    for hh in range(HB):
        s = logits[hh] * HEAD_DIM ** -0.5 + bias_ref[hh]
        if ok is not None:
            s = jnp.where(ok, s, NEG_BIG)
        p = jnp.exp(s - jnp.max(s, -1, keepdims=True))
        probs.append((p / jnp.sum(p, -1, keepdims=True)).astype(BF16))
    for hh in range(HB):
        o_ref[:, hh * HEAD_DIM:(hh + 1) * HEAD_DIM] = jnp.dot(
            probs[hh], window(v_parts, hh), preferred_element_type=F32).astype(o_ref.dtype)


def _band_prompt_body(q_ref, *rest, HB, TQ, NP):
    k_refs, v_refs = rest[:NP], rest[NP:2 * NP]
    bias_ref, o_ref = rest[2 * NP], rest[2 * NP + 2]
    j = pl.program_id(1)
    nk = NP * TQ
    q_pos = j * TQ + lax.broadcasted_iota(I32, (TQ, nk), 0)
    k_pos = (j - (NP - 1)) * TQ + lax.broadcasted_iota(I32, (TQ, nk), 1)
    qc, kc = q_pos >> 6, k_pos >> 6
    ok = jnp.logical_and(jnp.logical_and(k_pos >= 0, kc <= qc), kc >= qc - LEFT_CHUNKS)
    _band_heads(q_ref, [_head_cols(r) for r in k_refs], [_head_cols(r) for r in v_refs], bias_ref, ok, o_ref, HB)


def _band_prompt(qkv, bias, out_buf, *, n_blocks, HB, TQ, NP):
    W = HB * HEAD_DIM
    n_hg = H_CHK // HB

    def kv_spec(r, third):
        return pl.BlockSpec((TQ, W), lambda hg, j: (jnp.maximum(j - (NP - 1) + r, 0), third * n_hg + hg))

    return pl.pallas_call(
        functools.partial(_band_prompt_body, HB=HB, TQ=TQ, NP=NP),
        grid=(n_hg, n_blocks),
        in_specs=([pl.BlockSpec((TQ, W), lambda hg, j: (j, hg))]
                  + [kv_spec(r, 1) for r in range(NP)] + [kv_spec(r, 2) for r in range(NP)]
                  + [pl.BlockSpec((HB, TQ, NP * TQ), lambda hg, j: (hg, 0, 0)), pl.BlockSpec(memory_space=pl.ANY)]),
        out_specs=pl.BlockSpec((TQ, W), lambda hg, j: (j, hg)),
        out_shape=jax.ShapeDtypeStruct(out_buf.shape, out_buf.dtype),
        input_output_aliases={2 * NP + 2: 0},
        compiler_params=_params("parallel", "parallel"),
        name="band_prompt",
    )(qkv, *([qkv] * (2 * NP)), bias, out_buf)


def _band_sample_body(q_ref, kc_ref, kn_ref, vc_ref, vn_ref, bias_ref, _, o_ref, *, HB):
    _band_heads(q_ref, [_head_cols(kc_ref), _head_cols(kn_ref)], [_head_cols(vc_ref), _head_cols(vn_ref)], bias_ref,
                None, o_ref, HB)


def _band_sample(qkv, cache_k, cache_v, bias, out_buf, *, n_seq, T, row0, HB):
    P = cache_k.shape[1]
    W = HB * HEAD_DIM
    n_hg = H_CHK // HB
    rb0 = row0 // T
    new = lambda third: pl.BlockSpec((T, W), lambda b, hg: (rb0 + b, third * n_hg + hg))
    cache = pl.BlockSpec((None, P, W), lambda b, hg: (b, 0, hg))
    return pl.pallas_call(
        functools.partial(_band_sample_body, HB=HB),
        grid=(n_seq, n_hg),
        in_specs=[new(0), cache, new(1), cache, new(2),
                  pl.BlockSpec((HB, T, P + T), lambda b, hg: (hg, 0, 0)), pl.BlockSpec(memory_space=pl.ANY)],
        out_specs=pl.BlockSpec((T, W), lambda b, hg: (rb0 + b, hg)),
        out_shape=jax.ShapeDtypeStruct(out_buf.shape, out_buf.dtype),
        input_output_aliases={6: 0},
        compiler_params=_params("parallel", "parallel"),
        name="band_sample",
    )(qkv, cache_k, qkv, cache_v, qkv, bias, out_buf)


def _rot_tables(pos, head_dim, rot_dim, theta):
    half = rot_dim // 2
    inv = theta ** (-jnp.arange(half, dtype=F32) / half)
    ang = pos.astype(F32)[:, None] * inv[None, :]
    cos, sin = jnp.cos(ang), jnp.sin(ang)
    m = pos.shape[0]
    zh = jnp.zeros((m, half), F32)
    rest0 = jnp.zeros((m, head_dim - rot_dim), F32)
    c = jnp.concatenate([cos, cos, jnp.ones((m, head_dim - rot_dim), F32)], 1)
    s_up = jnp.concatenate([-sin, zh, rest0], 1)
    s_down = jnp.concatenate([zh, sin, rest0], 1)
    reps = LANES // head_dim
    return tuple(jnp.tile(t, (1, reps)) for t in (c, s_up, s_down))


def _log_gamma_table(width):
    lg = jnp.log1p(-(2.0 ** (-5.0 - jnp.arange(H_RET, dtype=F32))))
    return jnp.broadcast_to(lg[:, None, None], (H_RET, 8, width))


def _routing(top_e, sb, rg, row_multiple):
    n = top_e.shape[0]
    a = n * TOP_K
    n_rows = -(-((-(-a // sb) + N_EXPERTS) * sb + rg) // row_multiple) * row_multiple
    flat_e = top_e.reshape(-1)
    onehot = (flat_e[:, None] == jnp.arange(N_EXPERTS, dtype=I32)[None, :]).astype(I32)
    rank = jnp.take_along_axis(jnp.cumsum(onehot, 0), flat_e[:, None], 1)[:, 0] - 1
    counts = jnp.sum(onehot, 0)
    padded = (counts + sb - 1) // sb * sb
    pad_end = jnp.cumsum(padded)
    pad_start = pad_end - padded
    slot = pad_start[flat_e] + rank
    slot_tok = jnp.zeros((n_rows,), I32).at[slot].set(jnp.arange(a, dtype=I32) // TOP_K)

    g_count = (padded + rg - 1) // rg
    g_end = jnp.cumsum(g_count)
    g_start = g_end - g_count

    def groups(n_groups):
        s = jnp.arange(n_groups, dtype=I32)
        live = s < g_end[-1]
        expert = jnp.minimum(jnp.searchsorted(g_end, jnp.where(live, s, g_end[-1] - 1), side='right'), N_EXPERTS - 1)
        k = s - g_start[expert]
        nb = jnp.where(live, jnp.clip(padded[expert] - k * rg, 0, rg) // sb, 0)
        st = jnp.where(live, pad_start[expert] + k * rg, pad_end[-1]) // sb
        return _Groups(expert.astype(I32), st.astype(I32), nb.astype(I32), rg, sb, False)

    own_row = (g_start[flat_e] + rank // rg) * rg + rank % rg
    return own_row.reshape(n, TOP_K).astype(I32), slot_tok, pad_end[-1:].astype(I32), g_end[-1], groups


RG = 2080
RG_NARROW = 1664
TN = 512
TN_PAIR = 256
TN_DOWN = 2048
TK_DOWN = 1024
TR = 208
RG_MOE = 2560
SB_MOE = 64
RET_CHUNK = 256
RET_HR = 8
DSA_TQ = 128
DSA_TQ_S = 128
DSA_TK = 512
GATHER_TB = 256
BAND_HB = 8
BAND_TQ = 128
BAND_PARTS = (LEFT_CHUNKS * CHUNK) // BAND_TQ + 1


def kernel(x_prompt, x_sample, p_prompt, p_sample, state_ret, cache_dsa_k, cache_dsa_v, cache_dsa_kidx,
           cache_chk_k, cache_chk_v, ln_mix_g, ln_mix_b, ln_ffn_g, ln_ffn_b, ple_proj, ple_gate,
           w_in_even, ret_gn_g, kidx_ln_g, kidx_ln_b, w_out_even, ffn_w_gate, ffn_w_up, ffn_w_down,
           w_in_odd, rel_bias, w_out_odd, router_w, exp_w_gate, exp_w_up, exp_w_down):
    seq, d = x_prompt.shape[1], x_prompt.shape[2]
    nb_s, t_s = x_sample.shape[0], x_sample.shape[1]
    n_s = nb_s * t_s
    m = seq + n_s
    past = cache_dsa_k.shape[2]
    ret_w = H_RET * HEAD_DIM
    dsa_w = H_DSA * HEAD_DIM
    kv_w = KVH_DSA * HEAD_DIM

    x, xb = _stack_rows(x_prompt[0], x_sample.reshape(n_s, d), tr=n_s)
    p = jnp.concatenate([p_prompt[:, 0], p_sample.reshape(DEPTH, n_s, -1)], 1).astype(BF16)
    pos = jnp.concatenate([jnp.arange(seq, dtype=I32), past + jnp.tile(jnp.arange(t_s, dtype=I32), nb_s)])
    tab_ret = _rot_tables(pos, HEAD_DIM, HEAD_DIM, RET_THETA)
    tab_dsa = _rot_tables(pos, HEAD_DIM, ROPE_DIM, ROPE_THETA)
    tab_idx = _rot_tables(pos, D_IDX, IDX_ROPE_DIM, ROPE_THETA)

    tokens = [_dense_groups(m, RG, lead=i) for i in range(DEPTH)]
    tokens_narrow = [_dense_groups(m, RG_NARROW, lead=i, x_buffers=2) for i in range(DEPTH)]
    h = _rg_matmul(tokens[0], xb, jnp.swapaxes(w_in_even, 1, 2), tn=TN, w_transposed=True)
    c_dq = 4 * ret_w
    c_dk = c_dq + dsa_w
    c_dv = c_dk + kv_w
    c_iq = c_dv + kv_w
    c_ik = c_iq + H_IDX * D_IDX

    mix_in = jnp.zeros((m, ret_w + dsa_w), BF16)
    state0_p = jnp.zeros((1, H_RET, HEAD_DIM, HEAD_DIM), F32)
    mix_in, st_p = _retention(h, tab_ret, _log_gamma_table(RET_CHUNK), ret_gn_g[0], state0_p, mix_in,
                              C=RET_CHUNK, HR=RET_HR, n_seq=1, n_chunks=seq // RET_CHUNK, row0=0)
    mix_in, st_s = _retention(h, tab_ret, _log_gamma_table(LANES), ret_gn_g[0], state_ret[0], mix_in,
                              C=t_s, HR=RET_HR, n_seq=nb_s, n_chunks=1, row0=seq)

    (dq,) = _rotary_cols(h, c_dq // dsa_w, dsa_w, tab_dsa, ROPE_DIM // 2, [BF16], tr=TR)
    dk, dk_b = _rotary_cols(h, c_dk // kv_w, kv_w, tab_dsa, ROPE_DIM // 2, [F32, BF16], tr=TR)
    iq_half = H_IDX * D_IDX // 2
    iq, = _rotary_cols(h, c_iq // iq_half, iq_half, tab_idx, IDX_ROPE_DIM // 2, [BF16], tr=TR, n_blocks=2)
    ik, ik2, iw = _indexer_keys(h, c_ik // LANES, kidx_ln_g[0], kidx_ln_b[0], tab_idx, tr=TR)
    dv, dv_b = _copy_cols(h, c_dv // kv_w, kv_w, [F32, BF16], tr=TR)
    iwt = iw.T

    n_sel_p = min(TOPK_MAX, seq // 4)
    mix_in = _dsa(dq, iq, iwt, dk_b[:seq], dv_b[:seq], ik2[:seq], mix_in, TQ=DSA_TQ, TK=DSA_TK,
                  n_blocks=seq // DSA_TQ, n_sel=n_sel_p, q_pos0=0, q_stride=DSA_TQ, causal=True,
                  q_row0=0, out_row0=0, out_col0=ret_w, per_block_keys=False)

    def pad_queries(a):
        a = a[seq:].reshape(nb_s, t_s, -1)
        return jnp.pad(a, ((0, 0), (0, DSA_TQ_S - t_s), (0, 0))).reshape(nb_s * DSA_TQ_S, -1)

    def with_cache(cache, new):
        return jnp.concatenate([cache.reshape(nb_s, past, -1).astype(BF16), new[seq:].reshape(nb_s, t_s, -1)], 1)

    kidx_c = cache_dsa_kidx[0].astype(BF16)
    l_s = past + t_s
    do_s = _dsa(pad_queries(dq), pad_queries(iq), pad_queries(iw).T,
                with_cache(cache_dsa_k[0], dk_b), with_cache(cache_dsa_v[0], dv_b),
                with_cache(jnp.concatenate([kidx_c, kidx_c], -1), ik2),
                jnp.zeros((nb_s * DSA_TQ_S, dsa_w), BF16), TQ=DSA_TQ_S, TK=l_s, n_blocks=nb_s,
                n_sel=min(TOPK_MAX, l_s // 4), q_pos0=past, q_stride=0, causal=False,
                q_row0=0, out_row0=0, out_col0=0, per_block_keys=True)
    do_s = do_s.reshape(nb_s, DSA_TQ_S, dsa_w)[:, :t_s].reshape(n_s, dsa_w)
    mix_in = lax.dynamic_update_slice(mix_in, do_s, (seq, ret_w))

    mix = _rg_matmul(tokens_narrow[0], mix_in, w_out_even, tn=TN)
    x1, x1b = _residual_layer_norm(x, mix, ln_mix_g[0], ln_mix_b[0], tr=TR)
    act = _rg_swiglu(tokens[0], x1b, ffn_w_gate, ffn_w_up, tn=TN_PAIR)
    ffn = _rg_down(tokens[0], act, ffn_w_down, tn=TN_DOWN, tk=TK_DOWN)
    x2, x2b = _residual_layer_norm(x1, ffn, ln_ffn_g[0], ln_ffn_b[0], tr=TR)
    x3, x3b = _rg_ple(tokens_narrow[0], x2b, ple_gate, p[0], ple_proj, x2, tn=TN_PAIR)

    qkv = _rg_matmul(tokens[0], x3b, w_in_odd, tn=TN, out_dtype=BF16)
    chk_w = H_CHK * HEAD_DIM
    keep = min(LEFT_CHUNKS * CHUNK, seq)
    kv_new = _rg_matmul(_dense_groups(keep + n_s, keep + n_s), x3b[seq - keep:], w_in_odd, tn=TN,
                        stripes=(chk_w // TN, 2 * chk_w // TN))
    p_band = cache_chk_k.shape[2]
    bias_p = _expand_bias(rel_bias[0], BAND_TQ, BAND_PARTS * BAND_TQ, tn=8192)
    bias_s = _expand_bias(rel_bias[0], t_s, p_band + t_s, tn=(t_s * (p_band + t_s)) // 2)
    att = _band_prompt(qkv, bias_p, jnp.zeros((m, chk_w), BF16), n_blocks=seq // BAND_TQ, HB=BAND_HB, TQ=BAND_TQ,
                       NP=BAND_PARTS)
    att = _band_sample(qkv, cache_chk_k[0].reshape(nb_s, p_band, chk_w), cache_chk_v[0].reshape(nb_s, p_band, chk_w),
                       bias_s, att, n_seq=nb_s, T=t_s, row0=seq, HB=BAND_HB)
    mix = _rg_matmul(tokens_narrow[0], att, w_out_odd, tn=TN)
    x4, x4b = _residual_layer_norm(x3, mix, ln_mix_g[1], ln_mix_b[1], tr=TR)

    logits = _rg_matmul(tokens[0], x4b, router_w, tn=N_EXPERTS)
    top_e, gates = _router_top2(logits, tr=TR)
    tok_row, slot_tok, n_slots, n_live_groups, expert_groups = _routing(top_e, SB_MOE, RG_MOE, GATHER_TB)
    xs = _gather_rows(slot_tok, n_slots, x4, tb=GATHER_TB)

    def experts_ffn(n_groups):
        experts = expert_groups(n_groups)
        acts = _rg_swiglu(experts, xs, exp_w_gate[0], exp_w_up[0], tn=TN_PAIR)
        yb = _rg_down(experts, acts, exp_w_down[0], tn=TN_DOWN, tk=TK_DOWN)
        return _moe_combine_ln(tok_row, yb, gates, x4, ln_ffn_g[1], ln_ffn_b[1], tb=TR)

    table_sizes = sorted({N_EXPERTS, N_EXPERTS + 1, N_EXPERTS + (m * TOP_K) // RG_MOE})

    def with_table(sizes):
        if len(sizes) == 1:
            return functools.partial(experts_ffn, sizes[0])
        return lambda: lax.cond(n_live_groups <= sizes[0], functools.partial(experts_ffn, sizes[0]),
                                with_table(sizes[1:]))

    x5, x5b = with_table(table_sizes)()
    y, _ = _rg_ple(tokens_narrow[1], x5b, ple_gate, p[1], ple_proj, x5, tn=TN_PAIR)

    ck = kv_new[:, :chk_w]
    cv = kv_new[:, chk_w:]

    def heads(a, nh):
        return a.reshape(a.shape[0], nh, HEAD_DIM)

    return (y[:seq][None], y[seq:].reshape(nb_s, t_s, d),
            st_p.reshape(1, 1, H_RET, HEAD_DIM, HEAD_DIM), st_s.reshape(1, nb_s, H_RET, HEAD_DIM, HEAD_DIM),
            heads(dk[:seq], KVH_DSA)[None, None], heads(dv[:seq], KVH_DSA)[None, None], ik[:seq][None, None],
            heads(dk[seq:], KVH_DSA).reshape(1, nb_s, t_s, KVH_DSA, HEAD_DIM),
            heads(dv[seq:], KVH_DSA).reshape(1, nb_s, t_s, KVH_DSA, HEAD_DIM),
            ik[seq:].reshape(1, nb_s, t_s, D_IDX),
            heads(ck[:keep], H_CHK)[None, None], heads(cv[:keep], H_CHK)[None, None],
            heads(ck[keep:], H_CHK).reshape(1, nb_s, t_s, H_CHK, HEAD_DIM),
            heads(cv[keep:], H_CHK).reshape(1, nb_s, t_s, H_CHK, HEAD_DIM))
```
